```python
import math
import jax, jax.numpy as jnp
from jax import lax
import numpy as np

D_MODEL = 1024
BATCH = 4
SEQ = 8192
DEPTH = 2

D_MIX = D_MODEL
GLA_HEADS = 4
GLA_DV = D_MIX // 2 // GLA_HEADS
GLA_DK = GLA_DV // 2
GLA_LOWRANK = 16
GLA_TAU = 16.0
GLA_CHUNK = 64
SWA_HEADS = 8
SWA_KV_HEADS = 2
SWA_HD = (D_MIX - GLA_HEADS * GLA_DV) // SWA_HEADS
SWA_WINDOW = 128
D_FF = 2816
CONV_K = 3
EPS = 1e-6

GLA_Q = GLA_HEADS * GLA_DK
GLA_K = GLA_HEADS * GLA_DK
GLA_V = GLA_HEADS * GLA_DV
GLA_R = GLA_HEADS * GLA_DV
SWA_Q = SWA_HEADS * SWA_HD
SWA_K = SWA_KV_HEADS * SWA_HD
SWA_V = SWA_KV_HEADS * SWA_HD
SPLITS = (GLA_Q, GLA_K, GLA_V, GLA_R, GLA_LOWRANK, SWA_Q, SWA_K, SWA_V)
P_IN = sum(SPLITS)

kernel_name = "hymba_gla_swa_sink_convffn"


def alibi_slopes(n_heads):
    return np.array([2.0 ** (-8.0 * (h + 1) / n_heads) for h in range(n_heads)], dtype=np.float32)


def rms_norm(x, g):
    xf = x.astype(jnp.float32)
    y = xf * lax.rsqrt(jnp.mean(xf * xf, axis=-1, keepdims=True) + EPS)
    return (y * g.astype(jnp.float32)).astype(x.dtype)


def split_cols(z, sizes):
    idx = np.cumsum(sizes)[:-1].tolist()
    return jnp.split(z, idx, axis=-1)


def gla_chunked(q, k, v, log_a):
    B, L, H, dk = q.shape
    dv = v.shape[-1]
    C = GLA_CHUNK
    n = L // C

    def chunk(t):
        return t.astype(jnp.float32).reshape(B, n, C, H, t.shape[-1]).transpose(0, 3, 1, 2, 4)

    q, k, v, log_a = chunk(q), chunk(k), chunk(v), chunk(log_a)
    b = jnp.cumsum(log_a, axis=-2)
    b_last = b[..., -1:, :]
    qe = q * jnp.exp(b)
    ke = k * jnp.exp(-b)
    causal = jnp.tril(jnp.ones((C, C), dtype=bool))
    A = jnp.einsum('bhncd,bhnsd->bhncs', qe, ke)
    A = jnp.where(causal, A, 0.0)
    o_intra = jnp.einsum('bhncs,bhnsv->bhncv', A, v)
    kd = k * jnp.exp(b_last - b)
    chunk_state = jnp.einsum('bhncd,bhncv->bhndv', kd, v)
    decay = jnp.exp(b_last[..., 0, :])

    def step(S, inp):
        dec, cs = inp
        return S * dec[..., None] + cs, S

    S0 = jnp.zeros((B, H, dk, dv), jnp.float32)
    _, S_prev = lax.scan(step, S0, (jnp.moveaxis(decay, 2, 0), jnp.moveaxis(chunk_state, 2, 0)))
    S_prev = jnp.moveaxis(S_prev, 0, 2)
    o_inter = jnp.einsum('bhncd,bhndv->bhncv', qe, S_prev)
    o = o_intra + o_inter
    return o.transpose(0, 2, 3, 1, 4).reshape(B, L, H, dv)


def swa_sink_attention(q, k, v, sinks):
    B, L, Hq, hd = q.shape
    Hkv = k.shape[2]
    G = Hq // Hkv
    W = SWA_WINDOW
    nb = L // W
    qb = q.reshape(B, nb, W, Hkv, G, hd)
    kb = k.reshape(B, nb, W, Hkv, hd)
    vb = v.reshape(B, nb, W, Hkv, hd)
    pad = ((0, 0), (1, 0), (0, 0), (0, 0), (0, 0))
    kk = jnp.concatenate([jnp.pad(kb, pad)[:, :-1], kb], axis=2)
    vv = jnp.concatenate([jnp.pad(vb, pad)[:, :-1], vb], axis=2)
    s = jnp.einsum('bnqkgd,bnskd->bnkgqs', qb, kk,
                   preferred_element_type=jnp.float32) * (1.0 / math.sqrt(hd))
    i = jnp.arange(W)[:, None]
    j = jnp.arange(2 * W)[None, :]
    dist = (W + i - j).astype(jnp.float32)
    key_abs = (jnp.arange(nb)[:, None, None] - 1) * W + j[None]
    mask = (dist >= 0) & (dist < W) & (key_abs >= 0)
    slopes = jnp.asarray(alibi_slopes(Hq)).reshape(Hkv, G)
    s = s - slopes[:, :, None, None] * dist
    s = jnp.where(mask[None, :, None, None], s, -jnp.inf)
    sink = sinks.astype(jnp.float32).reshape(Hkv, G)[None, None, :, :, None, None]
    m = jnp.maximum(jnp.max(s, axis=-1, keepdims=True), sink)
    p = jnp.exp(s - m)
    denom = jnp.sum(p, axis=-1, keepdims=True) + jnp.exp(sink - m)
    p = p / denom
    o = jnp.einsum('bnkgqs,bnskd->bnqkgd', p, vv.astype(jnp.float32))
    return o.reshape(B, L, Hq * hd)


def causal_dwconv(u, w, b):
    K = w.shape[0]
    L = u.shape[1]
    up = jnp.pad(u, ((0, 0), (K - 1, 0), (0, 0)))
    out = up[:, 0:L] * w[0]
    for j in range(1, K):
        out = out + up[:, j:j + L] * w[j]
    return out + b


def hybrid_layer(x, mix_norm, w_in, w_alpha2, b_alpha, gla_norm, q_norm, k_norm, sinks,
                 w_out, ffn_norm, w_up, conv_w, conv_b, w_down):
    B, L, _ = x.shape
    dt = x.dtype
    h = rms_norm(x, mix_norm)
    z = h @ w_in
    gq, gk, gv, gr, glr, sq, sk, sv = split_cols(z, SPLITS)

    log_a = jax.nn.log_sigmoid((glr @ w_alpha2 + b_alpha).astype(jnp.float32)) / GLA_TAU
    gq = gq.reshape(B, L, GLA_HEADS, GLA_DK) * (GLA_DK ** -0.5)
    gk = gk.reshape(B, L, GLA_HEADS, GLA_DK)
    gv = gv.reshape(B, L, GLA_HEADS, GLA_DV)
    o_gla = gla_chunked(gq, gk, gv, log_a.reshape(B, L, GLA_HEADS, GLA_DK))
    o_gla = rms_norm(o_gla, gla_norm).reshape(B, L, GLA_V)
    o_gla = (o_gla * jax.nn.silu(gr.astype(jnp.float32))).astype(dt)

    sq = rms_norm(sq.reshape(B, L, SWA_HEADS, SWA_HD), q_norm)
    sk = rms_norm(sk.reshape(B, L, SWA_KV_HEADS, SWA_HD), k_norm)
    sv = sv.reshape(B, L, SWA_KV_HEADS, SWA_HD)
    o_swa = swa_sink_attention(sq, sk, sv, sinks).astype(dt)

    x = x + jnp.concatenate([o_gla, o_swa], axis=-1) @ w_out

    h2 = rms_norm(x, ffn_norm)
    u = causal_dwconv(h2 @ w_up, conv_w, conv_b)
    a, bval = jnp.split(u, 2, axis=-1)
    return x + (jax.nn.silu(a) * bval) @ w_down


def setup_inputs(seed: int = 0) -> dict:
    key = jax.random.key(seed)
    ks = jax.random.split(key, 16)
    f = jnp.float32

    def nrm(k, shape, scale):
        return jax.random.normal(k, shape, f) * scale

    return {
        "x": nrm(ks[0], (BATCH, SEQ, D_MODEL), 1.0),
        "mix_norm": 1.0 + nrm(ks[1], (DEPTH, D_MODEL), 0.02),
        "w_in": nrm(ks[2], (DEPTH, D_MODEL, P_IN), D_MODEL ** -0.5),
        "w_alpha2": nrm(ks[3], (DEPTH, GLA_LOWRANK, GLA_Q), GLA_LOWRANK ** -0.5),
        "b_alpha": nrm(ks[4], (DEPTH, GLA_Q), 0.1),
        "gla_norm": 1.0 + nrm(ks[5], (DEPTH, GLA_DV), 0.02),
        "q_norm": 1.0 + nrm(ks[6], (DEPTH, SWA_HD), 0.02),
        "k_norm": 1.0 + nrm(ks[7], (DEPTH, SWA_HD), 0.02),
        "sinks": nrm(ks[8], (DEPTH, SWA_HEADS), 0.5),
        "w_out": nrm(ks[9], (DEPTH, D_MIX, D_MODEL), (2.0 * D_MIX) ** -0.5),
        "ffn_norm": 1.0 + nrm(ks[10], (DEPTH, D_MODEL), 0.02),
        "w_up": nrm(ks[11], (DEPTH, D_MODEL, 2 * D_FF), D_MODEL ** -0.5),
        "conv_w": nrm(ks[12], (DEPTH, CONV_K, 2 * D_FF), CONV_K ** -0.5),
        "conv_b": nrm(ks[13], (DEPTH, 2 * D_FF), 0.02),
        "w_down": nrm(ks[14], (DEPTH, D_FF, D_MODEL), (2.0 * D_FF) ** -0.5),
    }


def reference(x, mix_norm, w_in, w_alpha2, b_alpha, gla_norm, q_norm, k_norm, sinks,
              w_out, ffn_norm, w_up, conv_w, conv_b, w_down):
    for l in range(DEPTH):
        x = hybrid_layer(x, mix_norm[l], w_in[l], w_alpha2[l], b_alpha[l], gla_norm[l],
                         q_norm[l], k_norm[l], sinks[l], w_out[l], ffn_norm[l], w_up[l],
                         conv_w[l], conv_b[l], w_down[l])
    return x
```

```python
import functools
import math

import numpy as np
import jax
import jax.numpy as jnp
from jax import lax
from jax.experimental import pallas as pl
from jax.experimental.pallas import tpu as pltpu

F32 = jnp.float32
BF16 = jnp.bfloat16

D_MODEL = 1024
GLA_HEADS = 4
GLA_DV = 128
GLA_DK = 64
GLA_LOWRANK = 16
GLA_TAU = 16.0
GLA_CHUNK = 64
SWA_HEADS = 8
SWA_KV_HEADS = 2
SWA_HD = 64
SWA_WINDOW = 128
D_FF = 2816
CONV_K = 3
EPS = 1e-6

GLA_QK = GLA_HEADS * GLA_DK
GLA_VW = GLA_HEADS * GLA_DV
SWA_QW = SWA_HEADS * SWA_HD
SWA_KW = SWA_KV_HEADS * SWA_HD

LANES = 128
SUBLANES = 8
VMEM_LIMIT = 56 * 1024 * 1024

_C_GQ = 0
_C_GK = _C_GQ + GLA_QK
_C_GV = _C_GK + GLA_QK
_C_GR = _C_GV + GLA_VW
_C_SQ = _C_GR + GLA_VW
_C_SK = _C_SQ + SWA_QW
_C_SV = _C_SK + SWA_KW
_C_LR = _C_SV + SWA_KW
P_PACK = _C_LR + LANES

SWA_HEAD_ORDER = (0, 4, 1, 5, 2, 6, 3, 7)

IN_TM = 512
GLA_TL = 256
SWA_TQ = 512
FFN_TM = 512
FFN_FC = 256
FFN_NC = D_FF // FFN_FC


def _alibi_slopes(n_heads):
    return np.array([2.0 ** (-8.0 * (h + 1) / n_heads) for h in range(n_heads)], dtype=np.float32)


def _lane_lo(shape):
    return lax.broadcasted_iota(jnp.int32, shape, len(shape) - 1) % LANES < (LANES // 2)


def _half_rms_inv(v):
    lo = _lane_lo(v.shape)
    sq = v * v
    ss_lo = jnp.sum(jnp.where(lo, sq, 0.0), axis=-1, keepdims=True)
    ss_hi = jnp.sum(jnp.where(lo, 0.0, sq), axis=-1, keepdims=True)
    inv_lo = lax.rsqrt(ss_lo * (1.0 / SWA_HD) + EPS)
    inv_hi = lax.rsqrt(ss_hi * (1.0 / SWA_HD) + EPS)
    return jnp.where(lo, inv_lo, inv_hi)


def _inproj_kernel(x_ref, g_ref, w_ref, wa_ref, ba_ref, qn_ref, kn_ref,
                   gq_ref, gk_ref, gv_ref, gr_ref, la_ref, sq_ref, sk_ref, sv_ref):
    x = x_ref[...]
    ms = jnp.mean(x * x, axis=-1, keepdims=True)
    h = (x * lax.rsqrt(ms + EPS) * g_ref[...]).astype(BF16)

    def proj(c0, width):
        return jnp.dot(h, w_ref[:, c0:c0 + width], preferred_element_type=F32)

    gq_ref[...] = proj(_C_GQ, GLA_QK) * (GLA_DK ** -0.5)
    gk_ref[...] = proj(_C_GK, GLA_QK)
    gv_ref[...] = proj(_C_GV, GLA_VW).astype(BF16)
    gr_ref[...] = proj(_C_GR, GLA_VW)

    glr = proj(_C_LR, LANES).astype(BF16)
    pre = jnp.dot(glr, wa_ref[...], preferred_element_type=F32) + ba_ref[...]
    log_sig = jnp.minimum(pre, 0.0) - jnp.log(1.0 + jnp.exp(-jnp.abs(pre)))
    la_ref[...] = log_sig * (1.0 / GLA_TAU)

    qn = qn_ref[...]
    for p in range(SWA_QW // LANES):
        q = proj(_C_SQ + p * LANES, LANES)
        sq_ref[:, p * LANES:(p + 1) * LANES] = (
            q * _half_rms_inv(q) * qn * (SWA_HD ** -0.5)).astype(BF16)
    k = proj(_C_SK, SWA_KW)
    sk_ref[...] = (k * _half_rms_inv(k) * kn_ref[...]).astype(BF16)
    sv_ref[...] = proj(_C_SV, SWA_KW).astype(BF16)


def _in_proj(x2, g, w_pack, wa_pad, ba, qn2, kn2):
    n = x2.shape[0]
    tm = min(IN_TM, n)
    row = lambda w: pl.BlockSpec((tm, w), lambda i: (i, 0))
    full = lambda a: pl.BlockSpec(a.shape, lambda i: (0,) * a.ndim)
    outs = [
        (GLA_QK, F32), (GLA_QK, F32), (GLA_VW, BF16), (GLA_VW, F32), (GLA_QK, F32),
        (SWA_QW, BF16), (SWA_KW, BF16), (SWA_KW, BF16),
    ]
    return pl.pallas_call(
        _inproj_kernel,
        grid=(n // tm,),
        in_specs=[row(D_MODEL), full(g), full(w_pack), full(wa_pad), full(ba), full(qn2), full(kn2)],
        out_specs=[row(w) for w, _ in outs],
        out_shape=[jax.ShapeDtypeStruct((n, w), dt) for w, dt in outs],
        compiler_params=pltpu.CompilerParams(
            dimension_semantics=("parallel",), vmem_limit_bytes=VMEM_LIMIT),
        name="in_proj",
    )(x2, g, w_pack, wa_pad, ba, qn2, kn2)


def _gla_kernel(q_ref, k_ref, v_ref, r_ref, la_ref, tri_ref, gn_ref, o_ref, st_ref):
    tl = q_ref.shape[0]
    C = GLA_CHUNK

    @pl.when(pl.program_id(1) == 0)
    def _():
        st_ref[...] = jnp.zeros_like(st_ref)

    la = la_ref[...]
    la_hi = la.astype(BF16)
    la_lo = (la - la_hi.astype(F32)).astype(BF16)
    tri = tri_ref[...]
    b_all = (jnp.dot(tri, la_hi, preferred_element_type=F32)
             + jnp.dot(tri, la_lo, preferred_element_type=F32))

    lo = _lane_lo((C, LANES))
    lo_sq = _lane_lo((LANES, LANES))
    rr = lax.broadcasted_iota(jnp.int32, (2 * C, C), 0) % C
    cc = lax.broadcasted_iota(jnp.int32, (2 * C, C), 1)
    causal2 = rr >= cc
    gn = gn_ref[...]
    nt = (((1,), (1,)), ((), ()))
    tn = (((0,), (0,)), ((), ()))

    for c in range(tl // C):
        rows = slice(c * C, (c + 1) * C)
        b = b_all[rows]
        b_last = b[C - 1:C]
        qe = q_ref[rows, :] * jnp.exp(b)
        kc = k_ref[rows, :]
        ke = (kc * jnp.exp(-b)).astype(BF16)
        kd = (kc * jnp.exp(b_last - b)).astype(BF16)
        dec = jnp.exp(b_last)
        for p in range(GLA_HEADS // 2):
            lanes = slice(p * LANES, (p + 1) * LANES)
            qe_p = qe[:, lanes]
            q2 = jnp.concatenate(
                [jnp.where(lo, qe_p, 0.0), jnp.where(lo, 0.0, qe_p)], axis=0).astype(BF16)
            a2 = lax.dot_general(q2, ke[:, lanes], nt, preferred_element_type=F32)
            a2 = jnp.where(causal2, a2, 0.0).astype(BF16)
            st = st_ref[p]
            inter = lax.dot_general(q2, st.astype(BF16), nt, preferred_element_type=F32)
            upd = []
            for half in range(2):
                hd = 2 * p + half
                vcol = slice(hd * GLA_DV, (hd + 1) * GLA_DV)
                v_h = v_ref[rows, vcol]
                o = (jnp.dot(a2[half * C:(half + 1) * C], v_h, preferred_element_type=F32)
                     + inter[half * C:(half + 1) * C])
                upd.append(lax.dot_general(v_h, kd[:, lanes], tn, preferred_element_type=F32))
                o = o * lax.rsqrt(jnp.mean(o * o, axis=-1, keepdims=True) + EPS) * gn
                r = r_ref[rows, vcol]
                o_ref[rows, vcol] = (o * (r / (1.0 + jnp.exp(-r)))).astype(o_ref.dtype)
            st_ref[p] = st * dec[:, lanes] + jnp.where(lo_sq, upd[0], upd[1])


def _gla(gq, gk, gv, gr, la, tri, gn, batch, seq):
    tl = min(GLA_TL, seq)
    nt = seq // tl
    row = lambda w: pl.BlockSpec((tl, w), lambda b, t: (b * nt + t, 0))
    full = lambda a: pl.BlockSpec(a.shape, lambda b, t: (0,) * a.ndim)
    return pl.pallas_call(
        _gla_kernel,
        grid=(batch, nt),
        in_specs=[row(GLA_QK), row(GLA_QK), row(GLA_VW), row(GLA_VW), row(GLA_QK), full(tri), full(gn)],
        out_specs=row(GLA_VW),
        out_shape=jax.ShapeDtypeStruct((batch * seq, GLA_VW), BF16),
        scratch_shapes=[pltpu.VMEM((GLA_HEADS // 2, GLA_DV, LANES), F32)],
        compiler_params=pltpu.CompilerParams(
            dimension_semantics=("parallel", "arbitrary"), vmem_limit_bytes=VMEM_LIMIT),
        name="gla",
    )(gq, gk, gv, gr, la, tri, gn)


def _swa_kernel(sink_ref, q_ref, kc_ref, kp_ref, vc_ref, vp_ref, bias_ref, o_ref):
    W = SWA_WINDOW
    tq = q_ref.shape[0]
    first = pl.program_id(1) == 0
    lo = _lane_lo((W, LANES))
    nt = (((1,), (1,)), ((), ()))
    for j in range(tq // W):
        rows = slice(j * W, (j + 1) * W)
        if j == 0:
            k_prev, v_prev = kp_ref[...], vp_ref[...]
        else:
            prev = slice((j - 1) * W, j * W)
            k_prev, v_prev = kc_ref[prev, :], vc_ref[prev, :]
        kk = jnp.concatenate([k_prev, kc_ref[rows, :]], axis=0)
        vv = jnp.concatenate([v_prev, vc_ref[rows, :]], axis=0)
        parts = []
        for p in range(SWA_QW // LANES):
            qp = q_ref[rows, p * LANES:(p + 1) * LANES]
            parts.append(jnp.where(lo, qp, jnp.zeros_like(qp)))
            parts.append(jnp.where(lo, jnp.zeros_like(qp), qp))
        qs = jnp.concatenate(parts, axis=0)
        s_all = lax.dot_general(qs, kk, nt, preferred_element_type=F32)
        outs = []
        for r in range(SWA_HEADS):
            head = SWA_HEAD_ORDER[r]
            if j == 0:
                table = jnp.where(first, SWA_HEADS + head, head)
            else:
                table = head
            s = s_all[r * W:(r + 1) * W] + bias_ref[table]
            sink = sink_ref[head]
            m = jnp.maximum(jnp.max(s, axis=-1, keepdims=True), sink)
            e = jnp.exp(s - m)
            denom = jnp.sum(e, axis=-1, keepdims=True) + jnp.exp(sink - m)
            o = jnp.dot(e.astype(BF16), vv, preferred_element_type=F32)
            outs.append(o / denom)
        for p in range(SWA_QW // LANES):
            o_ref[rows, p * LANES:(p + 1) * LANES] = jnp.where(
                lo, outs[2 * p], outs[2 * p + 1]).astype(o_ref.dtype)


def _swa(sinks, sq, sk, sv, bias, batch, seq):
    W = SWA_WINDOW
    tq = min(SWA_TQ, seq)
    nq = seq // tq
    bpt = tq // W
    bps = seq // W
    cur = lambda w: pl.BlockSpec((tq, w), lambda b, i: (b * nq + i, 0))
    prev = pl.BlockSpec((W, SWA_KW), lambda b, i: (b * bps + jnp.maximum(i * bpt - 1, 0), 0))
    return pl.pallas_call(
        _swa_kernel,
        grid=(batch, nq),
        in_specs=[
            pl.BlockSpec(memory_space=pltpu.SMEM),
            cur(SWA_QW), cur(SWA_KW), prev, cur(SWA_KW), prev,
            pl.BlockSpec(bias.shape, lambda b, i: (0, 0, 0)),
        ],
        out_specs=cur(SWA_QW),
        out_shape=jax.ShapeDtypeStruct((batch * seq, SWA_QW), BF16),
        compiler_params=pltpu.CompilerParams(
            dimension_semantics=("parallel", "parallel"), vmem_limit_bytes=VMEM_LIMIT),
        name="swa",
    )(sinks, sq, sk, sk, sv, sv, bias)


def _swa_bias_tables():
    W = SWA_WINDOW
    i = np.arange(W)[:, None]
    j = np.arange(2 * W)[None, :]
    dist = (W + i - j).astype(np.float32)
    valid = (dist >= 0) & (dist < W)
    slopes = _alibi_slopes(SWA_HEADS)
    base = np.where(valid[None], -slopes[:, None, None] * dist[None], -np.inf).astype(np.float32)
    first = np.where((j >= W)[None], base, -np.inf).astype(np.float32)
    return np.concatenate([base, first], axis=0)


def _ffn_kernel(x_ref, og_ref, os_ref, wo_ref, g_ref, wu_ref, cw_ref, wd_ref, y_ref,
                x1_ref, h_ref, acc_ref, halo_ref):
    tm = x_ref.shape[0]
    first = pl.program_id(1) == 0

    x1 = (x_ref[...]
          + jnp.dot(og_ref[...], wo_ref[:GLA_VW, :], preferred_element_type=F32)
          + jnp.dot(os_ref[...], wo_ref[GLA_VW:, :], preferred_element_type=F32))
    x1_ref[...] = x1
    ms = jnp.mean(x1 * x1, axis=-1, keepdims=True)
    h_ref[...] = (x1 * lax.rsqrt(ms + EPS) * g_ref[...]).astype(BF16)
    acc_ref[...] = jnp.zeros_like(acc_ref)

    row8 = lax.broadcasted_iota(jnp.int32, (SUBLANES, 2 * FFN_FC), 0)

    def chunk(c, carry):
        u = jnp.dot(h_ref[...], wu_ref[c], preferred_element_type=F32)
        halo = jnp.where(first, 0.0, halo_ref[c])
        halo_ref[c] = u[tm - SUBLANES:, :]
        cw = cw_ref[c]
        r1 = pltpu.roll(u, 1, axis=0)
        r2 = pltpu.roll(u, 2, axis=0)
        h1 = pltpu.roll(halo, 1, axis=0)
        h2 = pltpu.roll(halo, 2, axis=0)
        u1 = jnp.concatenate([jnp.where(row8 < 1, h1, r1[:SUBLANES]), r1[SUBLANES:]], axis=0)
        u2 = jnp.concatenate([jnp.where(row8 < 2, h2, r2[:SUBLANES]), r2[SUBLANES:]], axis=0)
        y = u2 * cw[0:1] + u1 * cw[1:2] + u * cw[2:3] + cw[3:4]
        a = y[:, :FFN_FC]
        gate = (a / (1.0 + jnp.exp(-a))) * y[:, FFN_FC:]
        acc_ref[...] += jnp.dot(gate.astype(BF16), wd_ref[c], preferred_element_type=F32)
        return carry

    lax.fori_loop(0, FFN_NC, chunk, 0)
    y_ref[...] = x1_ref[...] + acc_ref[...]


def _out_ffn(x2, o_gla, o_swa, wo, g, wu, cw, wd, batch, seq):
    tm = min(FFN_TM, seq)
    nt = seq // tm
    row = lambda w: pl.BlockSpec((tm, w), lambda b, t: (b * nt + t, 0))
    full = lambda a: pl.BlockSpec(a.shape, lambda b, t: (0,) * a.ndim,
                                  pipeline_mode=pl.Buffered(1))
    return pl.pallas_call(
        _ffn_kernel,
        grid=(batch, nt),
        in_specs=[row(D_MODEL), row(GLA_VW), row(SWA_QW), full(wo), full(g), full(wu), full(cw), full(wd)],
        out_specs=row(D_MODEL),
        out_shape=jax.ShapeDtypeStruct((batch * seq, D_MODEL), F32),
        scratch_shapes=[
            pltpu.VMEM((tm, D_MODEL), F32),
            pltpu.VMEM((tm, D_MODEL), BF16),
            pltpu.VMEM((tm, D_MODEL), F32),
            pltpu.VMEM((FFN_NC, SUBLANES, 2 * FFN_FC), F32),
        ],
        compiler_params=pltpu.CompilerParams(
            dimension_semantics=("parallel", "arbitrary"), vmem_limit_bytes=VMEM_LIMIT),
        name="out_ffn",
    )(x2, o_gla, o_swa, wo, g, wu, cw, wd)


def _pack_layer(mix_norm, w_in, w_alpha2, b_alpha, gla_norm, q_norm, k_norm, sinks,
                w_out, ffn_norm, w_up, conv_w, conv_b, w_down):
    o_gq = 0
    o_gk = o_gq + GLA_QK
    o_gv = o_gk + GLA_QK
    o_gr = o_gv + GLA_VW
    o_lr = o_gr + GLA_VW
    o_sq = o_lr + GLA_LOWRANK
    o_sk = o_sq + SWA_QW
    o_sv = o_sk + SWA_KW
    sq_cols = np.concatenate([o_sq + h * SWA_HD + np.arange(SWA_HD) for h in SWA_HEAD_ORDER])
    w_pack = jnp.concatenate([
        w_in[:, o_gq:o_lr],
        w_in[:, sq_cols],
        w_in[:, o_sk:o_sv + SWA_KW],
        w_in[:, o_lr:o_lr + GLA_LOWRANK],
        jnp.zeros((D_MODEL, LANES - GLA_LOWRANK), w_in.dtype),
    ], axis=1).astype(BF16)
    wa_pad = jnp.concatenate(
        [w_alpha2, jnp.zeros((LANES - GLA_LOWRANK, GLA_QK), w_alpha2.dtype)], axis=0).astype(BF16)

    swa_rows = np.concatenate([GLA_VW + h * SWA_HD + np.arange(SWA_HD) for h in SWA_HEAD_ORDER])
    wo = jnp.concatenate([w_out[:GLA_VW], w_out[swa_rows]], axis=0).astype(BF16)

    wu = jnp.concatenate([
        w_up[:, :D_FF].reshape(D_MODEL, FFN_NC, FFN_FC),
        w_up[:, D_FF:].reshape(D_MODEL, FFN_NC, FFN_FC)], axis=2)
    wu = wu.transpose(1, 0, 2).astype(BF16)
    taps = jnp.concatenate([conv_w, conv_b[None]], axis=0)
    taps = jnp.concatenate([
        taps[:, :D_FF].reshape(CONV_K + 1, FFN_NC, FFN_FC),
        taps[:, D_FF:].reshape(CONV_K + 1, FFN_NC, FFN_FC)], axis=2).transpose(1, 0, 2)
    cw = jnp.concatenate(
        [taps, jnp.zeros((FFN_NC, SUBLANES - CONV_K - 1, 2 * FFN_FC), taps.dtype)], axis=1)
    wd = w_down.reshape(FFN_NC, FFN_FC, D_MODEL).astype(BF16)
    return dict(
        mix_norm=mix_norm.reshape(1, D_MODEL), w_pack=w_pack, wa_pad=wa_pad,
        b_alpha=b_alpha.reshape(1, GLA_QK), gla_norm=gla_norm.reshape(1, GLA_DV),
        qn2=jnp.tile(q_norm, 2).reshape(1, LANES), kn2=jnp.tile(k_norm, 2).reshape(1, LANES),
        sinks=sinks, wo=wo, ffn_norm=ffn_norm.reshape(1, D_MODEL), wu=wu, cw=cw, wd=wd)


def kernel(x, mix_norm, w_in, w_alpha2, b_alpha, gla_norm, q_norm, k_norm, sinks, w_out, ffn_norm,
           w_up, conv_w, conv_b, w_down):
    batch, seq, d = x.shape
    assert d == D_MODEL and seq % max(GLA_TL, SWA_TQ, FFN_TM, IN_TM) == 0
    depth = w_in.shape[0]
    tl = min(GLA_TL, seq)
    pos = np.arange(tl)
    tri = jnp.asarray(
        (pos[:, None] >= pos[None, :]) & (pos[:, None] // GLA_CHUNK == pos[None, :] // GLA_CHUNK),
        dtype=BF16)
    bias = jnp.asarray(_swa_bias_tables())

    x2 = x.reshape(batch * seq, D_MODEL)
    for l in range(depth):
        p = _pack_layer(mix_norm[l], w_in[l], w_alpha2[l], b_alpha[l], gla_norm[l], q_norm[l],
                        k_norm[l], sinks[l], w_out[l], ffn_norm[l], w_up[l], conv_w[l], conv_b[l],
                        w_down[l])
        gq, gk, gv, gr, la, sq, sk, sv = _in_proj(
            x2, p["mix_norm"], p["w_pack"], p["wa_pad"], p["b_alpha"], p["qn2"], p["kn2"])
        o_gla = _gla(gq, gk, gv, gr, la, tri, p["gla_norm"], batch, seq)
        o_swa = _swa(p["sinks"], sq, sk, sv, bias, batch, seq)
        x2 = _out_ffn(x2, o_gla, o_swa, p["wo"], p["ffn_norm"], p["wu"], p["cw"], p["wd"],
                      batch, seq)
    return x2.reshape(batch, seq, D_MODEL)
```

```python
import functools

import numpy as np
import jax
import jax.numpy as jnp
from jax import lax
from jax.experimental import pallas as pl
from jax.experimental.pallas import tpu as pltpu

F32 = jnp.float32
BF16 = jnp.bfloat16

D_MODEL = 1024
GLA_HEADS = 4
GLA_DV = 128
GLA_DK = 64
GLA_LOWRANK = 16
GLA_TAU = 16.0
GLA_CHUNK = 64
SWA_HEADS = 8
SWA_KV_HEADS = 2
SWA_HD = 64
SWA_WINDOW = 128
D_FF = 2816
CONV_K = 3
EPS = 1e-6

GLA_QK = GLA_HEADS * GLA_DK
GLA_VW = GLA_HEADS * GLA_DV
SWA_QW = SWA_HEADS * SWA_HD
SWA_KW = SWA_KV_HEADS * SWA_HD
SWA_GROUP = SWA_HEADS // SWA_KV_HEADS

LANES = 128
SUBLANES = 8
VMEM_LIMIT = 56 * 1024 * 1024

_R_LR = 2 * GLA_QK + 2 * GLA_VW
_R_SQ = _R_LR + GLA_LOWRANK
P_IN = _R_SQ + SWA_QW + 2 * SWA_KW
_C_GQ = 0
_C_GK = _C_GQ + GLA_QK
_C_GV = _C_GK + GLA_QK
_C_GR = _C_GV + GLA_VW
_C_SQ = _C_GR + GLA_VW
_C_SK = _C_SQ + SWA_QW
_C_SV = _C_SK + SWA_KW
_C_LR = _C_SV + SWA_KW
P_PACK = _C_LR + LANES

IN_TM = 512
GLA_TL = 256
SWA_TQ = 512
FFN_TM = 512
FFN_FC = 256
FFN_NC = D_FF // FFN_FC
FFN_DOWN_GROUP = 4
FFN_RB = 64


def _alibi_slopes(n_heads):
    return np.array([2.0 ** (-8.0 * (h + 1) / n_heads) for h in range(n_heads)], dtype=np.float32)


def _lane_lo(shape):
    return lax.broadcasted_iota(jnp.int32, shape, len(shape) - 1) % LANES < (LANES // 2)


def _half_rms_inv(v):
    lo = _lane_lo(v.shape)
    sq = v * v
    ss_lo = jnp.sum(jnp.where(lo, sq, 0.0), axis=-1, keepdims=True)
    ss_hi = jnp.sum(jnp.where(lo, 0.0, sq), axis=-1, keepdims=True)
    inv_lo = lax.rsqrt(ss_lo * (1.0 / SWA_HD) + EPS)
    inv_hi = lax.rsqrt(ss_hi * (1.0 / SWA_HD) + EPS)
    return jnp.where(lo, inv_lo, inv_hi)


def _dup_halves(v):
    lo = _lane_lo(v.shape)
    swapped = pltpu.roll(v, LANES // 2, axis=1)
    return jnp.concatenate([jnp.where(lo, v, swapped), jnp.where(lo, swapped, v)], axis=1)


def _inproj_kernel(x_ref, g_ref, w_ref, wa_ref, ba_ref, qn_ref, kn_ref,
                   gq_ref, gk_ref, gv_ref, gr_ref, la_ref, sq_ref, sk_ref, sv_ref):
    x = x_ref[...]
    ms = jnp.mean(x * x, axis=-1, keepdims=True)
    h = (x * lax.rsqrt(ms + EPS) * g_ref[...]).astype(BF16)

    def proj(c0, width):
        return jnp.dot(h, w_ref[:, c0:c0 + width], preferred_element_type=F32)

    gq_ref[...] = proj(_C_GQ, GLA_QK) * (GLA_DK ** -0.5)
    gk_ref[...] = proj(_C_GK, GLA_QK)
    gv_ref[...] = proj(_C_GV, GLA_VW).astype(BF16)
    gr_ref[...] = proj(_C_GR, GLA_VW)

    glr = proj(_C_LR, LANES).astype(BF16)
    pre = jnp.dot(glr, wa_ref[...], preferred_element_type=F32) + ba_ref[...]
    log_sig = jnp.minimum(pre, 0.0) - jnp.log(1.0 + jnp.exp(-jnp.abs(pre)))
    la_ref[...] = log_sig * (1.0 / GLA_TAU)

    qn = qn_ref[...]
    for p in range(SWA_QW // LANES):
        q = proj(_C_SQ + p * LANES, LANES)
        sq_ref[:, p * LANES:(p + 1) * LANES] = (
            q * _half_rms_inv(q) * qn * (SWA_HD ** -0.5)).astype(BF16)
    k = proj(_C_SK, SWA_KW)
    sk_ref[...] = _dup_halves(k * _half_rms_inv(k) * kn_ref[...]).astype(BF16)
    sv_ref[...] = _dup_halves(proj(_C_SV, SWA_KW)).astype(BF16)


def _in_proj(x2, g, w_pack, wa_pad, ba, qn2, kn2, layer):
    n = x2.shape[0]
    tm = min(IN_TM, n)
    row = lambda w: pl.BlockSpec((tm, w), lambda i: (i, 0))
    lay = lambda a: pl.BlockSpec((None,) + a.shape[1:], lambda i: (layer,) + (0,) * (a.ndim - 1))
    outs = [
        (GLA_QK, F32), (GLA_QK, F32), (GLA_VW, BF16), (GLA_VW, F32), (GLA_QK, F32),
        (SWA_QW, BF16), (2 * SWA_KW, BF16), (2 * SWA_KW, BF16),
    ]
    return pl.pallas_call(
        _inproj_kernel,
        grid=(n // tm,),
        in_specs=[row(D_MODEL), lay(g), lay(w_pack), lay(wa_pad), lay(ba), lay(qn2), lay(kn2)],
        out_specs=[row(w) for w, _ in outs],
        out_shape=[jax.ShapeDtypeStruct((n, w), dt) for w, dt in outs],
        compiler_params=pltpu.CompilerParams(
            dimension_semantics=("parallel",), vmem_limit_bytes=VMEM_LIMIT),
        name="in_proj",
    )(x2, g, w_pack, wa_pad, ba, qn2, kn2)


def _gla_kernel(q_ref, k_ref, v_ref, r_ref, la_ref, tri_ref, gn_ref, o_ref, st_ref):
    tl = q_ref.shape[0]
    C = GLA_CHUNK

    @pl.when(pl.program_id(1) == 0)
    def _():
        st_ref[...] = jnp.zeros_like(st_ref)

    la = la_ref[...]
    la_hi = la.astype(BF16)
    la_lo = (la - la_hi.astype(F32)).astype(BF16)
    tri = tri_ref[...]
    b_all = (jnp.dot(tri, la_hi, preferred_element_type=F32)
             + jnp.dot(tri, la_lo, preferred_element_type=F32))

    lo = _lane_lo((C, LANES))
    lo_sq = _lane_lo((LANES, LANES))
    rr = lax.broadcasted_iota(jnp.int32, (2 * C, C), 0) % C
    cc = lax.broadcasted_iota(jnp.int32, (2 * C, C), 1)
    causal2 = rr >= cc
    gn = gn_ref[...]
    nt = (((1,), (1,)), ((), ()))
    tn = (((0,), (0,)), ((), ()))

    for c in range(tl // C):
        rows = slice(c * C, (c + 1) * C)
        b = b_all[rows]
        b_last = b[C - 1:C]
        qe = q_ref[rows, :] * jnp.exp(b)
        kc = k_ref[rows, :]
        ke = (kc * jnp.exp(-b)).astype(BF16)
        kd = (kc * jnp.exp(b_last - b)).astype(BF16)
        dec = jnp.exp(b_last)
        for p in range(GLA_HEADS // 2):
            lanes = slice(p * LANES, (p + 1) * LANES)
            qe_p = qe[:, lanes]
            q2 = jnp.concatenate(
                [jnp.where(lo, qe_p, 0.0), jnp.where(lo, 0.0, qe_p)], axis=0).astype(BF16)
            a2 = lax.dot_general(q2, ke[:, lanes], nt, preferred_element_type=F32)
            a2 = jnp.where(causal2, a2, 0.0).astype(BF16)
            st = st_ref[p]
            inter = lax.dot_general(q2, st.astype(BF16), nt, preferred_element_type=F32)
            upd = []
            for half in range(2):
                hd = 2 * p + half
                vcol = slice(hd * GLA_DV, (hd + 1) * GLA_DV)
                v_h = v_ref[rows, vcol]
                o = (jnp.dot(a2[half * C:(half + 1) * C], v_h, preferred_element_type=F32)
                     + inter[half * C:(half + 1) * C])
                upd.append(lax.dot_general(v_h, kd[:, lanes], tn, preferred_element_type=F32))
                o = o * lax.rsqrt(jnp.mean(o * o, axis=-1, keepdims=True) + EPS) * gn
                r = r_ref[rows, vcol]
                o_ref[rows, vcol] = (o * (r / (1.0 + jnp.exp(-r)))).astype(o_ref.dtype)
            st_ref[p] = st * dec[:, lanes] + jnp.where(lo_sq, upd[0], upd[1])


def _gla(gq, gk, gv, gr, la, tri, gn, layer, batch, seq):
    tl = min(GLA_TL, seq)
    nt = seq // tl
    row = lambda w: pl.BlockSpec((tl, w), lambda b, t: (b * nt + t, 0))
    return pl.pallas_call(
        _gla_kernel,
        grid=(batch, nt),
        in_specs=[row(GLA_QK), row(GLA_QK), row(GLA_VW), row(GLA_VW), row(GLA_QK),
                  pl.BlockSpec(tri.shape, lambda b, t: (0, 0)),
                  pl.BlockSpec((None, 1, GLA_DV), lambda b, t: (layer, 0, 0))],
        out_specs=row(GLA_VW),
        out_shape=jax.ShapeDtypeStruct((batch * seq, GLA_VW), BF16),
        scratch_shapes=[pltpu.VMEM((GLA_HEADS // 2, GLA_DV, LANES), F32)],
        compiler_params=pltpu.CompilerParams(
            dimension_semantics=("parallel", "arbitrary"), vmem_limit_bytes=VMEM_LIMIT),
        name="gla",
    )(gq, gk, gv, gr, la, tri, gn)


def _swa_kernel(sink_ref, q_ref, kc_ref, kp_ref, vc_ref, vp_ref, bias_ref, o_ref, *, layer):
    W = SWA_WINDOW
    tq = q_ref.shape[0]
    first = pl.program_id(1) == 0
    lo = _lane_lo((W, LANES))
    nt = (((1,), (1,)), ((), ()))
    pairs_per_kv = SWA_GROUP // 2
    for j in range(tq // W):
        rows = slice(j * W, (j + 1) * W)
        if j == 0:
            k_prev, v_prev = kp_ref[...], vp_ref[...]
        else:
            prev = slice((j - 1) * W, j * W)
            k_prev, v_prev = kc_ref[prev, :], vc_ref[prev, :]
        kk = jnp.concatenate([k_prev, kc_ref[rows, :]], axis=0)
        vv = jnp.concatenate([v_prev, vc_ref[rows, :]], axis=0)
        outs = []
        for kv in range(SWA_KV_HEADS):
            kv_lanes = slice(kv * LANES, (kv + 1) * LANES)
            parts = []
            for p in range(kv * pairs_per_kv, (kv + 1) * pairs_per_kv):
                qp = q_ref[rows, p * LANES:(p + 1) * LANES]
                parts.append(jnp.where(lo, qp, jnp.zeros_like(qp)))
                parts.append(jnp.where(lo, jnp.zeros_like(qp), qp))
            qs = jnp.concatenate(parts, axis=0)
            s_all = lax.dot_general(qs, kk[:, kv_lanes], nt, preferred_element_type=F32)
            for r in range(SWA_GROUP):
                head = kv * SWA_GROUP + r
                if j == 0:
                    table = jnp.where(first, SWA_HEADS + head, head)
                else:
                    table = head
                s = s_all[r * W:(r + 1) * W] + bias_ref[table]
                sink = sink_ref[layer, head]
                m = jnp.maximum(jnp.max(s, axis=-1, keepdims=True), sink)
                e = jnp.exp(s - m)
                denom = jnp.sum(e, axis=-1, keepdims=True) + jnp.exp(sink - m)
                o = jnp.dot(e.astype(BF16), vv[:, kv_lanes], preferred_element_type=F32)
                outs.append(o / denom)
        for p in range(SWA_QW // LANES):
            o_ref[rows, p * LANES:(p + 1) * LANES] = jnp.where(
                lo, outs[2 * p], outs[2 * p + 1]).astype(o_ref.dtype)


def _swa(sinks, sq, sk, sv, bias, layer, batch, seq):
    W = SWA_WINDOW
    tq = min(SWA_TQ, seq)
    nq = seq // tq
    bpt = tq // W
    bps = seq // W
    cur = lambda w: pl.BlockSpec((tq, w), lambda b, i: (b * nq + i, 0))
    prev = pl.BlockSpec((W, 2 * SWA_KW), lambda b, i: (b * bps + jnp.maximum(i * bpt - 1, 0), 0))
    return pl.pallas_call(
        functools.partial(_swa_kernel, layer=layer),
        grid=(batch, nq),
        in_specs=[
            pl.BlockSpec(memory_space=pltpu.SMEM),
            cur(SWA_QW), cur(2 * SWA_KW), prev, cur(2 * SWA_KW), prev,
            pl.BlockSpec(bias.shape, lambda b, i: (0, 0, 0)),
        ],
        out_specs=cur(SWA_QW),
        out_shape=jax.ShapeDtypeStruct((batch * seq, SWA_QW), BF16),
        compiler_params=pltpu.CompilerParams(
            dimension_semantics=("parallel", "parallel"), vmem_limit_bytes=VMEM_LIMIT),
        name="swa",
    )(sinks, sq, sk, sk, sv, sv, bias)


def _swa_bias_tables():
    W = SWA_WINDOW
    i = np.arange(W)[:, None]
    j = np.arange(2 * W)[None, :]
    dist = (W + i - j).astype(np.float32)
    valid = (dist >= 0) & (dist < W)
    slopes = _alibi_slopes(SWA_HEADS)
    base = np.where(valid[None], -slopes[:, None, None] * dist[None], -np.inf).astype(np.float32)
    first = np.where((j >= W)[None], base, -np.inf).astype(np.float32)
    return np.concatenate([base, first], axis=0)


def _ffn_kernel(x_ref, og_ref, os_ref, wo_ref, g_ref, wu_ref, cw_ref, wd_ref, y_ref,
                x1_ref, h_ref, t3_ref, ubuf_ref, gbuf_ref, halo_ref):
    tm = x_ref.shape[0]
    S = SUBLANES
    nv = tm // S
    nb = nv // S
    ng = D_MODEL // LANES
    first = pl.program_id(1) == 0

    x1 = (x_ref[...]
          + jnp.dot(og_ref[...], wo_ref[:GLA_VW, :], preferred_element_type=F32)
          + jnp.dot(os_ref[...], wo_ref[GLA_VW:, :], preferred_element_type=F32))
    x1_ref[...] = x1
    ms = jnp.mean(x1 * x1, axis=-1, keepdims=True)
    hn = x1 * lax.rsqrt(ms + EPS) * g_ref[...]
    for a in range(S):
        for b in range(nb):
            src = slice(S * (nb * a + b), S * (nb * a + b) + S)
            dst = slice(S * (S * b + a), S * (S * b + a) + S)
            for g in range(ng):
                t3_ref[g, dst, :] = hn[src, g * LANES:(g + 1) * LANES]
    for k in range(nv // 2):
        rows = []
        for v in (2 * k, 2 * k + 1):
            b, c = v // S, v % S
            rows.append(jnp.concatenate(
                [t3_ref[g, pl.ds(S * S * b + c, S, stride=S), :] for g in range(ng)], axis=1))
        h_ref[2 * S * k:2 * S * (k + 1), :] = jnp.concatenate(rows, axis=0).astype(BF16)

    sub = lax.broadcasted_iota(jnp.int32, (S, 2 * FFN_FC), 0)

    def up_stage(c):
        slot = c % 2
        h = h_ref[...]
        u = jnp.concatenate([
            jnp.dot(h, wu_ref[:, c * FFN_FC:(c + 1) * FFN_FC], preferred_element_type=F32),
            jnp.dot(h, wu_ref[:, D_FF + c * FFN_FC:D_FF + (c + 1) * FFN_FC],
                    preferred_element_type=F32)], axis=1)
        halo = jnp.where(first, 0.0, halo_ref[c])
        halo_ref[c] = u[tm - 2 * S:, :]
        fix2 = jnp.where(sub == 0, pltpu.roll(halo[:S], 1, axis=0),
                         pltpu.roll(u[tm - 2 * S:tm - S], 1, axis=0))
        fix1 = jnp.where(sub == 0, pltpu.roll(halo[S:], 1, axis=0),
                         pltpu.roll(u[tm - S:], 1, axis=0))
        ubuf_ref[slot, 0:S, :] = fix2
        ubuf_ref[slot, S:2 * S, :] = fix1
        ubuf_ref[slot, 2 * S:2 * S + tm, :] = u

    def gate_stage(c):
        slot = c % 2
        ca = slice(c * FFN_FC, (c + 1) * FFN_FC)
        cb = slice(D_FF + c * FFN_FC, D_FF + (c + 1) * FFN_FC)
        cw = jnp.concatenate([cw_ref[:, ca], cw_ref[:, cb]], axis=1)
        for r in range(0, tm, FFN_RB):
            u0 = ubuf_ref[slot, 2 * S + r:2 * S + r + FFN_RB, :]
            u1 = ubuf_ref[slot, S + r:S + r + FFN_RB, :]
            u2 = ubuf_ref[slot, r:r + FFN_RB, :]
            y = u2 * cw[0:1] + u1 * cw[1:2] + u0 * cw[2:3] + cw[3:4]
            a = y[:, :FFN_FC]
            gate = (a / (1.0 + jnp.exp(-a))) * y[:, FFN_FC:]
            gbuf_ref[r:r + FFN_RB, ca] = gate.astype(BF16)

    def down_stage(c0, c1):
        k0, k1 = c0 * FFN_FC, c1 * FFN_FC
        d = jnp.dot(gbuf_ref[:, k0:k1], wd_ref[k0:k1, :], preferred_element_type=F32)
        for g in range(ng):
            if c0 == 0:
                t3_ref[g] = d[:, g * LANES:(g + 1) * LANES]
            else:
                t3_ref[g] += d[:, g * LANES:(g + 1) * LANES]

    up_stage(0)
    group_start = 0
    for c in range(FFN_NC):
        if c + 1 < FFN_NC:
            up_stage(c + 1)
        gate_stage(c)
        if c + 1 - group_start == FFN_DOWN_GROUP or c + 1 == FFN_NC:
            down_stage(group_start, c + 1)
            group_start = c + 1

    for a in range(S):
        for b in range(nb):
            rows = slice(S * (nb * a + b), S * (nb * a + b) + S)
            ffn = jnp.concatenate(
                [t3_ref[g, pl.ds(S * S * b + a, S, stride=S), :] for g in range(ng)], axis=1)
            y_ref[rows, :] = x1_ref[rows, :] + ffn


def _out_ffn(x2, o_gla, o_swa, wo, g, wu, cw, wd, layer, batch, seq):
    tm = min(FFN_TM, seq)
    assert tm % (SUBLANES * SUBLANES) == 0
    nt = seq // tm
    row = lambda w: pl.BlockSpec((tm, w), lambda b, t: (b * nt + t, 0))
    lay = lambda a: pl.BlockSpec((None,) + a.shape[1:], lambda b, t: (layer,) + (0,) * (a.ndim - 1),
                                 pipeline_mode=pl.Buffered(1))
    return pl.pallas_call(
        _ffn_kernel,
        grid=(batch, nt),
        in_specs=[row(D_MODEL), row(GLA_VW), row(SWA_QW), lay(wo), lay(g), lay(wu), lay(cw), lay(wd)],
        out_specs=row(D_MODEL),
        out_shape=jax.ShapeDtypeStruct((batch * seq, D_MODEL), F32),
        scratch_shapes=[
            pltpu.VMEM((tm, D_MODEL), F32),
            pltpu.VMEM((tm, D_MODEL), BF16),
            pltpu.VMEM((D_MODEL // LANES, tm, LANES), F32),
            pltpu.VMEM((2, tm + 2 * SUBLANES, 2 * FFN_FC), F32),
            pltpu.VMEM((tm, D_FF), BF16),
            pltpu.VMEM((FFN_NC, 2 * SUBLANES, 2 * FFN_FC), F32),
        ],
        compiler_params=pltpu.CompilerParams(
            dimension_semantics=("parallel", "arbitrary"), vmem_limit_bytes=VMEM_LIMIT),
        name="out_ffn",
    )(x2, o_gla, o_swa, wo, g, wu, cw, wd)


def kernel(x, mix_norm, w_in, w_alpha2, b_alpha, gla_norm, q_norm, k_norm, sinks, w_out, ffn_norm,
           w_up, conv_w, conv_b, w_down):
    batch, seq, d = x.shape
    depth = w_in.shape[0]
    assert d == D_MODEL and w_in.shape[2] == P_IN
    assert seq % max(GLA_TL, SWA_TQ, FFN_TM) == 0 and (batch * seq) % IN_TM == 0

    w_pack = jnp.concatenate([
        w_in[:, :, :_R_LR], w_in[:, :, _R_SQ:], w_in[:, :, _R_LR:_R_SQ],
        jnp.zeros((depth, D_MODEL, LANES - GLA_LOWRANK), w_in.dtype)], axis=2).astype(BF16)
    wa_pad = jnp.concatenate(
        [w_alpha2, jnp.zeros((depth, LANES - GLA_LOWRANK, GLA_QK), w_alpha2.dtype)],
        axis=1).astype(BF16)
    wo = w_out.astype(BF16)
    wu = w_up.astype(BF16)
    wd = w_down.astype(BF16)
    cw = jnp.concatenate([
        conv_w, conv_b[:, None, :],
        jnp.zeros((depth, SUBLANES - CONV_K - 1, 2 * D_FF), conv_w.dtype)], axis=1)
    row3 = lambda a: a.reshape(depth, 1, a.shape[-1])
    qn2 = row3(jnp.tile(q_norm, (1, 2)))
    kn2 = row3(jnp.tile(k_norm, (1, 2)))

    tl = min(GLA_TL, seq)
    pos = np.arange(tl)
    tri = jnp.asarray(
        (pos[:, None] >= pos[None, :]) & (pos[:, None] // GLA_CHUNK == pos[None, :] // GLA_CHUNK),
        dtype=BF16)
    bias = jnp.asarray(_swa_bias_tables())

    x2 = x.reshape(batch * seq, D_MODEL)
    for l in range(depth):
        gq, gk, gv, gr, la, sq, sk, sv = _in_proj(
            x2, row3(mix_norm), w_pack, wa_pad, row3(b_alpha), qn2, kn2, l)
        o_gla = _gla(gq, gk, gv, gr, la, tri, row3(gla_norm), l, batch, seq)
        o_swa = _swa(sinks, sq, sk, sv, bias, l, batch, seq)
        x2 = _out_ffn(x2, o_gla, o_swa, wo, row3(ffn_norm), wu, cw, wd, l, batch, seq)
    return x2.reshape(batch, seq, D_MODEL)
```

```python
import functools

import numpy as np
import jax
import jax.numpy as jnp
from jax import lax
from jax.experimental import pallas as pl
from jax.experimental.pallas import tpu as pltpu

F32 = jnp.float32
BF16 = jnp.bfloat16

D_MODEL = 1024
GLA_HEADS = 4
GLA_DV = 128
GLA_DK = 64
GLA_LOWRANK = 16
GLA_TAU = 16.0
GLA_CHUNK = 64
SWA_HEADS = 8
SWA_KV_HEADS = 2
SWA_HD = 64
SWA_WINDOW = 128
D_FF = 2816
CONV_K = 3
EPS = 1e-6

GLA_QK = GLA_HEADS * GLA_DK
GLA_VW = GLA_HEADS * GLA_DV
SWA_QW = SWA_HEADS * SWA_HD
SWA_KW = SWA_KV_HEADS * SWA_HD
SWA_GROUP = SWA_HEADS // SWA_KV_HEADS

LANES = 128
SUBLANES = 8
VMEM_LIMIT = 56 * 1024 * 1024

_R_LR = 2 * GLA_QK + 2 * GLA_VW
_R_SQ = _R_LR + GLA_LOWRANK
P_IN = _R_SQ + SWA_QW + 2 * SWA_KW
_C_GQ = 0
_C_GK = _C_GQ + GLA_QK
_C_GV = _C_GK + GLA_QK
_C_GR = _C_GV + GLA_VW
_C_SQ = _C_GR + GLA_VW
_C_SK = _C_SQ + SWA_QW
_C_SV = _C_SK + SWA_KW
_C_LR = _C_SV + SWA_KW
P_PACK = _C_LR + LANES

IN_TM = 512
GLA_TRI = 128
GLA_TL = 512
SWA_TQ = 512
FFN_TM = 512
FFN_FC = 256
FFN_NC = D_FF // FFN_FC
FFN_DOWN_GROUP = 4
FFN_RB = 64


def _alibi_slopes(n_heads):
    return np.array([2.0 ** (-8.0 * (h + 1) / n_heads) for h in range(n_heads)], dtype=np.float32)


def _lane_lo(shape):
    return lax.broadcasted_iota(jnp.int32, shape, len(shape) - 1) % LANES < (LANES // 2)


def _half_rms_inv(v):
    lo = _lane_lo(v.shape)
    sq = v * v
    ss_lo = jnp.sum(jnp.where(lo, sq, 0.0), axis=-1, keepdims=True)
    ss_hi = jnp.sum(jnp.where(lo, 0.0, sq), axis=-1, keepdims=True)
    inv_lo = lax.rsqrt(ss_lo * (1.0 / SWA_HD) + EPS)
    inv_hi = lax.rsqrt(ss_hi * (1.0 / SWA_HD) + EPS)
    return jnp.where(lo, inv_lo, inv_hi)


def _dup_halves(v):
    lo = _lane_lo(v.shape)
    swapped = pltpu.roll(v, LANES // 2, axis=1)
    return jnp.concatenate([jnp.where(lo, v, swapped), jnp.where(lo, swapped, v)], axis=1)


def _inproj_kernel(x_ref, g_ref, w_ref, wa_ref, ba_ref, qn_ref, kn_ref, tri_ref,
                   qlo_ref, qhi_ref, ke_ref, kdlo_ref, kdhi_ref, dec_ref, gv_ref, sr_ref,
                   sq_ref, sk_ref, sv_ref):
    tm = x_ref.shape[0]
    C = GLA_CHUNK
    x = x_ref[...]
    ms = jnp.mean(x * x, axis=-1, keepdims=True)
    h = (x * lax.rsqrt(ms + EPS) * g_ref[...]).astype(BF16)

    def proj(c0, width):
        return jnp.dot(h, w_ref[:, c0:c0 + width], preferred_element_type=F32)

    glr = proj(_C_LR, LANES).astype(BF16)
    gv_ref[...] = proj(_C_GV, GLA_VW).astype(BF16)
    pre = jnp.dot(glr, wa_ref[...], preferred_element_type=F32) + ba_ref[...]
    la = (jnp.minimum(pre, 0.0) - jnp.log(1.0 + jnp.exp(-jnp.abs(pre)))) * (1.0 / GLA_TAU)
    la_hi = la.astype(BF16)
    la_lo = (la - la_hi.astype(F32)).astype(BF16)
    tri = tri_ref[...]
    tb = tri.shape[0]
    r = proj(_C_GR, GLA_VW)
    b_blks = [jnp.dot(tri, la_hi[r0:r0 + tb], preferred_element_type=F32)
              + jnp.dot(tri, la_lo[r0:r0 + tb], preferred_element_type=F32)
              for r0 in range(0, tm, tb)]
    gq = proj(_C_GQ, GLA_QK) * (GLA_DK ** -0.5)
    gk = proj(_C_GK, GLA_QK)
    sr_ref[...] = r / (1.0 + jnp.exp(-r))
    q_all = proj(_C_SQ, SWA_QW)
    kv = proj(_C_SK, 2 * SWA_KW)

    lo = _lane_lo((C, GLA_QK))
    for i, r0 in enumerate(range(0, tm, tb)):
        for c0 in range(0, tb, C):
            rows = slice(r0 + c0, r0 + c0 + C)
            b = b_blks[i][c0:c0 + C]
            b_last = b[C - 1:C]
            qe = gq[rows] * jnp.exp(b)
            qlo_ref[rows, :] = jnp.where(lo, qe, 0.0).astype(BF16)
            qhi_ref[rows, :] = jnp.where(lo, 0.0, qe).astype(BF16)
            ke_ref[rows, :] = (gk[rows] * jnp.exp(-b)).astype(BF16)
            kd = gk[rows] * jnp.exp(b_last - b)
            kdlo_ref[rows, :] = jnp.where(lo, kd, 0.0).astype(BF16)
            kdhi_ref[rows, :] = jnp.where(lo, 0.0, kd).astype(BF16)
            dec_ref[(r0 + c0) // C:(r0 + c0) // C + 1, :] = jnp.exp(b_last)

    qn = qn_ref[...]
    for p in range(SWA_QW // LANES):
        q = q_all[:, p * LANES:(p + 1) * LANES]
        sq_ref[:, p * LANES:(p + 1) * LANES] = (
            q * _half_rms_inv(q) * qn * (SWA_HD ** -0.5)).astype(BF16)
    k = kv[:, :SWA_KW]
    sk_ref[...] = _dup_halves(k * _half_rms_inv(k) * kn_ref[...]).astype(BF16)
    sv_ref[...] = _dup_halves(kv[:, SWA_KW:]).astype(BF16)


def _in_proj(x2, g, w_pack, wa_pad, ba, qn2, kn2, tri, layer):
    n = x2.shape[0]
    tm = min(IN_TM, n)
    row = lambda w: pl.BlockSpec((tm, w), lambda i: (i, 0))
    lay = lambda a: pl.BlockSpec((None,) + a.shape[1:], lambda i: (layer,) + (0,) * (a.ndim - 1))
    outs = [
        (GLA_QK, BF16), (GLA_QK, BF16), (GLA_QK, BF16), (GLA_QK, BF16), (GLA_QK, BF16), None,
        (GLA_VW, BF16), (GLA_VW, F32), (SWA_QW, BF16), (2 * SWA_KW, BF16), (2 * SWA_KW, BF16),
    ]
    dec_spec = pl.BlockSpec((tm // GLA_CHUNK, GLA_QK), lambda i: (i, 0))
    dec_shape = jax.ShapeDtypeStruct((n // GLA_CHUNK, GLA_QK), F32)
    return pl.pallas_call(
        _inproj_kernel,
        grid=(n // tm,),
        in_specs=[row(D_MODEL), lay(g), lay(w_pack), lay(wa_pad), lay(ba), lay(qn2), lay(kn2),
                  pl.BlockSpec(tri.shape, lambda i: (0, 0))],
        out_specs=[dec_spec if o is None else row(o[0]) for o in outs],
        out_shape=[dec_shape if o is None else jax.ShapeDtypeStruct((n, o[0]), o[1]) for o in outs],
        compiler_params=pltpu.CompilerParams(
            dimension_semantics=("parallel",), vmem_limit_bytes=VMEM_LIMIT),
        name="in_proj",
    )(x2, g, w_pack, wa_pad, ba, qn2, kn2, tri)


def _gla_kernel(qlo_ref, qhi_ref, ke_ref, kdlo_ref, kdhi_ref, dec_ref, v_ref, sr_ref, gn_ref, o_ref,
                st_ref, sall_ref):
    tl = v_ref.shape[0]
    C = GLA_CHUNK
    nc = tl // C
    npair = GLA_HEADS // 2
    units = [(c, p) for c in range(nc) for p in range(npair)]
    rows_of = lambda c: slice(c * C, (c + 1) * C)
    lanes_of = lambda p: slice(p * LANES, (p + 1) * LANES)
    vcol_of = lambda hd: slice(hd * GLA_DV, (hd + 1) * GLA_DV)
    nt = (((1,), (1,)), ((), ()))
    tn = (((0,), (0,)), ((), ()))

    @pl.when(pl.program_id(1) == 0)
    def _():
        st_ref[...] = jnp.zeros_like(st_ref)

    ri = lax.broadcasted_iota(jnp.int32, (2 * C, 2 * C), 0)
    ci = lax.broadcasted_iota(jnp.int32, (2 * C, 2 * C), 1)
    blockdiag_causal = (ri // C == ci // C) & (ci % C <= ri % C)
    gn = gn_ref[...]

    def v_pair(c, p):
        return jnp.concatenate([v_ref[rows_of(c), vcol_of(2 * p)],
                                v_ref[rows_of(c), vcol_of(2 * p + 1)]], axis=0)

    upd = {}
    for c, p in units:
        kd2 = jnp.concatenate([kdlo_ref[rows_of(c), lanes_of(p)], kdhi_ref[rows_of(c), lanes_of(p)]],
                              axis=0)
        upd[c, p] = lax.dot_general(v_pair(c, p), kd2, tn, preferred_element_type=F32)
    for p in range(npair):
        st = st_ref[p]
        for c in range(nc):
            sall_ref[c, p] = st.astype(BF16)
            st = st * dec_ref[c:c + 1, lanes_of(p)] + upd.pop((c, p))
        st_ref[p] = st

    sc, out = {}, {}

    def score_stage(c):
        for p in range(npair):
            q2 = jnp.concatenate(
                [qlo_ref[rows_of(c), lanes_of(p)], qhi_ref[rows_of(c), lanes_of(p)]], axis=0)
            ke = ke_ref[rows_of(c), lanes_of(p)]
            rhs = jnp.concatenate([ke, ke, sall_ref[c, p]], axis=0)
            sc[c, p] = lax.dot_general(q2, rhs, nt, preferred_element_type=F32)

    def value_stage(c):
        for p in range(npair):
            s2 = sc.pop((c, p))
            am = jnp.where(blockdiag_causal, s2[:, :2 * C], 0.0).astype(BF16)
            out[c, p] = jnp.dot(am, v_pair(c, p), preferred_element_type=F32) + s2[:, 2 * C:]

    def norm_stage(c):
        for p in range(npair):
            o = out.pop((c, p))
            o = o * lax.rsqrt(jnp.mean(o * o, axis=-1, keepdims=True) + EPS) * gn
            for half in range(2):
                vcol = vcol_of(2 * p + half)
                o_ref[rows_of(c), vcol] = (
                    o[half * C:(half + 1) * C] * sr_ref[rows_of(c), vcol]).astype(o_ref.dtype)

    for c in range(nc + 2):
        if c < nc:
            score_stage(c)
        if 0 <= c - 1 < nc:
            value_stage(c - 1)
        if 0 <= c - 2 < nc:
            norm_stage(c - 2)


def _gla(qlo, qhi, ke, kdlo, kdhi, dec, gv, sr, gn, layer, batch, seq):
    tl = min(GLA_TL, seq)
    nt = seq // tl
    row = lambda w: pl.BlockSpec((tl, w), lambda b, t: (b * nt + t, 0))
    return pl.pallas_call(
        _gla_kernel,
        grid=(batch, nt),
        in_specs=[row(GLA_QK), row(GLA_QK), row(GLA_QK), row(GLA_QK), row(GLA_QK),
                  pl.BlockSpec((tl // GLA_CHUNK, GLA_QK), lambda b, t: (b * nt + t, 0)),
                  row(GLA_VW), row(GLA_VW),
                  pl.BlockSpec((None, 1, GLA_DV), lambda b, t: (layer, 0, 0))],
        out_specs=row(GLA_VW),
        out_shape=jax.ShapeDtypeStruct((batch * seq, GLA_VW), BF16),
        scratch_shapes=[
            pltpu.VMEM((GLA_HEADS // 2, GLA_DV, LANES), F32),
            pltpu.VMEM((tl // GLA_CHUNK, GLA_HEADS // 2, GLA_DV, LANES), BF16),
        ],
        compiler_params=pltpu.CompilerParams(
            dimension_semantics=("parallel", "arbitrary"), vmem_limit_bytes=VMEM_LIMIT),
        name="gla",
    )(qlo, qhi, ke, kdlo, kdhi, dec, gv, sr, gn)


def _swa_kernel(sink_ref, q_ref, kc_ref, kp_ref, vc_ref, vp_ref, bias_ref, o_ref, *, layer):
    W = SWA_WINDOW
    tq = q_ref.shape[0]
    first = pl.program_id(1) == 0
    lo = _lane_lo((W, LANES))
    causal = (lax.broadcasted_iota(jnp.int32, (W, W), 1)
              <= lax.broadcasted_iota(jnp.int32, (W, W), 0))
    nt = (((1,), (1,)), ((), ()))
    pairs_per_kv = SWA_GROUP // 2
    units = [(j, kv) for j in range(tq // W) for kv in range(SWA_KV_HEADS)]
    rows_of = lambda j: slice(j * W, (j + 1) * W)
    lanes_of = lambda p: slice(p * LANES, (p + 1) * LANES)

    def window(cur_ref, prev_ref, j, kv):
        prev = prev_ref[:, lanes_of(kv)] if j == 0 else cur_ref[rows_of(j - 1), lanes_of(kv)]
        return jnp.concatenate([prev, cur_ref[rows_of(j), lanes_of(kv)]], axis=0)

    scores, outs = {}, {}

    def score_stage(j, kv):
        parts = []
        for p in range(kv * pairs_per_kv, (kv + 1) * pairs_per_kv):
            qp = q_ref[rows_of(j), lanes_of(p)]
            parts.append(jnp.where(lo, qp, jnp.zeros_like(qp)))
            parts.append(jnp.where(lo, jnp.zeros_like(qp), qp))
        qs = jnp.concatenate(parts, axis=0)
        scores[j, kv] = lax.dot_general(qs, window(kc_ref, kp_ref, j, kv), nt,
                                        preferred_element_type=F32)

    def softmax_stage(j, kv):
        s_all = scores.pop((j, kv))
        pes, denoms = [], []
        for r in range(SWA_GROUP):
            head = kv * SWA_GROUP + r
            table = jnp.where(first, SWA_HEADS + head, head) if j == 0 else head
            s2 = s_all[r * W:(r + 1) * W]
            s = jnp.where(causal, s2[:, W:], s2[:, :W]) + bias_ref[table]
            sink = sink_ref[layer, head]
            m = jnp.maximum(jnp.max(s, axis=-1, keepdims=True), sink)
            e = jnp.exp(s - m)
            denoms.append(jnp.sum(e, axis=-1, keepdims=True) + jnp.exp(sink - m))
            eb = e.astype(BF16)
            zero = jnp.zeros_like(eb)
            pes.append(jnp.concatenate(
                [jnp.where(causal, zero, eb), jnp.where(causal, eb, zero)], axis=1))
        o_all = jnp.dot(jnp.concatenate(pes, axis=0), window(vc_ref, vp_ref, j, kv),
                        preferred_element_type=F32)
        for r in range(SWA_GROUP):
            outs[j, kv * SWA_GROUP + r] = o_all[r * W:(r + 1) * W] / denoms[r]

    def store_stage(j):
        for p in range(SWA_QW // LANES):
            o_ref[rows_of(j), lanes_of(p)] = jnp.where(
                lo, outs.pop((j, 2 * p)), outs.pop((j, 2 * p + 1))).astype(o_ref.dtype)

    score_stage(*units[0])
    for i, (j, kv) in enumerate(units):
        if i + 1 < len(units):
            score_stage(*units[i + 1])
        softmax_stage(j, kv)
        if kv == SWA_KV_HEADS - 1:
            store_stage(j)


def _swa(sinks, sq, sk, sv, bias, layer, batch, seq):
    W = SWA_WINDOW
    tq = min(SWA_TQ, seq)
    nq = seq // tq
    bpt = tq // W
    bps = seq // W
    cur = lambda w: pl.BlockSpec((tq, w), lambda b, i: (b * nq + i, 0))
    prev = pl.BlockSpec((W, 2 * SWA_KW), lambda b, i: (b * bps + jnp.maximum(i * bpt - 1, 0), 0))
    return pl.pallas_call(
        functools.partial(_swa_kernel, layer=layer),
        grid=(batch, nq),
        in_specs=[
            pl.BlockSpec(memory_space=pltpu.SMEM),
            cur(SWA_QW), cur(2 * SWA_KW), prev, cur(2 * SWA_KW), prev,
            pl.BlockSpec(bias.shape, lambda b, i: (0, 0, 0)),
        ],
        out_specs=cur(SWA_QW),
        out_shape=jax.ShapeDtypeStruct((batch * seq, SWA_QW), BF16),
        compiler_params=pltpu.CompilerParams(
            dimension_semantics=("parallel", "parallel"), vmem_limit_bytes=VMEM_LIMIT),
        name="swa",
    )(sinks, sq, sk, sk, sv, sv, bias)


def _swa_bias_tables():
    W = SWA_WINDOW
    i = np.arange(W)[:, None]
    j = np.arange(W)[None, :]
    dist = np.where(j <= i, i - j, W + i - j).astype(np.float32)
    slopes = _alibi_slopes(SWA_HEADS)
    base = (-slopes[:, None, None] * dist[None]).astype(np.float32)
    first = np.where((j <= i)[None], base, -np.inf).astype(np.float32)
    return np.concatenate([base, first], axis=0)


def _ffn_kernel(x_ref, og_ref, os_ref, wo_ref, g_ref, wu_ref, cw_ref, wd_ref, y_ref,
                x1_ref, h_ref, t3_ref, ubuf_ref, gbuf_ref, halo_ref):
    tm = x_ref.shape[0]
    S = SUBLANES
    nv = tm // S
    nb = nv // S
    ng = D_MODEL // LANES
    first = pl.program_id(1) == 0

    x1 = (x_ref[...]
          + jnp.dot(og_ref[...], wo_ref[:GLA_VW, :], preferred_element_type=F32)
          + jnp.dot(os_ref[...], wo_ref[GLA_VW:, :], preferred_element_type=F32))
    x1_ref[...] = x1
    ms = jnp.mean(x1 * x1, axis=-1, keepdims=True)
    hn = x1 * lax.rsqrt(ms + EPS) * g_ref[...]
    for a in range(S):
        for b in range(nb):
            src = slice(S * (nb * a + b), S * (nb * a + b) + S)
            dst = slice(S * (S * b + a), S * (S * b + a) + S)
            for g in range(ng):
                t3_ref[g, dst, :] = hn[src, g * LANES:(g + 1) * LANES]
    for k in range(nv // 2):
        rows = []
        for v in (2 * k, 2 * k + 1):
            b, c = v // S, v % S
            rows.append(jnp.concatenate(
                [t3_ref[g, pl.ds(S * S * b + c, S, stride=S), :] for g in range(ng)], axis=1))
        h_ref[2 * S * k:2 * S * (k + 1), :] = jnp.concatenate(rows, axis=0).astype(BF16)

    sub = lax.broadcasted_iota(jnp.int32, (S, 2 * FFN_FC), 0)

    def up_stage(c):
        slot = c % 2
        h = h_ref[...]
        u = jnp.concatenate([
            jnp.dot(h, wu_ref[:, c * FFN_FC:(c + 1) * FFN_FC], preferred_element_type=F32),
            jnp.dot(h, wu_ref[:, D_FF + c * FFN_FC:D_FF + (c + 1) * FFN_FC],
                    preferred_element_type=F32)], axis=1)
        halo = jnp.where(first, 0.0, halo_ref[c])
        halo_ref[c] = u[tm - 2 * S:, :]
        fix2 = jnp.where(sub == 0, pltpu.roll(halo[:S], 1, axis=0),
                         pltpu.roll(u[tm - 2 * S:tm - S], 1, axis=0))
        fix1 = jnp.where(sub == 0, pltpu.roll(halo[S:], 1, axis=0),
                         pltpu.roll(u[tm - S:], 1, axis=0))
        ubuf_ref[slot, 0:S, :] = fix2
        ubuf_ref[slot, S:2 * S, :] = fix1
        ubuf_ref[slot, 2 * S:2 * S + tm, :] = u

    def gate_stage(c):
        slot = c % 2
        ca = slice(c * FFN_FC, (c + 1) * FFN_FC)
        cb = slice(D_FF + c * FFN_FC, D_FF + (c + 1) * FFN_FC)
        cw = jnp.concatenate([cw_ref[:, ca], cw_ref[:, cb]], axis=1)
        for r in range(0, tm, FFN_RB):
            u0 = ubuf_ref[slot, 2 * S + r:2 * S + r + FFN_RB, :]
            u1 = ubuf_ref[slot, S + r:S + r + FFN_RB, :]
            u2 = ubuf_ref[slot, r:r + FFN_RB, :]
            y = u2 * cw[0:1] + u1 * cw[1:2] + u0 * cw[2:3] + cw[3:4]
            a = y[:, :FFN_FC]
            gate = (a / (1.0 + jnp.exp(-a))) * y[:, FFN_FC:]
            gbuf_ref[r:r + FFN_RB, ca] = gate.astype(BF16)

    def down_stage(c0, c1):
        k0, k1 = c0 * FFN_FC, c1 * FFN_FC
        d = jnp.dot(gbuf_ref[:, k0:k1], wd_ref[k0:k1, :], preferred_element_type=F32)
        for g in range(ng):
            if c0 == 0:
                t3_ref[g] = d[:, g * LANES:(g + 1) * LANES]
            else:
                t3_ref[g] += d[:, g * LANES:(g + 1) * LANES]

    up_stage(0)
    group_start = 0
    for c in range(FFN_NC):
        if c + 1 < FFN_NC:
            up_stage(c + 1)
        gate_stage(c)
        if c + 1 - group_start == FFN_DOWN_GROUP or c + 1 == FFN_NC:
            down_stage(group_start, c + 1)
            group_start = c + 1

    for a in range(S):
        for b in range(nb):
            rows = slice(S * (nb * a + b), S * (nb * a + b) + S)
            ffn = jnp.concatenate(
                [t3_ref[g, pl.ds(S * S * b + a, S, stride=S), :] for g in range(ng)], axis=1)
            y_ref[rows, :] = x1_ref[rows, :] + ffn


def _out_ffn(x2, o_gla, o_swa, wo, g, wu, cw, wd, layer, batch, seq):
    tm = min(FFN_TM, seq)
    assert tm % (SUBLANES * SUBLANES) == 0
    nt = seq // tm
    row = lambda w: pl.BlockSpec((tm, w), lambda b, t: (b * nt + t, 0))
    lay = lambda a: pl.BlockSpec((None,) + a.shape[1:], lambda b, t: (layer,) + (0,) * (a.ndim - 1),
                                 pipeline_mode=pl.Buffered(1))
    return pl.pallas_call(
        _ffn_kernel,
        grid=(batch, nt),
        in_specs=[row(D_MODEL), row(GLA_VW), row(SWA_QW), lay(wo), lay(g), lay(wu), lay(cw), lay(wd)],
        out_specs=row(D_MODEL),
        out_shape=jax.ShapeDtypeStruct((batch * seq, D_MODEL), F32),
        scratch_shapes=[
            pltpu.VMEM((tm, D_MODEL), F32),
            pltpu.VMEM((tm, D_MODEL), BF16),
            pltpu.VMEM((D_MODEL // LANES, tm, LANES), F32),
            pltpu.VMEM((2, tm + 2 * SUBLANES, 2 * FFN_FC), F32),
            pltpu.VMEM((tm, D_FF), BF16),
            pltpu.VMEM((FFN_NC, 2 * SUBLANES, 2 * FFN_FC), F32),
        ],
        compiler_params=pltpu.CompilerParams(
            dimension_semantics=("parallel", "arbitrary"), vmem_limit_bytes=VMEM_LIMIT),
        name="out_ffn",
    )(x2, o_gla, o_swa, wo, g, wu, cw, wd)


def kernel(x, mix_norm, w_in, w_alpha2, b_alpha, gla_norm, q_norm, k_norm, sinks, w_out, ffn_norm,
           w_up, conv_w, conv_b, w_down):
    batch, seq, d = x.shape
    depth = w_in.shape[0]
    assert d == D_MODEL and w_in.shape[2] == P_IN
    assert seq % max(GLA_TL, SWA_TQ, FFN_TM) == 0 and (batch * seq) % IN_TM == 0

    w_pack = jnp.concatenate([
        w_in[:, :, :_R_LR], w_in[:, :, _R_SQ:], w_in[:, :, _R_LR:_R_SQ],
        jnp.zeros((depth, D_MODEL, LANES - GLA_LOWRANK), w_in.dtype)], axis=2).astype(BF16)
    wa_pad = jnp.concatenate(
        [w_alpha2, jnp.zeros((depth, LANES - GLA_LOWRANK, GLA_QK), w_alpha2.dtype)],
        axis=1).astype(BF16)
    wo = w_out.astype(BF16)
    wu = w_up.astype(BF16)
    wd = w_down.astype(BF16)
    cw = jnp.concatenate([
        conv_w, conv_b[:, None, :],
        jnp.zeros((depth, SUBLANES - CONV_K - 1, 2 * D_FF), conv_w.dtype)], axis=1)
    row3 = lambda a: a.reshape(depth, 1, a.shape[-1])
    qn2 = row3(jnp.tile(q_norm, (1, 2)))
    kn2 = row3(jnp.tile(k_norm, (1, 2)))

    pos = np.arange(GLA_TRI)
    tri = jnp.asarray(
        (pos[:, None] >= pos[None, :]) & (pos[:, None] // GLA_CHUNK == pos[None, :] // GLA_CHUNK),
        dtype=BF16)
    bias = jnp.asarray(_swa_bias_tables())

    x2 = x.reshape(batch * seq, D_MODEL)
    for l in range(depth):
        qlo, qhi, ke, kdlo, kdhi, dec, gv, sr, sq, sk, sv = _in_proj(
            x2, row3(mix_norm), w_pack, wa_pad, row3(b_alpha), qn2, kn2, tri, l)
        o_gla = _gla(qlo, qhi, ke, kdlo, kdhi, dec, gv, sr, row3(gla_norm), l, batch, seq)
        o_swa = _swa(sinks, sq, sk, sv, bias, l, batch, seq)
        x2 = _out_ffn(x2, o_gla, o_swa, wo, row3(ffn_norm), wu, cw, wd, l, batch, seq)
    return x2.reshape(batch, seq, D_MODEL)
```

```python
import functools

import numpy as np
import jax
import jax.numpy as jnp
from jax import lax
from jax.experimental import pallas as pl
from jax.experimental.pallas import tpu as pltpu

F32 = jnp.float32
BF16 = jnp.bfloat16

D_MODEL = 1024
GLA_HEADS = 4
GLA_DV = 128
GLA_DK = 64
GLA_LOWRANK = 16
GLA_TAU = 16.0
GLA_CHUNK = 64
SWA_HEADS = 8
SWA_KV_HEADS = 2
SWA_HD = 64
SWA_WINDOW = 128
D_FF = 2816
CONV_K = 3
EPS = 1e-6

GLA_QK = GLA_HEADS * GLA_DK
GLA_VW = GLA_HEADS * GLA_DV
SWA_QW = SWA_HEADS * SWA_HD
SWA_KW = SWA_KV_HEADS * SWA_HD
SWA_GROUP = SWA_HEADS // SWA_KV_HEADS

LANES = 128
SUBLANES = 8
VMEM_LIMIT = 56 * 1024 * 1024

_R_LR = 2 * GLA_QK + 2 * GLA_VW
_R_SQ = _R_LR + GLA_LOWRANK
P_IN = _R_SQ + SWA_QW + 2 * SWA_KW
_C_GQ = 0
_C_GK = _C_GQ + GLA_QK
_C_GV = _C_GK + GLA_QK
_C_GR = _C_GV + GLA_VW
_C_SQ = _C_GR + GLA_VW
_C_SK = _C_SQ + SWA_QW
_C_SV = _C_SK + SWA_KW
_C_LR = _C_SV + SWA_KW
P_PACK = _C_LR + LANES

IN_TM = 2048
IN_SUB = 512
GLA_TRI = 128
GLA_TL = 512
SWA_TQ = 1024
FFN_TM = 512
FFN_FC = 256
FFN_NC = D_FF // FFN_FC
FFN_DOWN_GROUP = 4
FFN_RB = 64


def _alibi_slopes(n_heads):
    return np.array([2.0 ** (-8.0 * (h + 1) / n_heads) for h in range(n_heads)], dtype=np.float32)


def _lane_lo(shape):
    return lax.broadcasted_iota(jnp.int32, shape, len(shape) - 1) % LANES < (LANES // 2)


def _half_rms_inv(v):
    lo = _lane_lo(v.shape)
    sq = v * v
    ss_lo = jnp.sum(jnp.where(lo, sq, 0.0), axis=-1, keepdims=True)
    ss_hi = jnp.sum(jnp.where(lo, 0.0, sq), axis=-1, keepdims=True)
    inv_lo = lax.rsqrt(ss_lo * (1.0 / SWA_HD) + EPS)
    inv_hi = lax.rsqrt(ss_hi * (1.0 / SWA_HD) + EPS)
    return jnp.where(lo, inv_lo, inv_hi)


def _dup_halves(v):
    lo = _lane_lo(v.shape)
    swapped = pltpu.roll(v, LANES // 2, axis=1)
    return jnp.concatenate([jnp.where(lo, v, swapped), jnp.where(lo, swapped, v)], axis=1)


def _inproj_kernel(x_ref, g_ref, w_ref, wa_ref, ba_ref, qn_ref, kn_ref, tri_ref,
                   qe_ref, ke_ref, kd_ref, dec_ref, gv_ref, sr_ref, sq_ref, sk_ref, sv_ref):
    tm = x_ref.shape[0]
    C = GLA_CHUNK
    tri = tri_ref[...]
    tb = tri.shape[0]
    qn = qn_ref[...]

    def norm_stage(r0):
        x = x_ref[r0:r0 + IN_SUB, :]
        ms = jnp.mean(x * x, axis=-1, keepdims=True)
        return (x * lax.rsqrt(ms + EPS) * g_ref[...]).astype(BF16)

    def matmul_stage(r0, h):
        rows = slice(r0, r0 + IN_SUB)
        proj = lambda c0, width: jnp.dot(h, w_ref[:, c0:c0 + width], preferred_element_type=F32)
        glr = proj(_C_LR, LANES).astype(BF16)
        gv_ref[rows, :] = proj(_C_GV, GLA_VW).astype(BF16)
        pre = jnp.dot(glr, wa_ref[...], preferred_element_type=F32) + ba_ref[...]
        la = (jnp.minimum(pre, 0.0) - jnp.log(1.0 + jnp.exp(-jnp.abs(pre)))) * (1.0 / GLA_TAU)
        la_hi = la.astype(BF16)
        la_lo = (la - la_hi.astype(F32)).astype(BF16)
        r = proj(_C_GR, GLA_VW)
        b_blks = [jnp.dot(tri, la_hi[t0:t0 + tb], preferred_element_type=F32)
                  + jnp.dot(tri, la_lo[t0:t0 + tb], preferred_element_type=F32)
                  for t0 in range(0, IN_SUB, tb)]
        gq = proj(_C_GQ, GLA_QK) * (GLA_DK ** -0.5)
        gk = proj(_C_GK, GLA_QK)
        q_all = proj(_C_SQ, SWA_QW)
        kv = proj(_C_SK, 2 * SWA_KW)
        return dict(r=r, b_blks=b_blks, gq=gq, gk=gk, q_all=q_all, kv=kv)

    def tail_stage(r0, v):
        rows = slice(r0, r0 + IN_SUB)
        r = v["r"]
        sr_ref[rows, :] = (r / (1.0 + jnp.exp(-r))).astype(sr_ref.dtype)
        for i, t0 in enumerate(range(0, IN_SUB, tb)):
            for c0 in range(0, tb, C):
                crow = slice(r0 + t0 + c0, r0 + t0 + c0 + C)
                lrow = slice(t0 + c0, t0 + c0 + C)
                b = v["b_blks"][i][c0:c0 + C]
                b_last = b[C - 1:C]
                qe_ref[crow, :] = (v["gq"][lrow] * jnp.exp(b)).astype(BF16)
                ke_ref[crow, :] = (v["gk"][lrow] * jnp.exp(-b)).astype(BF16)
                kd_ref[crow, :] = (v["gk"][lrow] * jnp.exp(b_last - b)).astype(BF16)
                ci = (r0 + t0 + c0) // C
                dec_ref[ci:ci + 1, :] = jnp.exp(b_last)
        for p in range(SWA_QW // LANES):
            q = v["q_all"][:, p * LANES:(p + 1) * LANES]
            sq_ref[rows, p * LANES:(p + 1) * LANES] = (
                q * _half_rms_inv(q) * qn * (SWA_HD ** -0.5)).astype(BF16)
        k = v["kv"][:, :SWA_KW]
        sk_ref[rows, :] = _dup_halves(k * _half_rms_inv(k) * kn_ref[...]).astype(BF16)
        sv_ref[rows, :] = _dup_halves(v["kv"][:, SWA_KW:]).astype(BF16)

    starts = list(range(0, tm, IN_SUB))
    hs = [norm_stage(r0) for r0 in starts]
    vals = matmul_stage(starts[0], hs[0])
    for i, r0 in enumerate(starts):
        nxt = matmul_stage(starts[i + 1], hs[i + 1]) if i + 1 < len(starts) else None
        tail_stage(r0, vals)
        vals = nxt


def _in_proj(x2, g, w_pack, wa_pad, ba, qn2, kn2, tri, layer):
    n = x2.shape[0]
    tm = min(IN_TM, n)
    row = lambda w: pl.BlockSpec((tm, w), lambda i: (i, 0))
    lay = lambda a: pl.BlockSpec((None,) + a.shape[1:], lambda i: (layer,) + (0,) * (a.ndim - 1),
                                 pipeline_mode=pl.Buffered(1))
    outs = [
        (GLA_QK, BF16), (GLA_QK, BF16), (GLA_QK, BF16), None,
        (GLA_VW, BF16), (GLA_VW, BF16), (SWA_QW, BF16), (2 * SWA_KW, BF16), (2 * SWA_KW, BF16),
    ]
    dec_spec = pl.BlockSpec((tm // GLA_CHUNK, GLA_QK), lambda i: (i, 0))
    dec_shape = jax.ShapeDtypeStruct((n // GLA_CHUNK, GLA_QK), F32)
    return pl.pallas_call(
        _inproj_kernel,
        grid=(n // tm,),
        in_specs=[row(D_MODEL), lay(g), lay(w_pack), lay(wa_pad), lay(ba), lay(qn2), lay(kn2),
                  pl.BlockSpec(tri.shape, lambda i: (0, 0))],
        out_specs=[dec_spec if o is None else row(o[0]) for o in outs],
        out_shape=[dec_shape if o is None else jax.ShapeDtypeStruct((n, o[0]), o[1]) for o in outs],
        compiler_params=pltpu.CompilerParams(
            dimension_semantics=("parallel",), vmem_limit_bytes=VMEM_LIMIT),
        name="in_proj",
    )(x2, g, w_pack, wa_pad, ba, qn2, kn2, tri)


def _gla_kernel(qe_ref, ke_ref, kd_ref, dec_ref, v_ref, sr_ref, gn_ref, o_ref, st_ref, sall_ref):
    tl = v_ref.shape[0]
    C = GLA_CHUNK
    nc = tl // C
    npair = GLA_HEADS // 2
    units = [(c, p) for c in range(nc) for p in range(npair)]
    rows_of = lambda c: slice(c * C, (c + 1) * C)
    lanes_of = lambda p: slice(p * LANES, (p + 1) * LANES)
    vcol_of = lambda hd: slice(hd * GLA_DV, (hd + 1) * GLA_DV)
    nt = (((1,), (1,)), ((), ()))
    tn = (((0,), (0,)), ((), ()))

    @pl.when(pl.program_id(1) == 0)
    def _():
        st_ref[...] = jnp.zeros_like(st_ref)

    ri = lax.broadcasted_iota(jnp.int32, (2 * C, 2 * C), 0)
    ci = lax.broadcasted_iota(jnp.int32, (2 * C, 2 * C), 1)
    blockdiag_causal = (ri // C == ci // C) & (ci % C <= ri % C)
    gn = gn_ref[...]
    lo = _lane_lo((C, LANES))

    def split_pair(ref, c, p):
        x = ref[rows_of(c), lanes_of(p)]
        zero = jnp.zeros_like(x)
        return jnp.concatenate([jnp.where(lo, x, zero), jnp.where(lo, zero, x)], axis=0)

    def v_pair(c, p):
        return jnp.concatenate([v_ref[rows_of(c), vcol_of(2 * p)],
                                v_ref[rows_of(c), vcol_of(2 * p + 1)]], axis=0)

    upd = {}
    for c, p in units:
        upd[c, p] = lax.dot_general(v_pair(c, p), split_pair(kd_ref, c, p), tn,
                                    preferred_element_type=F32)
    for p in range(npair):
        st = st_ref[p]
        for c in range(nc):
            sall_ref[c, p] = st.astype(BF16)
            st = st * dec_ref[c:c + 1, lanes_of(p)] + upd.pop((c, p))
        st_ref[p] = st

    sc, out = {}, {}

    def score_stage(c):
        for p in range(npair):
            q2 = split_pair(qe_ref, c, p)
            ke = ke_ref[rows_of(c), lanes_of(p)]
            rhs = jnp.concatenate([ke, ke, sall_ref[c, p]], axis=0)
            sc[c, p] = lax.dot_general(q2, rhs, nt, preferred_element_type=F32)

    def value_stage(c):
        for p in range(npair):
            s2 = sc.pop((c, p))
            am = jnp.where(blockdiag_causal, s2[:, :2 * C], 0.0).astype(BF16)
            out[c, p] = jnp.dot(am, v_pair(c, p), preferred_element_type=F32) + s2[:, 2 * C:]

    def norm_stage(c):
        for p in range(npair):
            o = out.pop((c, p))
            o = o * lax.rsqrt(jnp.mean(o * o, axis=-1, keepdims=True) + EPS) * gn
            for half in range(2):
                vcol = vcol_of(2 * p + half)
                o_ref[rows_of(c), vcol] = (
                    o[half * C:(half + 1) * C] * sr_ref[rows_of(c), vcol]).astype(o_ref.dtype)

    for c in range(nc + 2):
        if c < nc:
            score_stage(c)
        if 0 <= c - 1 < nc:
            value_stage(c - 1)
        if 0 <= c - 2 < nc:
            norm_stage(c - 2)


def _gla(qe, ke, kd, dec, gv, sr, gn, layer, batch, seq):
    tl = min(GLA_TL, seq)
    nt = seq // tl
    row = lambda w: pl.BlockSpec((tl, w), lambda b, t: (b * nt + t, 0))
    return pl.pallas_call(
        _gla_kernel,
        grid=(batch, nt),
        in_specs=[row(GLA_QK), row(GLA_QK), row(GLA_QK),
                  pl.BlockSpec((tl // GLA_CHUNK, GLA_QK), lambda b, t: (b * nt + t, 0)),
                  row(GLA_VW), row(GLA_VW),
                  pl.BlockSpec((None, 1, GLA_DV), lambda b, t: (layer, 0, 0))],
        out_specs=row(GLA_VW),
        out_shape=jax.ShapeDtypeStruct((batch * seq, GLA_VW), BF16),
        scratch_shapes=[
            pltpu.VMEM((GLA_HEADS // 2, GLA_DV, LANES), F32),
            pltpu.VMEM((tl // GLA_CHUNK, GLA_HEADS // 2, GLA_DV, LANES), BF16),
        ],
        compiler_params=pltpu.CompilerParams(
            dimension_semantics=("parallel", "arbitrary"), vmem_limit_bytes=VMEM_LIMIT),
        name="gla",
    )(qe, ke, kd, dec, gv, sr, gn)


def _swa_kernel(sink_ref, q_ref, kc_ref, kp_ref, vc_ref, vp_ref, bias_ref, o_ref, *, layer):
    W = SWA_WINDOW
    tq = q_ref.shape[0]
    first = pl.program_id(1) == 0
    lo = _lane_lo((W, LANES))
    causal = (lax.broadcasted_iota(jnp.int32, (W, W), 1)
              <= lax.broadcasted_iota(jnp.int32, (W, W), 0))
    nt = (((1,), (1,)), ((), ()))
    pairs_per_kv = SWA_GROUP // 2
    units = [(j, kv) for j in range(tq // W) for kv in range(SWA_KV_HEADS)]
    rows_of = lambda j: slice(j * W, (j + 1) * W)
    lanes_of = lambda p: slice(p * LANES, (p + 1) * LANES)

    def window(cur_ref, prev_ref, j, kv):
        prev = prev_ref[:, lanes_of(kv)] if j == 0 else cur_ref[rows_of(j - 1), lanes_of(kv)]
        return jnp.concatenate([prev, cur_ref[rows_of(j), lanes_of(kv)]], axis=0)

    scores, outs = {}, {}

    def score_stage(j, kv):
        parts = []
        for p in range(kv * pairs_per_kv, (kv + 1) * pairs_per_kv):
            qp = q_ref[rows_of(j), lanes_of(p)]
            parts.append(jnp.where(lo, qp, jnp.zeros_like(qp)))
            parts.append(jnp.where(lo, jnp.zeros_like(qp), qp))
        qs = jnp.concatenate(parts, axis=0)
        scores[j, kv] = lax.dot_general(qs, window(kc_ref, kp_ref, j, kv), nt,
                                        preferred_element_type=F32)

    def softmax_stage(j, kv):
        s_all = scores.pop((j, kv))
        pes, denoms = [], []
        for r in range(SWA_GROUP):
            head = kv * SWA_GROUP + r
            table = jnp.where(first, SWA_HEADS + head, head) if j == 0 else head
            s2 = s_all[r * W:(r + 1) * W]
            s = jnp.where(causal, s2[:, W:], s2[:, :W]) + bias_ref[table]
            sink = sink_ref[layer, head]
            m = jnp.maximum(jnp.max(s, axis=-1, keepdims=True), sink)
            e = jnp.exp(s - m)
            denoms.append(jnp.sum(e, axis=-1, keepdims=True) + jnp.exp(sink - m))
            eb = e.astype(BF16)
            zero = jnp.zeros_like(eb)
            pes.append(jnp.concatenate(
                [jnp.where(causal, zero, eb), jnp.where(causal, eb, zero)], axis=1))
        o_all = jnp.dot(jnp.concatenate(pes, axis=0), window(vc_ref, vp_ref, j, kv),
                        preferred_element_type=F32)
        for r in range(SWA_GROUP):
            outs[j, kv * SWA_GROUP + r] = o_all[r * W:(r + 1) * W] / denoms[r]

    def store_stage(j):
        for p in range(SWA_QW // LANES):
            o_ref[rows_of(j), lanes_of(p)] = jnp.where(
                lo, outs.pop((j, 2 * p)), outs.pop((j, 2 * p + 1))).astype(o_ref.dtype)

    score_stage(*units[0])
    for i, (j, kv) in enumerate(units):
        if i + 1 < len(units):
            score_stage(*units[i + 1])
        softmax_stage(j, kv)
        if kv == SWA_KV_HEADS - 1:
            store_stage(j)


def _swa(sinks, sq, sk, sv, bias, layer, batch, seq):
    W = SWA_WINDOW
    tq = min(SWA_TQ, seq)
    nq = seq // tq
    bpt = tq // W
    bps = seq // W
    cur = lambda w: pl.BlockSpec((tq, w), lambda b, i: (b * nq + i, 0))
    prev = pl.BlockSpec((W, 2 * SWA_KW), lambda b, i: (b * bps + jnp.maximum(i * bpt - 1, 0), 0))
    return pl.pallas_call(
        functools.partial(_swa_kernel, layer=layer),
        grid=(batch, nq),
        in_specs=[
            pl.BlockSpec(memory_space=pltpu.SMEM),
            cur(SWA_QW), cur(2 * SWA_KW), prev, cur(2 * SWA_KW), prev,
            pl.BlockSpec(bias.shape, lambda b, i: (0, 0, 0)),
        ],
        out_specs=cur(SWA_QW),
        out_shape=jax.ShapeDtypeStruct((batch * seq, SWA_QW), BF16),
        compiler_params=pltpu.CompilerParams(
            dimension_semantics=("parallel", "parallel"), vmem_limit_bytes=VMEM_LIMIT),
        name="swa",
    )(sinks, sq, sk, sk, sv, sv, bias)


def _swa_bias_tables():
    W = SWA_WINDOW
    i = np.arange(W)[:, None]
    j = np.arange(W)[None, :]
    dist = np.where(j <= i, i - j, W + i - j).astype(np.float32)
    slopes = _alibi_slopes(SWA_HEADS)
    base = (-slopes[:, None, None] * dist[None]).astype(np.float32)
    first = np.where((j <= i)[None], base, -np.inf).astype(np.float32)
    return np.concatenate([base, first], axis=0)


def _ffn_kernel(x_ref, og_ref, os_ref, wo_ref, g_ref, wu_ref, cw_ref, wd_ref, y_ref,
                x1_ref, h_ref, t3_ref, ubuf_ref, gbuf_ref, halo_ref):
    tm = x_ref.shape[0]
    S = SUBLANES
    nv = tm // S
    nb = nv // S
    ng = D_MODEL // LANES
    first = pl.program_id(1) == 0

    x1 = (x_ref[...]
          + jnp.dot(og_ref[...], wo_ref[:GLA_VW, :], preferred_element_type=F32)
          + jnp.dot(os_ref[...], wo_ref[GLA_VW:, :], preferred_element_type=F32))
    x1_ref[...] = x1
    ms = jnp.mean(x1 * x1, axis=-1, keepdims=True)
    hn = x1 * lax.rsqrt(ms + EPS) * g_ref[...]
    for a in range(S):
        for b in range(nb):
            src = slice(S * (nb * a + b), S * (nb * a + b) + S)
            dst = slice(S * (S * b + a), S * (S * b + a) + S)
            for g in range(ng):
                t3_ref[g, dst, :] = hn[src, g * LANES:(g + 1) * LANES]
    for k in range(nv // 2):
        rows = []
        for v in (2 * k, 2 * k + 1):
            b, c = v // S, v % S
            rows.append(jnp.concatenate(
                [t3_ref[g, pl.ds(S * S * b + c, S, stride=S), :] for g in range(ng)], axis=1))
        h_ref[2 * S * k:2 * S * (k + 1), :] = jnp.concatenate(rows, axis=0).astype(BF16)

    sub = lax.broadcasted_iota(jnp.int32, (S, 2 * FFN_FC), 0)

    def up_stage(c):
        slot = c % 2
        h = h_ref[...]
        u = jnp.concatenate([
            jnp.dot(h, wu_ref[:, c * FFN_FC:(c + 1) * FFN_FC], preferred_element_type=F32),
            jnp.dot(h, wu_ref[:, D_FF + c * FFN_FC:D_FF + (c + 1) * FFN_FC],
                    preferred_element_type=F32)], axis=1)
        halo = jnp.where(first, 0.0, halo_ref[c])
        halo_ref[c] = u[tm - 2 * S:, :]
        fix2 = jnp.where(sub == 0, pltpu.roll(halo[:S], 1, axis=0),
                         pltpu.roll(u[tm - 2 * S:tm - S], 1, axis=0))
        fix1 = jnp.where(sub == 0, pltpu.roll(halo[S:], 1, axis=0),
                         pltpu.roll(u[tm - S:], 1, axis=0))
        ubuf_ref[slot, 0:S, :] = fix2
        ubuf_ref[slot, S:2 * S, :] = fix1
        ubuf_ref[slot, 2 * S:2 * S + tm, :] = u

    def gate_stage(c):
        slot = c % 2
        ca = slice(c * FFN_FC, (c + 1) * FFN_FC)
        cb = slice(D_FF + c * FFN_FC, D_FF + (c + 1) * FFN_FC)
        cw = jnp.concatenate([cw_ref[:, ca], cw_ref[:, cb]], axis=1)
        for r in range(0, tm, FFN_RB):
            u0 = ubuf_ref[slot, 2 * S + r:2 * S + r + FFN_RB, :]
            u1 = ubuf_ref[slot, S + r:S + r + FFN_RB, :]
            u2 = ubuf_ref[slot, r:r + FFN_RB, :]
            y = u2 * cw[0:1] + u1 * cw[1:2] + u0 * cw[2:3] + cw[3:4]
            a = y[:, :FFN_FC]
            gate = (a / (1.0 + jnp.exp(-a))) * y[:, FFN_FC:]
            gbuf_ref[r:r + FFN_RB, ca] = gate.astype(BF16)

    def down_stage(c0, c1):
        k0, k1 = c0 * FFN_FC, c1 * FFN_FC
        d = jnp.dot(gbuf_ref[:, k0:k1], wd_ref[k0:k1, :], preferred_element_type=F32)
        for g in range(ng):
            if c0 == 0:
                t3_ref[g] = d[:, g * LANES:(g + 1) * LANES]
            else:
                t3_ref[g] += d[:, g * LANES:(g + 1) * LANES]

    up_stage(0)
    group_start = 0
    for c in range(FFN_NC):
        if c + 1 < FFN_NC:
            up_stage(c + 1)
        gate_stage(c)
        if c + 1 - group_start == FFN_DOWN_GROUP or c + 1 == FFN_NC:
            down_stage(group_start, c + 1)
            group_start = c + 1

    for a in range(S):
        for b in range(nb):
            rows = slice(S * (nb * a + b), S * (nb * a + b) + S)
            ffn = jnp.concatenate(
                [t3_ref[g, pl.ds(S * S * b + a, S, stride=S), :] for g in range(ng)], axis=1)
            y_ref[rows, :] = x1_ref[rows, :] + ffn


def _out_ffn(x2, o_gla, o_swa, wo, g, wu, cw, wd, layer, batch, seq):
    tm = min(FFN_TM, seq)
    assert tm % (SUBLANES * SUBLANES) == 0
    nt = seq // tm
    row = lambda w: pl.BlockSpec((tm, w), lambda b, t: (b * nt + t, 0))
    lay = lambda a: pl.BlockSpec((None,) + a.shape[1:], lambda b, t: (layer,) + (0,) * (a.ndim - 1),
                                 pipeline_mode=pl.Buffered(1))
    return pl.pallas_call(
        _ffn_kernel,
        grid=(batch, nt),
        in_specs=[row(D_MODEL), row(GLA_VW), row(SWA_QW), lay(wo), lay(g), lay(wu), lay(cw), lay(wd)],
        out_specs=row(D_MODEL),
        out_shape=jax.ShapeDtypeStruct((batch * seq, D_MODEL), F32),
        scratch_shapes=[
            pltpu.VMEM((tm, D_MODEL), F32),
            pltpu.VMEM((tm, D_MODEL), BF16),
            pltpu.VMEM((D_MODEL // LANES, tm, LANES), F32),
            pltpu.VMEM((2, tm + 2 * SUBLANES, 2 * FFN_FC), F32),
            pltpu.VMEM((tm, D_FF), BF16),
            pltpu.VMEM((FFN_NC, 2 * SUBLANES, 2 * FFN_FC), F32),
        ],
        compiler_params=pltpu.CompilerParams(
            dimension_semantics=("parallel", "arbitrary"), vmem_limit_bytes=VMEM_LIMIT),
        name="out_ffn",
    )(x2, o_gla, o_swa, wo, g, wu, cw, wd)


def kernel(x, mix_norm, w_in, w_alpha2, b_alpha, gla_norm, q_norm, k_norm, sinks, w_out, ffn_norm,
           w_up, conv_w, conv_b, w_down):
    batch, seq, d = x.shape
    depth = w_in.shape[0]
    assert d == D_MODEL and w_in.shape[2] == P_IN
    assert seq % max(GLA_TL, SWA_TQ, FFN_TM) == 0 and (batch * seq) % IN_TM == 0

    w_pack = jnp.concatenate([
        w_in[:, :, :_R_LR], w_in[:, :, _R_SQ:], w_in[:, :, _R_LR:_R_SQ],
        jnp.zeros((depth, D_MODEL, LANES - GLA_LOWRANK), w_in.dtype)], axis=2).astype(BF16)
    wa_pad = jnp.concatenate(
        [w_alpha2, jnp.zeros((depth, LANES - GLA_LOWRANK, GLA_QK), w_alpha2.dtype)],
        axis=1).astype(BF16)
    wo = w_out.astype(BF16)
    wu = w_up.astype(BF16)
    wd = w_down.astype(BF16)
    cw = jnp.concatenate([
        conv_w, conv_b[:, None, :],
        jnp.zeros((depth, SUBLANES - CONV_K - 1, 2 * D_FF), conv_w.dtype)], axis=1)
    row3 = lambda a: a.reshape(depth, 1, a.shape[-1])
    qn2 = row3(jnp.tile(q_norm, (1, 2)))
    kn2 = row3(jnp.tile(k_norm, (1, 2)))

    pos = np.arange(GLA_TRI)
    tri = jnp.asarray(
        (pos[:, None] >= pos[None, :]) & (pos[:, None] // GLA_CHUNK == pos[None, :] // GLA_CHUNK),
        dtype=BF16)
    bias = jnp.asarray(_swa_bias_tables())

    x2 = x.reshape(batch * seq, D_MODEL)
    for l in range(depth):
        qe, ke, kd, dec, gv, sr, sq, sk, sv = _in_proj(
            x2, row3(mix_norm), w_pack, wa_pad, row3(b_alpha), qn2, kn2, tri, l)
        o_gla = _gla(qe, ke, kd, dec, gv, sr, row3(gla_norm), l, batch, seq)
        o_swa = _swa(sinks, sq, sk, sv, bias, l, batch, seq)
        x2 = _out_ffn(x2, o_gla, o_swa, wo, row3(ffn_norm), wu, cw, wd, l, batch, seq)
    return x2.reshape(batch, seq, D_MODEL)
```

```python
import functools

import numpy as np
import jax
import jax.numpy as jnp
from jax import lax
from jax.experimental import pallas as pl
from jax.experimental.pallas import tpu as pltpu

F32 = jnp.float32
BF16 = jnp.bfloat16

D_MODEL = 1024
GLA_HEADS = 4
GLA_DV = 128
GLA_DK = 64
GLA_LOWRANK = 16
GLA_TAU = 16.0
GLA_CHUNK = 64
SWA_HEADS = 8
SWA_KV_HEADS = 2
SWA_HD = 64
SWA_WINDOW = 128
D_FF = 2816
CONV_K = 3
EPS = 1e-6
LOG2E = 1.4426950408889634

GLA_QK = GLA_HEADS * GLA_DK
GLA_VW = GLA_HEADS * GLA_DV
SWA_QW = SWA_HEADS * SWA_HD
SWA_KW = SWA_KV_HEADS * SWA_HD
SWA_GROUP = SWA_HEADS // SWA_KV_HEADS

LANES = 128
SUBLANES = 8
VMEM_LIMIT = 56 * 1024 * 1024

_R_LR = 2 * GLA_QK + 2 * GLA_VW
_R_SQ = _R_LR + GLA_LOWRANK
P_IN = _R_SQ + SWA_QW + 2 * SWA_KW
_C_GQ = 0
_C_GK = _C_GQ + GLA_QK
_C_GV = _C_GK + GLA_QK
_C_GR = _C_GV + GLA_VW
_C_SQ = _C_GR + GLA_VW
_C_SK = _C_SQ + SWA_QW
_C_SV = _C_SK + SWA_KW
_C_LR = _C_SV + SWA_KW
P_PACK = _C_LR + LANES

IN_TM = 2048
IN_SUB = 512
GLA_TRI = 128
GLA_TL = 1024
SWA_TQ = 1024
FFN_TM = 512
FFN_FC = 256
FFN_NC = D_FF // FFN_FC
FFN_DOWN_GROUP = 4
FFN_RB = 64


def _alibi_slopes(n_heads):
    return np.array([2.0 ** (-8.0 * (h + 1) / n_heads) for h in range(n_heads)], dtype=np.float32)


def _lane_lo(shape):
    return lax.broadcasted_iota(jnp.int32, shape, len(shape) - 1) % LANES < (LANES // 2)


def _half_rms_inv(v):
    lo = _lane_lo(v.shape)
    sq = v * v
    ss_lo = jnp.sum(jnp.where(lo, sq, 0.0), axis=-1, keepdims=True)
    ss_hi = jnp.sum(jnp.where(lo, 0.0, sq), axis=-1, keepdims=True)
    inv_lo = lax.rsqrt(ss_lo * (1.0 / SWA_HD) + EPS)
    inv_hi = lax.rsqrt(ss_hi * (1.0 / SWA_HD) + EPS)
    return jnp.where(lo, inv_lo, inv_hi)


def _dup_halves(v):
    lo = _lane_lo(v.shape)
    swapped = pltpu.roll(v, LANES // 2, axis=1)
    return jnp.concatenate([jnp.where(lo, v, swapped), jnp.where(lo, swapped, v)], axis=1)


def _inproj_kernel(x_ref, g_ref, w_ref, wa_ref, ba_ref, qn_ref, kn_ref, tri_ref,
                   qe_ref, ke_ref, kd_ref, dec_ref, gv_ref, sr_ref, sq_ref, sk_ref, sv_ref):
    tm = x_ref.shape[0]
    C = GLA_CHUNK
    tri = tri_ref[...]
    tb = tri.shape[0]
    qn = qn_ref[...]

    def norm_stage(r0):
        x = x_ref[r0:r0 + IN_SUB, :]
        ms = jnp.mean(x * x, axis=-1, keepdims=True)
        return (x * lax.rsqrt(ms + EPS) * g_ref[...]).astype(BF16)

    def matmul_stage(r0, h):
        rows = slice(r0, r0 + IN_SUB)
        proj = lambda c0, width: jnp.dot(h, w_ref[:, c0:c0 + width], preferred_element_type=F32)
        glr = proj(_C_LR, LANES).astype(BF16)
        gv_ref[rows, :] = proj(_C_GV, GLA_VW).astype(BF16)
        pre = jnp.dot(glr, wa_ref[...], preferred_element_type=F32) + ba_ref[...]
        la = (jnp.minimum(pre, 0.0) - jnp.log(1.0 + jnp.exp(-jnp.abs(pre)))) * (1.0 / GLA_TAU)
        la_hi = la.astype(BF16)
        la_lo = (la - la_hi.astype(F32)).astype(BF16)
        r = proj(_C_GR, GLA_VW)
        b_blks = [jnp.dot(tri, la_hi[t0:t0 + tb], preferred_element_type=F32)
                  + jnp.dot(tri, la_lo[t0:t0 + tb], preferred_element_type=F32)
                  for t0 in range(0, IN_SUB, tb)]
        gq = proj(_C_GQ, GLA_QK) * (GLA_DK ** -0.5)
        gk = proj(_C_GK, GLA_QK)
        q_all = proj(_C_SQ, SWA_QW)
        kv = proj(_C_SK, 2 * SWA_KW)
        return dict(r=r, b_blks=b_blks, gq=gq, gk=gk, q_all=q_all, kv=kv)

    def tail_stage(r0, v):
        rows = slice(r0, r0 + IN_SUB)
        r = v["r"]
        sr_ref[rows, :] = (r / (1.0 + jnp.exp(-r))).astype(sr_ref.dtype)
        for i, t0 in enumerate(range(0, IN_SUB, tb)):
            for c0 in range(0, tb, C):
                crow = slice(r0 + t0 + c0, r0 + t0 + c0 + C)
                lrow = slice(t0 + c0, t0 + c0 + C)
                b = v["b_blks"][i][c0:c0 + C]
                b_last = b[C - 1:C]
                qe_ref[crow, :] = (v["gq"][lrow] * jnp.exp(b)).astype(BF16)
                ke_ref[crow, :] = (v["gk"][lrow] * jnp.exp(-b)).astype(BF16)
                kd_ref[crow, :] = (v["gk"][lrow] * jnp.exp(b_last - b)).astype(BF16)
                ci = (r0 + t0 + c0) // C
                dec_ref[ci:ci + 1, :] = jnp.exp(b_last)
        for p in range(SWA_QW // LANES):
            q = v["q_all"][:, p * LANES:(p + 1) * LANES]
            sq_ref[rows, p * LANES:(p + 1) * LANES] = (
                q * _half_rms_inv(q) * qn * (SWA_HD ** -0.5 * LOG2E)).astype(BF16)
        k = v["kv"][:, :SWA_KW]
        sk_ref[rows, :] = _dup_halves(k * _half_rms_inv(k) * kn_ref[...]).astype(BF16)
        sv_ref[rows, :] = _dup_halves(v["kv"][:, SWA_KW:]).astype(BF16)

    starts = list(range(0, tm, IN_SUB))
    hs = [norm_stage(r0) for r0 in starts]
    vals = matmul_stage(starts[0], hs[0])
    for i, r0 in enumerate(starts):
        nxt = matmul_stage(starts[i + 1], hs[i + 1]) if i + 1 < len(starts) else None
        tail_stage(r0, vals)
        vals = nxt


def _in_proj(x2, g, w_pack, wa_pad, ba, qn2, kn2, tri, layer):
    n = x2.shape[0]
    tm = min(IN_TM, n)
    row = lambda w: pl.BlockSpec((tm, w), lambda i: (i, 0))
    lay = lambda a: pl.BlockSpec((None,) + a.shape[1:], lambda i: (layer,) + (0,) * (a.ndim - 1),
                                 pipeline_mode=pl.Buffered(1))
    outs = [
        (GLA_QK, BF16), (GLA_QK, BF16), (GLA_QK, BF16), None,
        (GLA_VW, BF16), (GLA_VW, BF16), (SWA_QW, BF16), (2 * SWA_KW, BF16), (2 * SWA_KW, BF16),
    ]
    dec_spec = pl.BlockSpec((tm // GLA_CHUNK, GLA_QK), lambda i: (i, 0))
    dec_shape = jax.ShapeDtypeStruct((n // GLA_CHUNK, GLA_QK), F32)
    return pl.pallas_call(
        _inproj_kernel,
        grid=(n // tm,),
        in_specs=[row(D_MODEL), lay(g), lay(w_pack), lay(wa_pad), lay(ba), lay(qn2), lay(kn2),
                  pl.BlockSpec(tri.shape, lambda i: (0, 0))],
        out_specs=[dec_spec if o is None else row(o[0]) for o in outs],
        out_shape=[dec_shape if o is None else jax.ShapeDtypeStruct((n, o[0]), o[1]) for o in outs],
        compiler_params=pltpu.CompilerParams(
            dimension_semantics=("parallel",), vmem_limit_bytes=VMEM_LIMIT),
        name="in_proj",
    )(x2, g, w_pack, wa_pad, ba, qn2, kn2, tri)


def _gla_kernel(qe_ref, ke_ref, kd_ref, dec_ref, v_ref, sr_ref, gn_ref, o_ref, st_ref, sall_ref):
    tl = v_ref.shape[0]
    C = GLA_CHUNK
    nc = tl // C
    npair = GLA_HEADS // 2
    units = [(c, p) for c in range(nc) for p in range(npair)]
    rows_of = lambda c: slice(c * C, (c + 1) * C)
    lanes_of = lambda p: slice(p * LANES, (p + 1) * LANES)
    vcol_of = lambda hd: slice(hd * GLA_DV, (hd + 1) * GLA_DV)
    nt = (((1,), (1,)), ((), ()))
    tn = (((0,), (0,)), ((), ()))

    @pl.when(pl.program_id(1) == 0)
    def _():
        st_ref[...] = jnp.zeros_like(st_ref)

    ri = lax.broadcasted_iota(jnp.int32, (2 * C, 2 * C), 0)
    ci = lax.broadcasted_iota(jnp.int32, (2 * C, 2 * C), 1)
    blockdiag_causal = (ri // C == ci // C) & (ci % C <= ri % C)
    gn = gn_ref[...]
    lo = _lane_lo((C, LANES))

    def split_pair(ref, c, p):
        x = ref[rows_of(c), lanes_of(p)]
        zero = jnp.zeros_like(x)
        return jnp.concatenate([jnp.where(lo, x, zero), jnp.where(lo, zero, x)], axis=0)

    def v_pair(c, p):
        return jnp.concatenate([v_ref[rows_of(c), vcol_of(2 * p)],
                                v_ref[rows_of(c), vcol_of(2 * p + 1)]], axis=0)

    upd = {}
    for c, p in units:
        upd[c, p] = lax.dot_general(v_pair(c, p), split_pair(kd_ref, c, p), tn,
                                    preferred_element_type=F32)
    for p in range(npair):
        st = st_ref[p]
        for c in range(nc):
            sall_ref[c, p] = st.astype(BF16)
            st = st * dec_ref[c:c + 1, lanes_of(p)] + upd.pop((c, p))
        st_ref[p] = st

    sc, out = {}, {}

    def score_stage(c):
        for p in range(npair):
            q2 = split_pair(qe_ref, c, p)
            ke = ke_ref[rows_of(c), lanes_of(p)]
            rhs = jnp.concatenate([ke, ke, sall_ref[c, p]], axis=0)
            sc[c, p] = lax.dot_general(q2, rhs, nt, preferred_element_type=F32)

    def value_stage(c):
        for p in range(npair):
            s2 = sc.pop((c, p))
            am = jnp.where(blockdiag_causal, s2[:, :2 * C], 0.0).astype(BF16)
            out[c, p] = jnp.dot(am, v_pair(c, p), preferred_element_type=F32) + s2[:, 2 * C:]

    def norm_stage(c):
        for p in range(npair):
            o = out.pop((c, p))
            o = o * lax.rsqrt(jnp.mean(o * o, axis=-1, keepdims=True) + EPS) * gn
            for half in range(2):
                vcol = vcol_of(2 * p + half)
                o_ref[rows_of(c), vcol] = (
                    o[half * C:(half + 1) * C] * sr_ref[rows_of(c), vcol]).astype(o_ref.dtype)

    for c in range(nc + 2):
        if c < nc:
            score_stage(c)
        if 0 <= c - 1 < nc:
            value_stage(c - 1)
        if 0 <= c - 2 < nc:
            norm_stage(c - 2)


def _gla(qe, ke, kd, dec, gv, sr, gn, layer, batch, seq):
    tl = min(GLA_TL, seq)
    nt = seq // tl
    row = lambda w: pl.BlockSpec((tl, w), lambda b, t: (b * nt + t, 0))
    return pl.pallas_call(
        _gla_kernel,
        grid=(batch, nt),
        in_specs=[row(GLA_QK), row(GLA_QK), row(GLA_QK),
                  pl.BlockSpec((tl // GLA_CHUNK, GLA_QK), lambda b, t: (b * nt + t, 0)),
                  row(GLA_VW), row(GLA_VW),
                  pl.BlockSpec((None, 1, GLA_DV), lambda b, t: (layer, 0, 0))],
        out_specs=row(GLA_VW),
        out_shape=jax.ShapeDtypeStruct((batch * seq, GLA_VW), BF16),
        scratch_shapes=[
            pltpu.VMEM((GLA_HEADS // 2, GLA_DV, LANES), F32),
            pltpu.VMEM((tl // GLA_CHUNK, GLA_HEADS // 2, GLA_DV, LANES), BF16),
        ],
        compiler_params=pltpu.CompilerParams(
            dimension_semantics=("parallel", "arbitrary"), vmem_limit_bytes=VMEM_LIMIT),
        name="gla",
    )(qe, ke, kd, dec, gv, sr, gn)


def _swa_kernel(sink_ref, q_ref, kc_ref, kp_ref, vc_ref, vp_ref, bias_ref, o_ref, *, layer):
    W = SWA_WINDOW
    tq = q_ref.shape[0]
    first = pl.program_id(1) == 0
    lo = _lane_lo((W, LANES))
    causal = (lax.broadcasted_iota(jnp.int32, (W, W), 1)
              <= lax.broadcasted_iota(jnp.int32, (W, W), 0))
    nt = (((1,), (1,)), ((), ()))
    pairs_per_kv = SWA_GROUP // 2
    units = [(j, kv) for j in range(tq // W) for kv in range(SWA_KV_HEADS)]
    rows_of = lambda j: slice(j * W, (j + 1) * W)
    lanes_of = lambda p: slice(p * LANES, (p + 1) * LANES)

    def window(cur_ref, prev_ref, j, kv):
        prev = prev_ref[:, lanes_of(kv)] if j == 0 else cur_ref[rows_of(j - 1), lanes_of(kv)]
        return jnp.concatenate([prev, cur_ref[rows_of(j), lanes_of(kv)]], axis=0)

    scores, outs = {}, {}

    def score_stage(j, kv):
        parts = []
        for p in range(kv * pairs_per_kv, (kv + 1) * pairs_per_kv):
            qp = q_ref[rows_of(j), lanes_of(p)]
            parts.append(jnp.where(lo, qp, jnp.zeros_like(qp)))
            parts.append(jnp.where(lo, jnp.zeros_like(qp), qp))
        qs = jnp.concatenate(parts, axis=0)
        scores[j, kv] = lax.dot_general(qs, window(kc_ref, kp_ref, j, kv), nt,
                                        preferred_element_type=F32)

    def softmax_stage(j, kv):
        s_all = scores.pop((j, kv))
        pes, denoms = [], []
        for r in range(SWA_GROUP):
            head = kv * SWA_GROUP + r
            table = jnp.where(first, SWA_HEADS + head, head) if j == 0 else head
            s2 = s_all[r * W:(r + 1) * W]
            s = jnp.where(causal, s2[:, W:], s2[:, :W]) + bias_ref[table]
            sink = sink_ref[layer, head] * LOG2E
            m = jnp.maximum(jnp.max(s, axis=-1, keepdims=True), sink)
            e = jnp.exp2(s - m)
            denoms.append(jnp.sum(e, axis=-1, keepdims=True) + jnp.exp2(sink - m))
            eb = e.astype(BF16)
            zero = jnp.zeros_like(eb)
            pes.append(jnp.concatenate(
                [jnp.where(causal, zero, eb), jnp.where(causal, eb, zero)], axis=1))
        o_all = jnp.dot(jnp.concatenate(pes, axis=0), window(vc_ref, vp_ref, j, kv),
                        preferred_element_type=F32)
        for r in range(SWA_GROUP):
            outs[j, kv * SWA_GROUP + r] = o_all[r * W:(r + 1) * W] / denoms[r]

    def store_stage(j):
        for p in range(SWA_QW // LANES):
            o_ref[rows_of(j), lanes_of(p)] = jnp.where(
                lo, outs.pop((j, 2 * p)), outs.pop((j, 2 * p + 1))).astype(o_ref.dtype)

    score_stage(*units[0])
    for i, (j, kv) in enumerate(units):
        if i + 1 < len(units):
            score_stage(*units[i + 1])
        softmax_stage(j, kv)
        if kv == SWA_KV_HEADS - 1:
            store_stage(j)


def _swa(sinks, sq, sk, sv, bias, layer, batch, seq):
    W = SWA_WINDOW
    tq = min(SWA_TQ, seq)
    nq = seq // tq
    bpt = tq // W
    bps = seq // W
    cur = lambda w: pl.BlockSpec((tq, w), lambda b, i: (b * nq + i, 0))
    prev = pl.BlockSpec((W, 2 * SWA_KW), lambda b, i: (b * bps + jnp.maximum(i * bpt - 1, 0), 0))
    return pl.pallas_call(
        functools.partial(_swa_kernel, layer=layer),
        grid=(batch, nq),
        in_specs=[
            pl.BlockSpec(memory_space=pltpu.SMEM),
            cur(SWA_QW), cur(2 * SWA_KW), prev, cur(2 * SWA_KW), prev,
            pl.BlockSpec(bias.shape, lambda b, i: (0, 0, 0)),
        ],
        out_specs=cur(SWA_QW),
        out_shape=jax.ShapeDtypeStruct((batch * seq, SWA_QW), BF16),
        compiler_params=pltpu.CompilerParams(
            dimension_semantics=("parallel", "parallel"), vmem_limit_bytes=VMEM_LIMIT),
        name="swa",
    )(sinks, sq, sk, sk, sv, sv, bias)


def _swa_bias_tables():
    W = SWA_WINDOW
    i = np.arange(W)[:, None]
    j = np.arange(W)[None, :]
    dist = np.where(j <= i, i - j, W + i - j).astype(np.float32)
    slopes = _alibi_slopes(SWA_HEADS)
    base = (-slopes[:, None, None] * dist[None] * LOG2E).astype(np.float32)
    first = np.where((j <= i)[None], base, -np.inf).astype(np.float32)
    return np.concatenate([base, first], axis=0)


def _ffn_kernel(x_ref, og_ref, os_ref, wo_ref, g_ref, wu_ref, cw_ref, wd_ref, y_ref,
                x1_ref, h_ref, t3_ref, ubuf_ref, gbuf_ref, halo_ref):
    tm = x_ref.shape[0]
    S = SUBLANES
    nv = tm // S
    nb = nv // S
    ng = D_MODEL // LANES
    first = pl.program_id(1) == 0

    x1 = (x_ref[...]
          + jnp.dot(og_ref[...], wo_ref[:GLA_VW, :], preferred_element_type=F32)
          + jnp.dot(os_ref[...], wo_ref[GLA_VW:, :], preferred_element_type=F32))
    x1_ref[...] = x1
    ms = jnp.mean(x1 * x1, axis=-1, keepdims=True)
    hn = x1 * lax.rsqrt(ms + EPS) * g_ref[...]
    for a in range(S):
        for b in range(nb):
            src = slice(S * (nb * a + b), S * (nb * a + b) + S)
            dst = slice(S * (S * b + a), S * (S * b + a) + S)
            for g in range(ng):
                t3_ref[g, dst, :] = hn[src, g * LANES:(g + 1) * LANES]
    for k in range(nv // 2):
        rows = []
        for v in (2 * k, 2 * k + 1):
            b, c = v // S, v % S
            rows.append(jnp.concatenate(
                [t3_ref[g, pl.ds(S * S * b + c, S, stride=S), :] for g in range(ng)], axis=1))
        h_ref[2 * S * k:2 * S * (k + 1), :] = jnp.concatenate(rows, axis=0).astype(BF16)

    sub = lax.broadcasted_iota(jnp.int32, (S, 2 * FFN_FC), 0)

    def up_stage(c):
        slot = c % 2
        h = h_ref[...]
        u = jnp.concatenate([
            jnp.dot(h, wu_ref[:, c * FFN_FC:(c + 1) * FFN_FC], preferred_element_type=F32),
            jnp.dot(h, wu_ref[:, D_FF + c * FFN_FC:D_FF + (c + 1) * FFN_FC],
                    preferred_element_type=F32)], axis=1)
        halo = jnp.where(first, 0.0, halo_ref[c])
        halo_ref[c] = u[tm - 2 * S:, :]
        fix2 = jnp.where(sub == 0, pltpu.roll(halo[:S], 1, axis=0),
                         pltpu.roll(u[tm - 2 * S:tm - S], 1, axis=0))
        fix1 = jnp.where(sub == 0, pltpu.roll(halo[S:], 1, axis=0),
                         pltpu.roll(u[tm - S:], 1, axis=0))
        ubuf_ref[slot, 0:S, :] = fix2
        ubuf_ref[slot, S:2 * S, :] = fix1
        ubuf_ref[slot, 2 * S:2 * S + tm, :] = u

    def gate_stage(c):
        slot = c % 2
        ca = slice(c * FFN_FC, (c + 1) * FFN_FC)
        cb = slice(D_FF + c * FFN_FC, D_FF + (c + 1) * FFN_FC)
        cw = jnp.concatenate([cw_ref[:, ca], cw_ref[:, cb]], axis=1)
        for r in range(0, tm, FFN_RB):
            u0 = ubuf_ref[slot, 2 * S + r:2 * S + r + FFN_RB, :]
            u1 = ubuf_ref[slot, S + r:S + r + FFN_RB, :]
            u2 = ubuf_ref[slot, r:r + FFN_RB, :]
            y = u2 * cw[0:1] + u1 * cw[1:2] + u0 * cw[2:3] + cw[3:4]
            a = y[:, :FFN_FC]
            gate = (a / (1.0 + jnp.exp(-a))) * y[:, FFN_FC:]
            gbuf_ref[r:r + FFN_RB, ca] = gate.astype(BF16)

    def down_stage(c0, c1):
        k0, k1 = c0 * FFN_FC, c1 * FFN_FC
        d = jnp.dot(gbuf_ref[:, k0:k1], wd_ref[k0:k1, :], preferred_element_type=F32)
        for g in range(ng):
            if c0 == 0:
                t3_ref[g] = d[:, g * LANES:(g + 1) * LANES]
            else:
                t3_ref[g] += d[:, g * LANES:(g + 1) * LANES]

    up_stage(0)
    group_start = 0
    for c in range(FFN_NC):
        if c + 1 < FFN_NC:
            up_stage(c + 1)
        gate_stage(c)
        if c + 1 - group_start == FFN_DOWN_GROUP or c + 1 == FFN_NC:
            down_stage(group_start, c + 1)
            group_start = c + 1

    for a in range(S):
        for b in range(nb):
            rows = slice(S * (nb * a + b), S * (nb * a + b) + S)
            ffn = jnp.concatenate(
                [t3_ref[g, pl.ds(S * S * b + a, S, stride=S), :] for g in range(ng)], axis=1)
            y_ref[rows, :] = x1_ref[rows, :] + ffn


def _out_ffn(x2, o_gla, o_swa, wo, g, wu, cw, wd, layer, batch, seq):
    tm = min(FFN_TM, seq)
    assert tm % (SUBLANES * SUBLANES) == 0
    nt = seq // tm
    row = lambda w: pl.BlockSpec((tm, w), lambda b, t: (b * nt + t, 0))
    lay = lambda a: pl.BlockSpec((None,) + a.shape[1:], lambda b, t: (layer,) + (0,) * (a.ndim - 1),
                                 pipeline_mode=pl.Buffered(1))
    return pl.pallas_call(
        _ffn_kernel,
        grid=(batch, nt),
        in_specs=[row(D_MODEL), row(GLA_VW), row(SWA_QW), lay(wo), lay(g), lay(wu), lay(cw), lay(wd)],
        out_specs=row(D_MODEL),
        out_shape=jax.ShapeDtypeStruct((batch * seq, D_MODEL), F32),
        scratch_shapes=[
            pltpu.VMEM((tm, D_MODEL), F32),
            pltpu.VMEM((tm, D_MODEL), BF16),
            pltpu.VMEM((D_MODEL // LANES, tm, LANES), F32),
            pltpu.VMEM((2, tm + 2 * SUBLANES, 2 * FFN_FC), F32),
            pltpu.VMEM((tm, D_FF), BF16),
            pltpu.VMEM((FFN_NC, 2 * SUBLANES, 2 * FFN_FC), F32),
        ],
        compiler_params=pltpu.CompilerParams(
            dimension_semantics=("parallel", "arbitrary"), vmem_limit_bytes=VMEM_LIMIT),
        name="out_ffn",
    )(x2, o_gla, o_swa, wo, g, wu, cw, wd)


def kernel(x, mix_norm, w_in, w_alpha2, b_alpha, gla_norm, q_norm, k_norm, sinks, w_out, ffn_norm,
           w_up, conv_w, conv_b, w_down):
    batch, seq, d = x.shape
    depth = w_in.shape[0]
    assert d == D_MODEL and w_in.shape[2] == P_IN
    assert seq % max(GLA_TL, SWA_TQ, FFN_TM) == 0 and (batch * seq) % IN_TM == 0

    w_in_b = w_in.astype(BF16)
    w_pack = jnp.concatenate([
        w_in_b[:, :, :_R_LR], w_in_b[:, :, _R_SQ:], w_in_b[:, :, _R_LR:_R_SQ],
        jnp.zeros((depth, D_MODEL, LANES - GLA_LOWRANK), BF16)], axis=2)
    wa_pad = jnp.concatenate(
        [w_alpha2, jnp.zeros((depth, LANES - GLA_LOWRANK, GLA_QK), w_alpha2.dtype)],
        axis=1).astype(BF16)
    wo = w_out.astype(BF16)
    wu = w_up.astype(BF16)
    wd = w_down.astype(BF16)
    cw = jnp.concatenate([
        conv_w, conv_b[:, None, :],
        jnp.zeros((depth, SUBLANES - CONV_K - 1, 2 * D_FF), conv_w.dtype)], axis=1)
    row3 = lambda a: a.reshape(depth, 1, a.shape[-1])
    qn2 = row3(jnp.tile(q_norm, (1, 2)))
    kn2 = row3(jnp.tile(k_norm, (1, 2)))

    pos = np.arange(GLA_TRI)
    tri = jnp.asarray(
        (pos[:, None] >= pos[None, :]) & (pos[:, None] // GLA_CHUNK == pos[None, :] // GLA_CHUNK),
        dtype=BF16)
    bias = jnp.asarray(_swa_bias_tables())

    x2 = x.reshape(batch * seq, D_MODEL)
    for l in range(depth):
        qe, ke, kd, dec, gv, sr, sq, sk, sv = _in_proj(
            x2, row3(mix_norm), w_pack, wa_pad, row3(b_alpha), qn2, kn2, tri, l)
        o_gla = _gla(qe, ke, kd, dec, gv, sr, row3(gla_norm), l, batch, seq)
        o_swa = _swa(sinks, sq, sk, sv, bias, l, batch, seq)
        x2 = _out_ffn(x2, o_gla, o_swa, wo, row3(ffn_norm), wu, cw, wd, l, batch, seq)
    return x2.reshape(batch, seq, D_MODEL)
```

```python
import functools

import numpy as np
import jax
import jax.numpy as jnp
from jax import lax
from jax.experimental import pallas as pl
from jax.experimental.pallas import tpu as pltpu

F32 = jnp.float32
BF16 = jnp.bfloat16

D_MODEL = 1024
GLA_HEADS = 4
GLA_DV = 128
GLA_DK = 64
GLA_LOWRANK = 16
GLA_TAU = 16.0
GLA_CHUNK = 64
SWA_HEADS = 8
SWA_KV_HEADS = 2
SWA_HD = 64
SWA_WINDOW = 128
D_FF = 2816
CONV_K = 3
EPS = 1e-6
LOG2E = 1.4426950408889634

GLA_QK = GLA_HEADS * GLA_DK
GLA_VW = GLA_HEADS * GLA_DV
SWA_QW = SWA_HEADS * SWA_HD
SWA_KW = SWA_KV_HEADS * SWA_HD
SWA_GROUP = SWA_HEADS // SWA_KV_HEADS

LANES = 128
SUBLANES = 8
VMEM_LIMIT = 56 * 1024 * 1024

_R_LR = 2 * GLA_QK + 2 * GLA_VW
_R_SQ = _R_LR + GLA_LOWRANK
P_IN = _R_SQ + SWA_QW + 2 * SWA_KW
_C_GQ = 0
_C_GK = _C_GQ + GLA_QK
_C_GV = _C_GK + GLA_QK
_C_GR = _C_GV + GLA_VW
_C_SQ = _C_GR + GLA_VW
_C_SK = _C_SQ + SWA_QW
_C_SV = _C_SK + SWA_KW
_C_LR = _C_SV + SWA_KW
P_PACK = _C_LR + LANES

IN_TM = 2048
IN_SUB = 512
GLA_TRI = 128
GLA_TL = 1024
SWA_TQ = 1024
FFN_TM = 512
FFN_FC = 256
FFN_NC = D_FF // FFN_FC
FFN_DOWN_GROUP = 4
FFN_RB = 64


def _alibi_slopes(n_heads):
    return np.array([2.0 ** (-8.0 * (h + 1) / n_heads) for h in range(n_heads)], dtype=np.float32)


def _lane_lo(shape):
    return lax.broadcasted_iota(jnp.int32, shape, len(shape) - 1) % LANES < (LANES // 2)


def _half_rms_inv(v):
    lo = _lane_lo(v.shape)
    sq = v * v
    ss_lo = jnp.sum(jnp.where(lo, sq, 0.0), axis=-1, keepdims=True)
    ss_hi = jnp.sum(jnp.where(lo, 0.0, sq), axis=-1, keepdims=True)
    inv_lo = lax.rsqrt(ss_lo * (1.0 / SWA_HD) + EPS)
    inv_hi = lax.rsqrt(ss_hi * (1.0 / SWA_HD) + EPS)
    return jnp.where(lo, inv_lo, inv_hi)


def _dup_halves(v):
    lo = _lane_lo(v.shape)
    swapped = pltpu.roll(v, LANES // 2, axis=1)
    return jnp.concatenate([jnp.where(lo, v, swapped), jnp.where(lo, swapped, v)], axis=1)


def _inproj_kernel(x_ref, g_ref, w_ref, wa_ref, ba_ref, qn_ref, kn_ref, tri_ref,
                   qe_ref, ke_ref, kd_ref, dec_ref, gv_ref, sr_ref, sq_ref, sk_ref, sv_ref):
    tm = x_ref.shape[0]
    C = GLA_CHUNK
    tri = tri_ref[...]
    tb = tri.shape[0]
    qn = qn_ref[...]

    def norm_stage(r0):
        x = x_ref[r0:r0 + IN_SUB, :]
        ms = jnp.mean(x * x, axis=-1, keepdims=True)
        return (x * lax.rsqrt(ms + EPS) * g_ref[...]).astype(BF16)

    def matmul_stage(r0, h):
        rows = slice(r0, r0 + IN_SUB)
        proj = lambda c0, width: jnp.dot(h, w_ref[:, c0:c0 + width], preferred_element_type=F32)
        glr = proj(_C_LR, LANES).astype(BF16)
        gq = proj(_C_GQ, GLA_QK) * (GLA_DK ** -0.5)
        pre = jnp.dot(glr, wa_ref[...], preferred_element_type=F32) + ba_ref[...]
        la = (jnp.minimum(pre, 0.0) - jnp.log(1.0 + jnp.exp(-jnp.abs(pre)))) * (1.0 / GLA_TAU)
        la_hi = la.astype(BF16)
        la_lo = (la - la_hi.astype(F32)).astype(BF16)
        gk = proj(_C_GK, GLA_QK)
        b_blks = [jnp.dot(tri, la_hi[t0:t0 + tb], preferred_element_type=F32)
                  + jnp.dot(tri, la_lo[t0:t0 + tb], preferred_element_type=F32)
                  for t0 in range(0, IN_SUB, tb)]
        q_all = proj(_C_SQ, SWA_QW)
        kv = proj(_C_SK, 2 * SWA_KW)
        r = proj(_C_GR, GLA_VW)
        gv_ref[rows, :] = proj(_C_GV, GLA_VW).astype(BF16)
        return dict(r=r, b_blks=b_blks, gq=gq, gk=gk, q_all=q_all, kv=kv)

    def tail_stage(r0, v):
        rows = slice(r0, r0 + IN_SUB)
        r = v["r"]
        sr_ref[rows, :] = (r / (1.0 + jnp.exp(-r))).astype(sr_ref.dtype)
        for i, t0 in enumerate(range(0, IN_SUB, tb)):
            for c0 in range(0, tb, C):
                crow = slice(r0 + t0 + c0, r0 + t0 + c0 + C)
                lrow = slice(t0 + c0, t0 + c0 + C)
                b = v["b_blks"][i][c0:c0 + C]
                b_last = b[C - 1:C]
                qe_ref[crow, :] = (v["gq"][lrow] * jnp.exp(b)).astype(BF16)
                ke_ref[crow, :] = (v["gk"][lrow] * jnp.exp(-b)).astype(BF16)
                kd_ref[crow, :] = (v["gk"][lrow] * jnp.exp(b_last - b)).astype(BF16)
                ci = (r0 + t0 + c0) // C
                dec_ref[ci:ci + 1, :] = jnp.exp(b_last)
        for p in range(SWA_QW // LANES):
            q = v["q_all"][:, p * LANES:(p + 1) * LANES]
            sq_ref[rows, p * LANES:(p + 1) * LANES] = (
                q * _half_rms_inv(q) * qn * (SWA_HD ** -0.5 * LOG2E)).astype(BF16)
        k = v["kv"][:, :SWA_KW]
        sk_ref[rows, :] = _dup_halves(k * _half_rms_inv(k) * kn_ref[...]).astype(BF16)
        sv_ref[rows, :] = _dup_halves(v["kv"][:, SWA_KW:]).astype(BF16)

    starts = list(range(0, tm, IN_SUB))
    hs = [norm_stage(r0) for r0 in starts]
    vals = matmul_stage(starts[0], hs[0])
    for i, r0 in enumerate(starts):
        nxt = matmul_stage(starts[i + 1], hs[i + 1]) if i + 1 < len(starts) else None
        tail_stage(r0, vals)
        vals = nxt


def _in_proj(x2, g, w_pack, wa_pad, ba, qn2, kn2, tri, layer):
    n = x2.shape[0]
    tm = min(IN_TM, n)
    row = lambda w: pl.BlockSpec((tm, w), lambda i: (i, 0))
    lay = lambda a: pl.BlockSpec((None,) + a.shape[1:], lambda i: (layer,) + (0,) * (a.ndim - 1),
                                 pipeline_mode=pl.Buffered(1))
    outs = [
        (GLA_QK, BF16), (GLA_QK, BF16), (GLA_QK, BF16), None,
        (GLA_VW, BF16), (GLA_VW, BF16), (SWA_QW, BF16), (2 * SWA_KW, BF16), (2 * SWA_KW, BF16),
    ]
    dec_spec = pl.BlockSpec((tm // GLA_CHUNK, GLA_QK), lambda i: (i, 0))
    dec_shape = jax.ShapeDtypeStruct((n // GLA_CHUNK, GLA_QK), F32)
    return pl.pallas_call(
        _inproj_kernel,
        grid=(n // tm,),
        in_specs=[row(D_MODEL), lay(g), lay(w_pack), lay(wa_pad), lay(ba), lay(qn2), lay(kn2),
                  pl.BlockSpec(tri.shape, lambda i: (0, 0))],
        out_specs=[dec_spec if o is None else row(o[0]) for o in outs],
        out_shape=[dec_shape if o is None else jax.ShapeDtypeStruct((n, o[0]), o[1]) for o in outs],
        compiler_params=pltpu.CompilerParams(
            dimension_semantics=("parallel",), vmem_limit_bytes=VMEM_LIMIT),
        name="in_proj",
    )(x2, g, w_pack, wa_pad, ba, qn2, kn2, tri)


def _gla_kernel(qe_ref, ke_ref, kd_ref, dec_ref, v_ref, sr_ref, gn_ref, o_ref, st_ref, sall_ref):
    tl = v_ref.shape[0]
    C = GLA_CHUNK
    nc = tl // C
    npair = GLA_HEADS // 2
    units = [(c, p) for c in range(nc) for p in range(npair)]
    rows_of = lambda c: slice(c * C, (c + 1) * C)
    lanes_of = lambda p: slice(p * LANES, (p + 1) * LANES)
    vcol_of = lambda hd: slice(hd * GLA_DV, (hd + 1) * GLA_DV)
    nt = (((1,), (1,)), ((), ()))
    tn = (((0,), (0,)), ((), ()))

    @pl.when(pl.program_id(1) == 0)
    def _():
        st_ref[...] = jnp.zeros_like(st_ref)

    ri = lax.broadcasted_iota(jnp.int32, (2 * C, 2 * C), 0)
    ci = lax.broadcasted_iota(jnp.int32, (2 * C, 2 * C), 1)
    blockdiag_causal = (ri // C == ci // C) & (ci % C <= ri % C)
    gn = gn_ref[...]
    lo = _lane_lo((C, LANES))

    def split_pair(ref, c, p):
        x = ref[rows_of(c), lanes_of(p)]
        zero = jnp.zeros_like(x)
        return jnp.concatenate([jnp.where(lo, x, zero), jnp.where(lo, zero, x)], axis=0)

    def v_pair(c, p):
        return jnp.concatenate([v_ref[rows_of(c), vcol_of(2 * p)],
                                v_ref[rows_of(c), vcol_of(2 * p + 1)]], axis=0)

    upd = {}
    for c, p in units:
        upd[c, p] = lax.dot_general(v_pair(c, p), split_pair(kd_ref, c, p), tn,
                                    preferred_element_type=F32)
    for p in range(npair):
        st = st_ref[p]
        for c in range(nc):
            sall_ref[c, p] = st.astype(BF16)
            st = st * dec_ref[c:c + 1, lanes_of(p)] + upd.pop((c, p))
        st_ref[p] = st

    sc, out = {}, {}

    def score_stage(c):
        for p in range(npair):
            q2 = split_pair(qe_ref, c, p)
            ke = ke_ref[rows_of(c), lanes_of(p)]
            rhs = jnp.concatenate([ke, ke, sall_ref[c, p]], axis=0)
            sc[c, p] = lax.dot_general(q2, rhs, nt, preferred_element_type=F32)

    def value_stage(c):
        for p in range(npair):
            s2 = sc.pop((c, p))
            am = jnp.where(blockdiag_causal, s2[:, :2 * C], 0.0).astype(BF16)
            out[c, p] = jnp.dot(am, v_pair(c, p), preferred_element_type=F32) + s2[:, 2 * C:]

    def norm_stage(c):
        for p in range(npair):
            o = out.pop((c, p))
            o = o * lax.rsqrt(jnp.mean(o * o, axis=-1, keepdims=True) + EPS) * gn
            for half in range(2):
                vcol = vcol_of(2 * p + half)
                o_ref[rows_of(c), vcol] = (
                    o[half * C:(half + 1) * C] * sr_ref[rows_of(c), vcol]).astype(o_ref.dtype)

    for c in range(nc + 2):
        if c < nc:
            score_stage(c)
        if 0 <= c - 1 < nc:
            value_stage(c - 1)
        if 0 <= c - 2 < nc:
            norm_stage(c - 2)


def _gla(qe, ke, kd, dec, gv, sr, gn, layer, batch, seq):
    tl = min(GLA_TL, seq)
    nt = seq // tl
    row = lambda w: pl.BlockSpec((tl, w), lambda b, t: (b * nt + t, 0))
    return pl.pallas_call(
        _gla_kernel,
        grid=(batch, nt),
        in_specs=[row(GLA_QK), row(GLA_QK), row(GLA_QK),
                  pl.BlockSpec((tl // GLA_CHUNK, GLA_QK), lambda b, t: (b * nt + t, 0)),
                  row(GLA_VW), row(GLA_VW),
                  pl.BlockSpec((None, 1, GLA_DV), lambda b, t: (layer, 0, 0))],
        out_specs=row(GLA_VW),
        out_shape=jax.ShapeDtypeStruct((batch * seq, GLA_VW), BF16),
        scratch_shapes=[
            pltpu.VMEM((GLA_HEADS // 2, GLA_DV, LANES), F32),
            pltpu.VMEM((tl // GLA_CHUNK, GLA_HEADS // 2, GLA_DV, LANES), BF16),
        ],
        compiler_params=pltpu.CompilerParams(
            dimension_semantics=("parallel", "arbitrary"), vmem_limit_bytes=VMEM_LIMIT),
        name="gla",
    )(qe, ke, kd, dec, gv, sr, gn)


def _swa_kernel(sink_ref, q_ref, kc_ref, kp_ref, vc_ref, vp_ref, bias_ref, o_ref, *, layer):
    W = SWA_WINDOW
    tq = q_ref.shape[0]
    first = pl.program_id(1) == 0
    lo = _lane_lo((W, LANES))
    causal = (lax.broadcasted_iota(jnp.int32, (W, W), 1)
              <= lax.broadcasted_iota(jnp.int32, (W, W), 0))
    nt = (((1,), (1,)), ((), ()))
    pairs_per_kv = SWA_GROUP // 2
    units = [(j, kv) for j in range(tq // W) for kv in range(SWA_KV_HEADS)]
    rows_of = lambda j: slice(j * W, (j + 1) * W)
    lanes_of = lambda p: slice(p * LANES, (p + 1) * LANES)

    def window(cur_ref, prev_ref, j, kv):
        prev = prev_ref[:, lanes_of(kv)] if j == 0 else cur_ref[rows_of(j - 1), lanes_of(kv)]
        return jnp.concatenate([prev, cur_ref[rows_of(j), lanes_of(kv)]], axis=0)

    scores, outs = {}, {}

    def score_stage(j, kv):
        parts = []
        for p in range(kv * pairs_per_kv, (kv + 1) * pairs_per_kv):
            qp = q_ref[rows_of(j), lanes_of(p)]
            parts.append(jnp.where(lo, qp, jnp.zeros_like(qp)))
            parts.append(jnp.where(lo, jnp.zeros_like(qp), qp))
        qs = jnp.concatenate(parts, axis=0)
        scores[j, kv] = lax.dot_general(qs, window(kc_ref, kp_ref, j, kv), nt,
                                        preferred_element_type=F32)

    def softmax_stage(j, kv):
        s_all = scores.pop((j, kv))
        pes, denoms = [], []
        for r in range(SWA_GROUP):
            head = kv * SWA_GROUP + r
            table = jnp.where(first, SWA_HEADS + head, head) if j == 0 else head
            s2 = s_all[r * W:(r + 1) * W]
            s = jnp.where(causal, s2[:, W:], s2[:, :W]) + bias_ref[table]
            sink = sink_ref[layer, head] * LOG2E
            m = jnp.maximum(jnp.max(s, axis=-1, keepdims=True), sink)
            e = jnp.exp2(s - m)
            denoms.append(jnp.sum(e, axis=-1, keepdims=True) + jnp.exp2(sink - m))
            eb = e.astype(BF16)
            zero = jnp.zeros_like(eb)
            pes.append(jnp.concatenate(
                [jnp.where(causal, zero, eb), jnp.where(causal, eb, zero)], axis=1))
        o_all = jnp.dot(jnp.concatenate(pes, axis=0), window(vc_ref, vp_ref, j, kv),
                        preferred_element_type=F32)
        for r in range(SWA_GROUP):
            outs[j, kv * SWA_GROUP + r] = o_all[r * W:(r + 1) * W] / denoms[r]

    def store_stage(j):
        for p in range(SWA_QW // LANES):
            o_ref[rows_of(j), lanes_of(p)] = jnp.where(
                lo, outs.pop((j, 2 * p)), outs.pop((j, 2 * p + 1))).astype(o_ref.dtype)

    score_stage(*units[0])
    for i, (j, kv) in enumerate(units):
        if i + 1 < len(units):
            score_stage(*units[i + 1])
        softmax_stage(j, kv)
        if kv == SWA_KV_HEADS - 1:
            store_stage(j)


def _swa(sinks, sq, sk, sv, bias, layer, batch, seq):
    W = SWA_WINDOW
    tq = min(SWA_TQ, seq)
    nq = seq // tq
    bpt = tq // W
    bps = seq // W
    cur = lambda w: pl.BlockSpec((tq, w), lambda b, i: (b * nq + i, 0))
    prev = pl.BlockSpec((W, 2 * SWA_KW), lambda b, i: (b * bps + jnp.maximum(i * bpt - 1, 0), 0))
    return pl.pallas_call(
        functools.partial(_swa_kernel, layer=layer),
        grid=(batch, nq),
        in_specs=[
            pl.BlockSpec(memory_space=pltpu.SMEM),
            cur(SWA_QW), cur(2 * SWA_KW), prev, cur(2 * SWA_KW), prev,
            pl.BlockSpec(bias.shape, lambda b, i: (0, 0, 0)),
        ],
        out_specs=cur(SWA_QW),
        out_shape=jax.ShapeDtypeStruct((batch * seq, SWA_QW), BF16),
        compiler_params=pltpu.CompilerParams(
            dimension_semantics=("parallel", "parallel"), vmem_limit_bytes=VMEM_LIMIT),
        name="swa",
    )(sinks, sq, sk, sk, sv, sv, bias)


def _swa_bias_tables():
    W = SWA_WINDOW
    i = np.arange(W)[:, None]
    j = np.arange(W)[None, :]
    dist = np.where(j <= i, i - j, W + i - j).astype(np.float32)
    slopes = _alibi_slopes(SWA_HEADS)
    base = (-slopes[:, None, None] * dist[None] * LOG2E).astype(np.float32)
    first = np.where((j <= i)[None], base, -np.inf).astype(np.float32)
    return np.concatenate([base, first], axis=0)


def _ffn_kernel(x_ref, og_ref, os_ref, wo_ref, g_ref, wu_ref, cw_ref, wd_ref, y_ref,
                x1_ref, h_ref, t3_ref, ubuf_ref, gbuf_ref, halo_ref):
    tm = x_ref.shape[0]
    S = SUBLANES
    nv = tm // S
    nb = nv // S
    ng = D_MODEL // LANES
    first = pl.program_id(1) == 0

    x1 = (x_ref[...]
          + jnp.dot(og_ref[...], wo_ref[:GLA_VW, :], preferred_element_type=F32)
          + jnp.dot(os_ref[...], wo_ref[GLA_VW:, :], preferred_element_type=F32))
    x1_ref[...] = x1
    ms = jnp.mean(x1 * x1, axis=-1, keepdims=True)
    hn = x1 * lax.rsqrt(ms + EPS) * g_ref[...]
    for a in range(S):
        for b in range(nb):
            src = slice(S * (nb * a + b), S * (nb * a + b) + S)
            dst = slice(S * (S * b + a), S * (S * b + a) + S)
            for g in range(ng):
                t3_ref[g, dst, :] = hn[src, g * LANES:(g + 1) * LANES]
    for k in range(nv // 2):
        rows = []
        for v in (2 * k, 2 * k + 1):
            b, c = v // S, v % S
            rows.append(jnp.concatenate(
                [t3_ref[g, pl.ds(S * S * b + c, S, stride=S), :] for g in range(ng)], axis=1))
        h_ref[2 * S * k:2 * S * (k + 1), :] = jnp.concatenate(rows, axis=0).astype(BF16)

    sub = lax.broadcasted_iota(jnp.int32, (S, 2 * FFN_FC), 0)

    def up_stage(c):
        slot = c % 2
        h = h_ref[...]
        u = jnp.concatenate([
            jnp.dot(h, wu_ref[:, c * FFN_FC:(c + 1) * FFN_FC], preferred_element_type=F32),
            jnp.dot(h, wu_ref[:, D_FF + c * FFN_FC:D_FF + (c + 1) * FFN_FC],
                    preferred_element_type=F32)], axis=1)
        halo = jnp.where(first, 0.0, halo_ref[c])
        halo_ref[c] = u[tm - 2 * S:, :]
        fix2 = jnp.where(sub == 0, pltpu.roll(halo[:S], 1, axis=0),
                         pltpu.roll(u[tm - 2 * S:tm - S], 1, axis=0))
        fix1 = jnp.where(sub == 0, pltpu.roll(halo[S:], 1, axis=0),
                         pltpu.roll(u[tm - S:], 1, axis=0))
        ubuf_ref[slot, 0:S, :] = fix2
        ubuf_ref[slot, S:2 * S, :] = fix1
        ubuf_ref[slot, 2 * S:2 * S + tm, :] = u

    def gate_stage(c):
        slot = c % 2
        ca = slice(c * FFN_FC, (c + 1) * FFN_FC)
        cb = slice(D_FF + c * FFN_FC, D_FF + (c + 1) * FFN_FC)
        cw = jnp.concatenate([cw_ref[:, ca], cw_ref[:, cb]], axis=1)
        for r in range(0, tm, FFN_RB):
            u0 = ubuf_ref[slot, 2 * S + r:2 * S + r + FFN_RB, :]
            u1 = ubuf_ref[slot, S + r:S + r + FFN_RB, :]
            u2 = ubuf_ref[slot, r:r + FFN_RB, :]
            y = u2 * cw[0:1] + u1 * cw[1:2] + u0 * cw[2:3] + cw[3:4]
            a = y[:, :FFN_FC]
            gate = (a / (1.0 + jnp.exp(-a))) * y[:, FFN_FC:]
            gbuf_ref[r:r + FFN_RB, ca] = gate.astype(BF16)

    def down_stage(c0, c1):
        k0, k1 = c0 * FFN_FC, c1 * FFN_FC
        d = jnp.dot(gbuf_ref[:, k0:k1], wd_ref[k0:k1, :], preferred_element_type=F32)
        for g in range(ng):
            if c0 == 0:
                t3_ref[g] = d[:, g * LANES:(g + 1) * LANES]
            else:
                t3_ref[g] += d[:, g * LANES:(g + 1) * LANES]

    up_stage(0)
    group_start = 0
    for c in range(FFN_NC):
        if c + 1 < FFN_NC:
            up_stage(c + 1)
        gate_stage(c)
        if c + 1 - group_start == FFN_DOWN_GROUP or c + 1 == FFN_NC:
            down_stage(group_start, c + 1)
            group_start = c + 1

    for a in range(S):
        for b in range(nb):
            rows = slice(S * (nb * a + b), S * (nb * a + b) + S)
            ffn = jnp.concatenate(
                [t3_ref[g, pl.ds(S * S * b + a, S, stride=S), :] for g in range(ng)], axis=1)
            y_ref[rows, :] = x1_ref[rows, :] + ffn


def _out_ffn(x2, o_gla, o_swa, wo, g, wu, cw, wd, layer, batch, seq):
    tm = min(FFN_TM, seq)
    assert tm % (SUBLANES * SUBLANES) == 0
    nt = seq // tm
    row = lambda w: pl.BlockSpec((tm, w), lambda b, t: (b * nt + t, 0))
    lay = lambda a: pl.BlockSpec((None,) + a.shape[1:], lambda b, t: (layer,) + (0,) * (a.ndim - 1),
                                 pipeline_mode=pl.Buffered(1))
    return pl.pallas_call(
        _ffn_kernel,
        grid=(batch, nt),
        in_specs=[row(D_MODEL), row(GLA_VW), row(SWA_QW), lay(wo), lay(g), lay(wu), lay(cw), lay(wd)],
        out_specs=row(D_MODEL),
        out_shape=jax.ShapeDtypeStruct((batch * seq, D_MODEL), F32),
        scratch_shapes=[
            pltpu.VMEM((tm, D_MODEL), F32),
            pltpu.VMEM((tm, D_MODEL), BF16),
            pltpu.VMEM((D_MODEL // LANES, tm, LANES), F32),
            pltpu.VMEM((2, tm + 2 * SUBLANES, 2 * FFN_FC), F32),
            pltpu.VMEM((tm, D_FF), BF16),
            pltpu.VMEM((FFN_NC, 2 * SUBLANES, 2 * FFN_FC), F32),
        ],
        compiler_params=pltpu.CompilerParams(
            dimension_semantics=("parallel", "arbitrary"), vmem_limit_bytes=VMEM_LIMIT),
        name="out_ffn",
    )(x2, o_gla, o_swa, wo, g, wu, cw, wd)


PACK_ROWS = 256


def _pack_kernel(w_ref, o_ref):
    w = w_ref[...]
    o_ref[:, :_R_LR] = w[:, :_R_LR].astype(BF16)
    o_ref[:, _C_SQ:_C_LR] = w[:, _R_SQ:].astype(BF16)
    tail = jnp.concatenate(
        [w[:, _R_LR:_R_SQ], jnp.zeros((w.shape[0], LANES - GLA_LOWRANK), w.dtype)], axis=1)
    o_ref[:, _C_LR:] = tail.astype(BF16)


def _pack_w_in(w_in):
    depth = w_in.shape[0]
    return pl.pallas_call(
        _pack_kernel,
        grid=(depth, D_MODEL // PACK_ROWS),
        in_specs=[pl.BlockSpec((None, PACK_ROWS, P_IN), lambda l, i: (l, i, 0))],
        out_specs=pl.BlockSpec((None, PACK_ROWS, P_PACK), lambda l, i: (l, i, 0)),
        out_shape=jax.ShapeDtypeStruct((depth, D_MODEL, P_PACK), BF16),
        name="pack_w_in",
    )(w_in)


def kernel(x, mix_norm, w_in, w_alpha2, b_alpha, gla_norm, q_norm, k_norm, sinks, w_out, ffn_norm,
           w_up, conv_w, conv_b, w_down):
    batch, seq, d = x.shape
    depth = w_in.shape[0]
    assert d == D_MODEL and w_in.shape[2] == P_IN
    assert seq % max(GLA_TL, SWA_TQ, FFN_TM) == 0 and (batch * seq) % IN_TM == 0

    w_pack = _pack_w_in(w_in)
    wa_pad = jnp.concatenate(
        [w_alpha2, jnp.zeros((depth, LANES - GLA_LOWRANK, GLA_QK), w_alpha2.dtype)],
        axis=1).astype(BF16)
    wo = w_out.astype(BF16)
    wu = w_up.astype(BF16)
    wd = w_down.astype(BF16)
    cw = jnp.concatenate([
        conv_w, conv_b[:, None, :],
        jnp.zeros((depth, SUBLANES - CONV_K - 1, 2 * D_FF), conv_w.dtype)], axis=1)
    row3 = lambda a: a.reshape(depth, 1, a.shape[-1])
    qn2 = row3(jnp.tile(q_norm, (1, 2)))
    kn2 = row3(jnp.tile(k_norm, (1, 2)))

    pos = np.arange(GLA_TRI)
    tri = jnp.asarray(
        (pos[:, None] >= pos[None, :]) & (pos[:, None] // GLA_CHUNK == pos[None, :] // GLA_CHUNK),
        dtype=BF16)
    bias = jnp.asarray(_swa_bias_tables())

    x2 = x.reshape(batch * seq, D_MODEL)
    for l in range(depth):
        qe, ke, kd, dec, gv, sr, sq, sk, sv = _in_proj(
            x2, row3(mix_norm), w_pack, wa_pad, row3(b_alpha), qn2, kn2, tri, l)
        o_gla = _gla(qe, ke, kd, dec, gv, sr, row3(gla_norm), l, batch, seq)
        o_swa = _swa(sinks, sq, sk, sv, bias, l, batch, seq)
        x2 = _out_ffn(x2, o_gla, o_swa, wo, row3(ffn_norm), wu, cw, wd, l, batch, seq)
    return x2.reshape(batch, seq, D_MODEL)
```

```python
import functools

import numpy as np
import jax
import jax.numpy as jnp
from jax import lax
from jax.experimental import pallas as pl
from jax.experimental.pallas import tpu as pltpu

F32 = jnp.float32
BF16 = jnp.bfloat16

D_MODEL = 1024
GLA_HEADS = 4
GLA_DV = 128
GLA_DK = 64
GLA_LOWRANK = 16
GLA_TAU = 16.0
GLA_CHUNK = 64
SWA_HEADS = 8
SWA_KV_HEADS = 2
SWA_HD = 64
SWA_WINDOW = 128
D_FF = 2816
CONV_K = 3
EPS = 1e-6
LOG2E = 1.4426950408889634

GLA_QK = GLA_HEADS * GLA_DK
GLA_VW = GLA_HEADS * GLA_DV
SWA_QW = SWA_HEADS * SWA_HD
SWA_KW = SWA_KV_HEADS * SWA_HD
SWA_GROUP = SWA_HEADS // SWA_KV_HEADS

LANES = 128
SUBLANES = 8
VMEM_LIMIT = 56 * 1024 * 1024

_R_LR = 2 * GLA_QK + 2 * GLA_VW
_R_SQ = _R_LR + GLA_LOWRANK
P_IN = _R_SQ + SWA_QW + 2 * SWA_KW
_C_GQ = 0
_C_GK = _C_GQ + GLA_QK
_C_GV = _C_GK + GLA_QK
_C_GR = _C_GV + GLA_VW
_C_SQ = _C_GR + GLA_VW
_C_SK = _C_SQ + SWA_QW
_C_SV = _C_SK + SWA_KW
_C_LR = _C_SV + SWA_KW
P_PACK = _C_LR + LANES

IN_TM = 2048
IN_SUB = 512
GLA_TRI = 128
GLA_TL = 2048
SWA_TQ = 2048
FFN_TM = 512
FFN_FC = 256
FFN_NC = D_FF // FFN_FC
FFN_RB = 64


def _alibi_slopes(n_heads):
    return np.array([2.0 ** (-8.0 * (h + 1) / n_heads) for h in range(n_heads)], dtype=np.float32)


def _lane_lo(shape):
    return lax.broadcasted_iota(jnp.int32, shape, len(shape) - 1) % LANES < (LANES // 2)


def _half_rms_inv(v):
    lo = _lane_lo(v.shape)
    sq = v * v
    ss_lo = jnp.sum(jnp.where(lo, sq, 0.0), axis=-1, keepdims=True)
    ss_hi = jnp.sum(jnp.where(lo, 0.0, sq), axis=-1, keepdims=True)
    inv_lo = lax.rsqrt(ss_lo * (1.0 / SWA_HD) + EPS)
    inv_hi = lax.rsqrt(ss_hi * (1.0 / SWA_HD) + EPS)
    return jnp.where(lo, inv_lo, inv_hi)


def _dup_halves(v):
    lo = _lane_lo(v.shape)
    swapped = pltpu.roll(v, LANES // 2, axis=1)
    return jnp.concatenate([jnp.where(lo, v, swapped), jnp.where(lo, swapped, v)], axis=1)


def _inproj_kernel(x_ref, g_ref, w_ref, wa_ref, ba_ref, qn_ref, kn_ref, tri_ref,
                   qe_ref, ke_ref, kd_ref, dec_ref, gv_ref, sr_ref, sq_ref, sk_ref, sv_ref):
    tm = x_ref.shape[0]
    C = GLA_CHUNK
    tri = tri_ref[...]
    tb = tri.shape[0]
    qn = qn_ref[...]

    def norm_stage(r0):
        x = x_ref[r0:r0 + IN_SUB, :]
        ms = jnp.mean(x * x, axis=-1, keepdims=True)
        return (x * lax.rsqrt(ms + EPS) * g_ref[...]).astype(BF16)

    def matmul_stage(r0, h):
        rows = slice(r0, r0 + IN_SUB)
        proj = lambda c0, width: jnp.dot(h, w_ref[:, c0:c0 + width], preferred_element_type=F32)
        glr = proj(_C_LR, LANES).astype(BF16)
        gq = proj(_C_GQ, GLA_QK) * (GLA_DK ** -0.5)
        pre = jnp.dot(glr, wa_ref[...], preferred_element_type=F32) + ba_ref[...]
        la = (jnp.minimum(pre, 0.0) - jnp.log(1.0 + jnp.exp(-jnp.abs(pre)))) * (1.0 / GLA_TAU)
        la_hi = la.astype(BF16)
        la_lo = (la - la_hi.astype(F32)).astype(BF16)
        gk = proj(_C_GK, GLA_QK)
        b_blks = [jnp.dot(tri, la_hi[t0:t0 + tb], preferred_element_type=F32)
                  + jnp.dot(tri, la_lo[t0:t0 + tb], preferred_element_type=F32)
                  for t0 in range(0, IN_SUB, tb)]
        q_all = proj(_C_SQ, SWA_QW)
        kv = proj(_C_SK, 2 * SWA_KW)
        r = proj(_C_GR, GLA_VW)
        gv_ref[rows, :] = proj(_C_GV, GLA_VW).astype(BF16)
        return dict(r=r, b_blks=b_blks, gq=gq, gk=gk, q_all=q_all, kv=kv)

    def tail_stage(r0, v):
        rows = slice(r0, r0 + IN_SUB)
        r = v["r"]
        sr_ref[rows, :] = (r / (1.0 + jnp.exp(-r))).astype(sr_ref.dtype)
        for i, t0 in enumerate(range(0, IN_SUB, tb)):
            for c0 in range(0, tb, C):
                crow = slice(r0 + t0 + c0, r0 + t0 + c0 + C)
                lrow = slice(t0 + c0, t0 + c0 + C)
                b = v["b_blks"][i][c0:c0 + C]
                b_last = b[C - 1:C]
                qe_ref[crow, :] = (v["gq"][lrow] * jnp.exp(b)).astype(BF16)
                ke_ref[crow, :] = (v["gk"][lrow] * jnp.exp(-b)).astype(BF16)
                kd_ref[crow, :] = (v["gk"][lrow] * jnp.exp(b_last - b)).astype(BF16)
                ci = (r0 + t0 + c0) // C
                dec_ref[ci:ci + 1, :] = jnp.exp(b_last)
        for p in range(SWA_QW // LANES):
            q = v["q_all"][:, p * LANES:(p + 1) * LANES]
            sq_ref[rows, p * LANES:(p + 1) * LANES] = (
                q * _half_rms_inv(q) * qn * (SWA_HD ** -0.5 * LOG2E)).astype(BF16)
        k = v["kv"][:, :SWA_KW]
        sk_ref[rows, :] = _dup_halves(k * _half_rms_inv(k) * kn_ref[...]).astype(BF16)
        sv_ref[rows, :] = _dup_halves(v["kv"][:, SWA_KW:]).astype(BF16)

    starts = list(range(0, tm, IN_SUB))
    hs = [norm_stage(r0) for r0 in starts]
    vals = matmul_stage(starts[0], hs[0])
    for i, r0 in enumerate(starts):
        nxt = matmul_stage(starts[i + 1], hs[i + 1]) if i + 1 < len(starts) else None
        tail_stage(r0, vals)
        vals = nxt


def _in_proj(x2, g, w_pack, wa_pad, ba, qn2, kn2, tri, layer):
    n = x2.shape[0]
    tm = min(IN_TM, n)
    row = lambda w: pl.BlockSpec((tm, w), lambda i: (i, 0))
    lay = lambda a: pl.BlockSpec((None,) + a.shape[1:], lambda i: (layer,) + (0,) * (a.ndim - 1),
                                 pipeline_mode=pl.Buffered(1))
    outs = [
        (GLA_QK, BF16), (GLA_QK, BF16), (GLA_QK, BF16), None,
        (GLA_VW, BF16), (GLA_VW, BF16), (SWA_QW, BF16), (2 * SWA_KW, BF16), (2 * SWA_KW, BF16),
    ]
    dec_spec = pl.BlockSpec((tm // GLA_CHUNK, GLA_QK), lambda i: (i, 0))
    dec_shape = jax.ShapeDtypeStruct((n // GLA_CHUNK, GLA_QK), F32)
    return pl.pallas_call(
        _inproj_kernel,
        grid=(n // tm,),
        in_specs=[row(D_MODEL), lay(g), lay(w_pack), lay(wa_pad), lay(ba), lay(qn2), lay(kn2),
                  pl.BlockSpec(tri.shape, lambda i: (0, 0))],
        out_specs=[dec_spec if o is None else row(o[0]) for o in outs],
        out_shape=[dec_shape if o is None else jax.ShapeDtypeStruct((n, o[0]), o[1]) for o in outs],
        compiler_params=pltpu.CompilerParams(
            dimension_semantics=("parallel",), vmem_limit_bytes=VMEM_LIMIT),
        name="in_proj",
    )(x2, g, w_pack, wa_pad, ba, qn2, kn2, tri)


def _gla_kernel(qe_ref, ke_ref, kd_ref, dec_ref, v_ref, sr_ref, gn_ref, o_ref, st_ref, sall_ref):
    tl = v_ref.shape[0]
    C = GLA_CHUNK
    nc = tl // C
    npair = GLA_HEADS // 2
    units = [(c, p) for c in range(nc) for p in range(npair)]
    rows_of = lambda c: slice(c * C, (c + 1) * C)
    lanes_of = lambda p: slice(p * LANES, (p + 1) * LANES)
    vcol_of = lambda hd: slice(hd * GLA_DV, (hd + 1) * GLA_DV)
    nt = (((1,), (1,)), ((), ()))
    tn = (((0,), (0,)), ((), ()))

    @pl.when(pl.program_id(1) == 0)
    def _():
        st_ref[...] = jnp.zeros_like(st_ref)

    ri = lax.broadcasted_iota(jnp.int32, (2 * C, 2 * C), 0)
    ci = lax.broadcasted_iota(jnp.int32, (2 * C, 2 * C), 1)
    blockdiag_causal = (ri // C == ci // C) & (ci % C <= ri % C)
    gn = gn_ref[...]
    lo = _lane_lo((C, LANES))

    def split_pair(ref, c, p):
        x = ref[rows_of(c), lanes_of(p)]
        zero = jnp.zeros_like(x)
        return jnp.concatenate([jnp.where(lo, x, zero), jnp.where(lo, zero, x)], axis=0)

    def v_pair(c, p):
        return jnp.concatenate([v_ref[rows_of(c), vcol_of(2 * p)],
                                v_ref[rows_of(c), vcol_of(2 * p + 1)]], axis=0)

    upd = {}
    for c, p in units:
        upd[c, p] = lax.dot_general(v_pair(c, p), split_pair(kd_ref, c, p), tn,
                                    preferred_element_type=F32)
    for p in range(npair):
        st = st_ref[p]
        for c in range(nc):
            sall_ref[c, p] = st.astype(BF16)
            st = st * dec_ref[c:c + 1, lanes_of(p)] + upd.pop((c, p))
        st_ref[p] = st

    sc, out = {}, {}

    def score_stage(c):
        for p in range(npair):
            q2 = split_pair(qe_ref, c, p)
            ke = ke_ref[rows_of(c), lanes_of(p)]
            rhs = jnp.concatenate([ke, ke, sall_ref[c, p]], axis=0)
            sc[c, p] = lax.dot_general(q2, rhs, nt, preferred_element_type=F32)

    def value_stage(c):
        for p in range(npair):
            s2 = sc.pop((c, p))
            am = jnp.where(blockdiag_causal, s2[:, :2 * C], 0.0).astype(BF16)
            out[c, p] = jnp.dot(am, v_pair(c, p), preferred_element_type=F32) + s2[:, 2 * C:]

    def norm_stage(c):
        for p in range(npair):
            o = out.pop((c, p))
            o = o * lax.rsqrt(jnp.mean(o * o, axis=-1, keepdims=True) + EPS) * gn
            for half in range(2):
                vcol = vcol_of(2 * p + half)
                o_ref[rows_of(c), vcol] = (
                    o[half * C:(half + 1) * C] * sr_ref[rows_of(c), vcol]).astype(o_ref.dtype)

    for c in range(nc + 2):
        if c < nc:
            score_stage(c)
        if 0 <= c - 1 < nc:
            value_stage(c - 1)
        if 0 <= c - 2 < nc:
            norm_stage(c - 2)


def _gla(qe, ke, kd, dec, gv, sr, gn, layer, batch, seq):
    tl = min(GLA_TL, seq)
    nt = seq // tl
    row = lambda w: pl.BlockSpec((tl, w), lambda b, t: (b * nt + t, 0))
    return pl.pallas_call(
        _gla_kernel,
        grid=(batch, nt),
        in_specs=[row(GLA_QK), row(GLA_QK), row(GLA_QK),
                  pl.BlockSpec((tl // GLA_CHUNK, GLA_QK), lambda b, t: (b * nt + t, 0)),
                  row(GLA_VW), row(GLA_VW),
                  pl.BlockSpec((None, 1, GLA_DV), lambda b, t: (layer, 0, 0))],
        out_specs=row(GLA_VW),
        out_shape=jax.ShapeDtypeStruct((batch * seq, GLA_VW), BF16),
        scratch_shapes=[
            pltpu.VMEM((GLA_HEADS // 2, GLA_DV, LANES), F32),
            pltpu.VMEM((tl // GLA_CHUNK, GLA_HEADS // 2, GLA_DV, LANES), BF16),
        ],
        compiler_params=pltpu.CompilerParams(
            dimension_semantics=("parallel", "arbitrary"), vmem_limit_bytes=VMEM_LIMIT),
        name="gla",
    )(qe, ke, kd, dec, gv, sr, gn)


def _swa_kernel(sink_ref, q_ref, kc_ref, kp_ref, vc_ref, vp_ref, bias_ref, o_ref, *, layer):
    W = SWA_WINDOW
    tq = q_ref.shape[0]
    first = pl.program_id(1) == 0
    lo = _lane_lo((W, LANES))
    causal = (lax.broadcasted_iota(jnp.int32, (W, W), 1)
              <= lax.broadcasted_iota(jnp.int32, (W, W), 0))
    nt = (((1,), (1,)), ((), ()))
    pairs_per_kv = SWA_GROUP // 2
    units = [(j, kv) for j in range(tq // W) for kv in range(SWA_KV_HEADS)]
    rows_of = lambda j: slice(j * W, (j + 1) * W)
    lanes_of = lambda p: slice(p * LANES, (p + 1) * LANES)

    def window(cur_ref, prev_ref, j, kv):
        prev = prev_ref[:, lanes_of(kv)] if j == 0 else cur_ref[rows_of(j - 1), lanes_of(kv)]
        return jnp.concatenate([prev, cur_ref[rows_of(j), lanes_of(kv)]], axis=0)

    scores, outs = {}, {}

    def score_stage(j, kv):
        parts = []
        for p in range(kv * pairs_per_kv, (kv + 1) * pairs_per_kv):
            qp = q_ref[rows_of(j), lanes_of(p)]
            parts.append(jnp.where(lo, qp, jnp.zeros_like(qp)))
            parts.append(jnp.where(lo, jnp.zeros_like(qp), qp))
        qs = jnp.concatenate(parts, axis=0)
        scores[j, kv] = lax.dot_general(qs, window(kc_ref, kp_ref, j, kv), nt,
                                        preferred_element_type=F32)

    def softmax_stage(j, kv):
        s_all = scores.pop((j, kv))
        pes, denoms = [], []
        for r in range(SWA_GROUP):
            head = kv * SWA_GROUP + r
            table = jnp.where(first, SWA_HEADS + head, head) if j == 0 else head
            s2 = s_all[r * W:(r + 1) * W]
            s = jnp.where(causal, s2[:, W:], s2[:, :W]) + bias_ref[table]
            sink = sink_ref[layer, head] * LOG2E
            m = jnp.maximum(jnp.max(s, axis=-1, keepdims=True), sink)
            e = jnp.exp2(s - m)
            denoms.append(jnp.sum(e, axis=-1, keepdims=True) + jnp.exp2(sink - m))
            eb = e.astype(BF16)
            zero = jnp.zeros_like(eb)
            pes.append(jnp.concatenate(
                [jnp.where(causal, zero, eb), jnp.where(causal, eb, zero)], axis=1))
        o_all = jnp.dot(jnp.concatenate(pes, axis=0), window(vc_ref, vp_ref, j, kv),
                        preferred_element_type=F32)
        for r in range(SWA_GROUP):
            outs[j, kv * SWA_GROUP + r] = o_all[r * W:(r + 1) * W] / denoms[r]

    def store_stage(j):
        for p in range(SWA_QW // LANES):
            o_ref[rows_of(j), lanes_of(p)] = jnp.where(
                lo, outs.pop((j, 2 * p)), outs.pop((j, 2 * p + 1))).astype(o_ref.dtype)

    score_stage(*units[0])
    for i, (j, kv) in enumerate(units):
        if i + 1 < len(units):
            score_stage(*units[i + 1])
        softmax_stage(j, kv)
        if kv == SWA_KV_HEADS - 1:
            store_stage(j)


def _swa(sinks, sq, sk, sv, bias, layer, batch, seq):
    W = SWA_WINDOW
    tq = min(SWA_TQ, seq)
    nq = seq // tq
    bpt = tq // W
    bps = seq // W
    cur = lambda w: pl.BlockSpec((tq, w), lambda b, i: (b * nq + i, 0))
    prev = pl.BlockSpec((W, 2 * SWA_KW), lambda b, i: (b * bps + jnp.maximum(i * bpt - 1, 0), 0))
    return pl.pallas_call(
        functools.partial(_swa_kernel, layer=layer),
        grid=(batch, nq),
        in_specs=[
            pl.BlockSpec(memory_space=pltpu.SMEM),
            cur(SWA_QW), cur(2 * SWA_KW), prev, cur(2 * SWA_KW), prev,
            pl.BlockSpec(bias.shape, lambda b, i: (0, 0, 0)),
        ],
        out_specs=cur(SWA_QW),
        out_shape=jax.ShapeDtypeStruct((batch * seq, SWA_QW), BF16),
        compiler_params=pltpu.CompilerParams(
            dimension_semantics=("parallel", "parallel"), vmem_limit_bytes=VMEM_LIMIT),
        name="swa",
    )(sinks, sq, sk, sk, sv, sv, bias)


def _swa_bias_tables():
    W = SWA_WINDOW
    i = np.arange(W)[:, None]
    j = np.arange(W)[None, :]
    dist = np.where(j <= i, i - j, W + i - j).astype(np.float32)
    slopes = _alibi_slopes(SWA_HEADS)
    base = (-slopes[:, None, None] * dist[None] * LOG2E).astype(np.float32)
    first = np.where((j <= i)[None], base, -np.inf).astype(np.float32)
    return np.concatenate([base, first], axis=0)


def _ffn_kernel(x_ref, og_ref, os_ref, wo_ref, g_ref, wu_ref, cw_ref, wd_ref, y_ref,
                x1_ref, h_ref, t3_ref, ubuf_ref, gbuf_ref, halo_ref):
    tm = x_ref.shape[0]
    S = SUBLANES
    nv = tm // S
    nb = nv // S
    ng = D_MODEL // LANES
    first = pl.program_id(1) == 0

    x1 = (x_ref[...]
          + jnp.dot(og_ref[...], wo_ref[:GLA_VW, :], preferred_element_type=F32)
          + jnp.dot(os_ref[...], wo_ref[GLA_VW:, :], preferred_element_type=F32))
    x1_ref[...] = x1
    ms = jnp.mean(x1 * x1, axis=-1, keepdims=True)
    hn = x1 * lax.rsqrt(ms + EPS) * g_ref[...]
    for a in range(S):
        for b in range(nb):
            src = slice(S * (nb * a + b), S * (nb * a + b) + S)
            dst = slice(S * (S * b + a), S * (S * b + a) + S)
            for g in range(ng):
                t3_ref[g, dst, :] = hn[src, g * LANES:(g + 1) * LANES]
    for k in range(nv // 2):
        rows = []
        for v in (2 * k, 2 * k + 1):
            b, c = v // S, v % S
            rows.append(jnp.concatenate(
                [t3_ref[g, pl.ds(S * S * b + c, S, stride=S), :] for g in range(ng)], axis=1))
        h_ref[2 * S * k:2 * S * (k + 1), :] = jnp.concatenate(rows, axis=0).astype(BF16)

    sub = lax.broadcasted_iota(jnp.int32, (S, 2 * FFN_FC), 0)

    def up_stage(c):
        slot = c % 2
        h = h_ref[...]
        u = jnp.concatenate([
            jnp.dot(h, wu_ref[:, c * FFN_FC:(c + 1) * FFN_FC], preferred_element_type=F32),
            jnp.dot(h, wu_ref[:, D_FF + c * FFN_FC:D_FF + (c + 1) * FFN_FC],
                    preferred_element_type=F32)], axis=1)
        halo = jnp.where(first, 0.0, halo_ref[c])
        halo_ref[c] = u[tm - 2 * S:, :]
        fix2 = jnp.where(sub == 0, pltpu.roll(halo[:S], 1, axis=0),
                         pltpu.roll(u[tm - 2 * S:tm - S], 1, axis=0))
        fix1 = jnp.where(sub == 0, pltpu.roll(halo[S:], 1, axis=0),
                         pltpu.roll(u[tm - S:], 1, axis=0))
        ubuf_ref[slot, 0:S, :] = fix2
        ubuf_ref[slot, S:2 * S, :] = fix1
        ubuf_ref[slot, 2 * S:2 * S + tm, :] = u

    def gate_stage(c):
        slot = c % 2
        ca = slice(c * FFN_FC, (c + 1) * FFN_FC)
        cb = slice(D_FF + c * FFN_FC, D_FF + (c + 1) * FFN_FC)
        cw = jnp.concatenate([cw_ref[:, ca], cw_ref[:, cb]], axis=1)
        for r in range(0, tm, FFN_RB):
            u0 = ubuf_ref[slot, 2 * S + r:2 * S + r + FFN_RB, :]
            u1 = ubuf_ref[slot, S + r:S + r + FFN_RB, :]
            u2 = ubuf_ref[slot, r:r + FFN_RB, :]
            y = u2 * cw[0:1] + u1 * cw[1:2] + u0 * cw[2:3] + cw[3:4]
            a = y[:, :FFN_FC]
            gate = (a / (1.0 + jnp.exp(-a))) * y[:, FFN_FC:]
            gbuf_ref[r:r + FFN_RB, ca] = gate.astype(BF16)

    up_stage(0)
    for c in range(FFN_NC):
        if c + 1 < FFN_NC:
            up_stage(c + 1)
        gate_stage(c)
    d = jnp.dot(gbuf_ref[...], wd_ref[...], preferred_element_type=F32)
    for g in range(ng):
        t3_ref[g] = d[:, g * LANES:(g + 1) * LANES]

    for a in range(S):
        for b in range(nb):
            rows = slice(S * (nb * a + b), S * (nb * a + b) + S)
            ffn = jnp.concatenate(
                [t3_ref[g, pl.ds(S * S * b + a, S, stride=S), :] for g in range(ng)], axis=1)
            y_ref[rows, :] = x1_ref[rows, :] + ffn


def _out_ffn(x2, o_gla, o_swa, wo, g, wu, cw, wd, layer, batch, seq):
    tm = min(FFN_TM, seq)
    assert tm % (SUBLANES * SUBLANES) == 0
    nt = seq // tm
    row = lambda w: pl.BlockSpec((tm, w), lambda b, t: (b * nt + t, 0))
    lay = lambda a: pl.BlockSpec((None,) + a.shape[1:], lambda b, t: (layer,) + (0,) * (a.ndim - 1),
                                 pipeline_mode=pl.Buffered(1))
    return pl.pallas_call(
        _ffn_kernel,
        grid=(batch, nt),
        in_specs=[row(D_MODEL), row(GLA_VW), row(SWA_QW), lay(wo), lay(g), lay(wu), lay(cw), lay(wd)],
        out_specs=row(D_MODEL),
        out_shape=jax.ShapeDtypeStruct((batch * seq, D_MODEL), F32),
        scratch_shapes=[
            pltpu.VMEM((tm, D_MODEL), F32),
            pltpu.VMEM((tm, D_MODEL), BF16),
            pltpu.VMEM((D_MODEL // LANES, tm, LANES), F32),
            pltpu.VMEM((2, tm + 2 * SUBLANES, 2 * FFN_FC), F32),
            pltpu.VMEM((tm, D_FF), BF16),
            pltpu.VMEM((FFN_NC, 2 * SUBLANES, 2 * FFN_FC), F32),
        ],
        compiler_params=pltpu.CompilerParams(
            dimension_semantics=("parallel", "arbitrary"), vmem_limit_bytes=VMEM_LIMIT),
        name="out_ffn",
    )(x2, o_gla, o_swa, wo, g, wu, cw, wd)


def kernel(x, mix_norm, w_in, w_alpha2, b_alpha, gla_norm, q_norm, k_norm, sinks, w_out, ffn_norm,
           w_up, conv_w, conv_b, w_down):
    batch, seq, d = x.shape
    depth = w_in.shape[0]
    assert d == D_MODEL and w_in.shape[2] == P_IN
    assert seq % max(GLA_TL, SWA_TQ, FFN_TM) == 0 and (batch * seq) % IN_TM == 0

    w_in_b = w_in.astype(BF16)
    w_pack = jnp.concatenate([
        w_in_b[:, :, :_R_LR], w_in_b[:, :, _R_SQ:], w_in_b[:, :, _R_LR:_R_SQ],
        jnp.zeros((depth, D_MODEL, LANES - GLA_LOWRANK), BF16)], axis=2)
    wa_pad = jnp.concatenate(
        [w_alpha2, jnp.zeros((depth, LANES - GLA_LOWRANK, GLA_QK), w_alpha2.dtype)],
        axis=1).astype(BF16)
    wo = w_out.astype(BF16)
    wu = w_up.astype(BF16)
    wd = w_down.astype(BF16)
    cw = jnp.concatenate([
        conv_w, conv_b[:, None, :],
        jnp.zeros((depth, SUBLANES - CONV_K - 1, 2 * D_FF), conv_w.dtype)], axis=1)
    row3 = lambda a: a.reshape(depth, 1, a.shape[-1])
    qn2 = row3(jnp.tile(q_norm, (1, 2)))
    kn2 = row3(jnp.tile(k_norm, (1, 2)))

    pos = np.arange(GLA_TRI)
    tri = jnp.asarray(
        (pos[:, None] >= pos[None, :]) & (pos[:, None] // GLA_CHUNK == pos[None, :] // GLA_CHUNK),
        dtype=BF16)
    bias = jnp.asarray(_swa_bias_tables())

    x2 = x.reshape(batch * seq, D_MODEL)
    for l in range(depth):
        qe, ke, kd, dec, gv, sr, sq, sk, sv = _in_proj(
            x2, row3(mix_norm), w_pack, wa_pad, row3(b_alpha), qn2, kn2, tri, l)
        o_gla = _gla(qe, ke, kd, dec, gv, sr, row3(gla_norm), l, batch, seq)
        o_swa = _swa(sinks, sq, sk, sv, bias, l, batch, seq)
        x2 = _out_ffn(x2, o_gla, o_swa, wo, row3(ffn_norm), wu, cw, wd, l, batch, seq)
    return x2.reshape(batch, seq, D_MODEL)
```

```python
import functools

import numpy as np
import jax
import jax.numpy as jnp
from jax import lax
from jax.experimental import pallas as pl
from jax.experimental.pallas import tpu as pltpu

F32 = jnp.float32
BF16 = jnp.bfloat16

D_MODEL = 1024
GLA_HEADS = 4
GLA_DV = 128
GLA_DK = 64
GLA_LOWRANK = 16
GLA_TAU = 16.0
GLA_CHUNK = 64
SWA_HEADS = 8
SWA_KV_HEADS = 2
SWA_HD = 64
SWA_WINDOW = 128
D_FF = 2816
CONV_K = 3
EPS = 1e-6
LOG2E = 1.4426950408889634

GLA_QK = GLA_HEADS * GLA_DK
GLA_VW = GLA_HEADS * GLA_DV
SWA_QW = SWA_HEADS * SWA_HD
SWA_KW = SWA_KV_HEADS * SWA_HD
SWA_GROUP = SWA_HEADS // SWA_KV_HEADS

LANES = 128
SUBLANES = 8
VMEM_LIMIT = 56 * 1024 * 1024

_R_LR = 2 * GLA_QK + 2 * GLA_VW
_R_SQ = _R_LR + GLA_LOWRANK
P_IN = _R_SQ + SWA_QW + 2 * SWA_KW
_C_GQ = 0
_C_GK = _C_GQ + GLA_QK
_C_GV = _C_GK + GLA_QK
_C_GR = _C_GV + GLA_VW
_C_SQ = _C_GR + GLA_VW
_C_SK = _C_SQ + SWA_QW
_C_SV = _C_SK + SWA_KW
_C_LR = _C_SV + SWA_KW
P_PACK = _C_LR + LANES

IN_TM = 2048
W_STAGE_ROWS = 128
IN_SUB = 512
GLA_TRI = 128
GLA_TL = 2048
SWA_TQ = 2048
FFN_TM = 512
FFN_FC = 256
FFN_NC = D_FF // FFN_FC
FFN_RB = 64


def _alibi_slopes(n_heads):
    return np.array([2.0 ** (-8.0 * (h + 1) / n_heads) for h in range(n_heads)], dtype=np.float32)


def _lane_lo(shape):
    return lax.broadcasted_iota(jnp.int32, shape, len(shape) - 1) % LANES < (LANES // 2)


def _half_rms_inv(v):
    lo = _lane_lo(v.shape)
    sq = v * v
    ss_lo = jnp.sum(jnp.where(lo, sq, 0.0), axis=-1, keepdims=True)
    ss_hi = jnp.sum(jnp.where(lo, 0.0, sq), axis=-1, keepdims=True)
    inv_lo = lax.rsqrt(ss_lo * (1.0 / SWA_HD) + EPS)
    inv_hi = lax.rsqrt(ss_hi * (1.0 / SWA_HD) + EPS)
    return jnp.where(lo, inv_lo, inv_hi)


def _dup_halves(v):
    lo = _lane_lo(v.shape)
    swapped = pltpu.roll(v, LANES // 2, axis=1)
    return jnp.concatenate([jnp.where(lo, v, swapped), jnp.where(lo, swapped, v)], axis=1)


def _load_packed_w_in(w_hbm, w_ref, stage_ref, sem_ref):
    n_chunks = D_MODEL // W_STAGE_ROWS

    def chunk_copy(i):
        return pltpu.make_async_copy(
            w_hbm.at[pl.ds(i * W_STAGE_ROWS, W_STAGE_ROWS)], stage_ref.at[i % 2], sem_ref.at[i % 2])

    chunk_copy(0).start()
    for i in range(n_chunks):
        if i + 1 < n_chunks:
            chunk_copy(i + 1).start()
        chunk_copy(i).wait()
        w = stage_ref[i % 2]
        rows = slice(i * W_STAGE_ROWS, (i + 1) * W_STAGE_ROWS)
        w_ref[rows, :_R_LR] = w[:, :_R_LR].astype(BF16)
        w_ref[rows, _C_SQ:_C_LR] = w[:, _R_SQ:].astype(BF16)
        tail = jnp.concatenate(
            [w[:, _R_LR:_R_SQ], jnp.zeros((W_STAGE_ROWS, LANES - GLA_LOWRANK), w.dtype)], axis=1)
        w_ref[rows, _C_LR:] = tail.astype(BF16)


def _inproj_kernel(x_ref, g_ref, w_hbm, wa_ref, ba_ref, qn_ref, kn_ref, tri_ref,
                   qe_ref, ke_ref, kd_ref, dec_ref, gv_ref, sr_ref, sq_ref, sk_ref, sv_ref,
                   w_ref, stage_ref, sem_ref, *, layer):
    tm = x_ref.shape[0]
    C = GLA_CHUNK
    tri = tri_ref[...]
    tb = tri.shape[0]
    qn = qn_ref[...]

    @pl.when(pl.program_id(0) == 0)
    def _():
        _load_packed_w_in(w_hbm.at[layer], w_ref, stage_ref, sem_ref)

    def norm_stage(r0):
        x = x_ref[r0:r0 + IN_SUB, :]
        ms = jnp.mean(x * x, axis=-1, keepdims=True)
        return (x * lax.rsqrt(ms + EPS) * g_ref[...]).astype(BF16)

    def matmul_stage(r0, h):
        rows = slice(r0, r0 + IN_SUB)
        proj = lambda c0, width: jnp.dot(h, w_ref[:, c0:c0 + width], preferred_element_type=F32)
        glr = proj(_C_LR, LANES).astype(BF16)
        gq = proj(_C_GQ, GLA_QK) * (GLA_DK ** -0.5)
        pre = jnp.dot(glr, wa_ref[...], preferred_element_type=F32) + ba_ref[...]
        la = (jnp.minimum(pre, 0.0) - jnp.log(1.0 + jnp.exp(-jnp.abs(pre)))) * (1.0 / GLA_TAU)
        la_hi = la.astype(BF16)
        la_lo = (la - la_hi.astype(F32)).astype(BF16)
        gk = proj(_C_GK, GLA_QK)
        b_blks = [jnp.dot(tri, la_hi[t0:t0 + tb], preferred_element_type=F32)
                  + jnp.dot(tri, la_lo[t0:t0 + tb], preferred_element_type=F32)
                  for t0 in range(0, IN_SUB, tb)]
        q_all = proj(_C_SQ, SWA_QW)
        kv = proj(_C_SK, 2 * SWA_KW)
        r = proj(_C_GR, GLA_VW)
        gv_ref[rows, :] = proj(_C_GV, GLA_VW).astype(BF16)
        return dict(r=r, b_blks=b_blks, gq=gq, gk=gk, q_all=q_all, kv=kv)

    def tail_stage(r0, v):
        rows = slice(r0, r0 + IN_SUB)
        r = v["r"]
        sr_ref[rows, :] = (r / (1.0 + jnp.exp(-r))).astype(sr_ref.dtype)
        for i, t0 in enumerate(range(0, IN_SUB, tb)):
            for c0 in range(0, tb, C):
                crow = slice(r0 + t0 + c0, r0 + t0 + c0 + C)
                lrow = slice(t0 + c0, t0 + c0 + C)
                b = v["b_blks"][i][c0:c0 + C]
                b_last = b[C - 1:C]
                qe_ref[crow, :] = (v["gq"][lrow] * jnp.exp(b)).astype(BF16)
                ke_ref[crow, :] = (v["gk"][lrow] * jnp.exp(-b)).astype(BF16)
                kd_ref[crow, :] = (v["gk"][lrow] * jnp.exp(b_last - b)).astype(BF16)
                ci = (r0 + t0 + c0) // C
                dec_ref[ci:ci + 1, :] = jnp.exp(b_last)
        for p in range(SWA_QW // LANES):
            q = v["q_all"][:, p * LANES:(p + 1) * LANES]
            sq_ref[rows, p * LANES:(p + 1) * LANES] = (
                q * _half_rms_inv(q) * qn * (SWA_HD ** -0.5 * LOG2E)).astype(BF16)
        k = v["kv"][:, :SWA_KW]
        sk_ref[rows, :] = _dup_halves(k * _half_rms_inv(k) * kn_ref[...]).astype(BF16)
        sv_ref[rows, :] = _dup_halves(v["kv"][:, SWA_KW:]).astype(BF16)

    starts = list(range(0, tm, IN_SUB))
    hs = [norm_stage(r0) for r0 in starts]
    vals = matmul_stage(starts[0], hs[0])
    for i, r0 in enumerate(starts):
        nxt = matmul_stage(starts[i + 1], hs[i + 1]) if i + 1 < len(starts) else None
        tail_stage(r0, vals)
        vals = nxt


def _in_proj(x2, g, w_in, wa_pad, ba, qn2, kn2, tri, layer):
    n = x2.shape[0]
    tm = min(IN_TM, n)
    row = lambda w: pl.BlockSpec((tm, w), lambda i: (i, 0))
    lay = lambda a: pl.BlockSpec((None,) + a.shape[1:], lambda i: (layer,) + (0,) * (a.ndim - 1),
                                 pipeline_mode=pl.Buffered(1))
    outs = [
        (GLA_QK, BF16), (GLA_QK, BF16), (GLA_QK, BF16), None,
        (GLA_VW, BF16), (GLA_VW, BF16), (SWA_QW, BF16), (2 * SWA_KW, BF16), (2 * SWA_KW, BF16),
    ]
    dec_spec = pl.BlockSpec((tm // GLA_CHUNK, GLA_QK), lambda i: (i, 0))
    dec_shape = jax.ShapeDtypeStruct((n // GLA_CHUNK, GLA_QK), F32)
    return pl.pallas_call(
        functools.partial(_inproj_kernel, layer=layer),
        grid=(n // tm,),
        in_specs=[row(D_MODEL), lay(g), pl.BlockSpec(memory_space=pl.ANY), lay(wa_pad), lay(ba),
                  lay(qn2), lay(kn2), pl.BlockSpec(tri.shape, lambda i: (0, 0))],
        out_specs=[dec_spec if o is None else row(o[0]) for o in outs],
        out_shape=[dec_shape if o is None else jax.ShapeDtypeStruct((n, o[0]), o[1]) for o in outs],
        scratch_shapes=[
            pltpu.VMEM((D_MODEL, P_PACK), BF16),
            pltpu.VMEM((2, W_STAGE_ROWS, P_IN), F32),
            pltpu.SemaphoreType.DMA((2,)),
        ],
        compiler_params=pltpu.CompilerParams(
            dimension_semantics=("arbitrary",), vmem_limit_bytes=VMEM_LIMIT),
        name="in_proj",
    )(x2, g, w_in, wa_pad, ba, qn2, kn2, tri)


def _gla_kernel(qe_ref, ke_ref, kd_ref, dec_ref, v_ref, sr_ref, gn_ref, o_ref, st_ref, sall_ref):
    tl = v_ref.shape[0]
    C = GLA_CHUNK
    nc = tl // C
    npair = GLA_HEADS // 2
    units = [(c, p) for c in range(nc) for p in range(npair)]
    rows_of = lambda c: slice(c * C, (c + 1) * C)
    lanes_of = lambda p: slice(p * LANES, (p + 1) * LANES)
    vcol_of = lambda hd: slice(hd * GLA_DV, (hd + 1) * GLA_DV)
    nt = (((1,), (1,)), ((), ()))
    tn = (((0,), (0,)), ((), ()))

    @pl.when(pl.program_id(1) == 0)
    def _():
        st_ref[...] = jnp.zeros_like(st_ref)

    ri = lax.broadcasted_iota(jnp.int32, (2 * C, 2 * C), 0)
    ci = lax.broadcasted_iota(jnp.int32, (2 * C, 2 * C), 1)
    blockdiag_causal = (ri // C == ci // C) & (ci % C <= ri % C)
    gn = gn_ref[...]
    lo = _lane_lo((C, LANES))

    def split_pair(ref, c, p):
        x = ref[rows_of(c), lanes_of(p)]
        zero = jnp.zeros_like(x)
        return jnp.concatenate([jnp.where(lo, x, zero), jnp.where(lo, zero, x)], axis=0)

    def v_pair(c, p):
        return jnp.concatenate([v_ref[rows_of(c), vcol_of(2 * p)],
                                v_ref[rows_of(c), vcol_of(2 * p + 1)]], axis=0)

    upd = {}
    for c, p in units:
        upd[c, p] = lax.dot_general(v_pair(c, p), split_pair(kd_ref, c, p), tn,
                                    preferred_element_type=F32)
    for p in range(npair):
        st = st_ref[p]
        for c in range(nc):
            sall_ref[c, p] = st.astype(BF16)
            st = st * dec_ref[c:c + 1, lanes_of(p)] + upd.pop((c, p))
        st_ref[p] = st

    sc, out = {}, {}

    def score_stage(c):
        for p in range(npair):
            q2 = split_pair(qe_ref, c, p)
            ke = ke_ref[rows_of(c), lanes_of(p)]
            rhs = jnp.concatenate([ke, ke, sall_ref[c, p]], axis=0)
            sc[c, p] = lax.dot_general(q2, rhs, nt, preferred_element_type=F32)

    def value_stage(c):
        for p in range(npair):
            s2 = sc.pop((c, p))
            am = jnp.where(blockdiag_causal, s2[:, :2 * C], 0.0).astype(BF16)
            out[c, p] = jnp.dot(am, v_pair(c, p), preferred_element_type=F32) + s2[:, 2 * C:]

    def norm_stage(c):
        for p in range(npair):
            o = out.pop((c, p))
            o = o * lax.rsqrt(jnp.mean(o * o, axis=-1, keepdims=True) + EPS) * gn
            for half in range(2):
                vcol = vcol_of(2 * p + half)
                o_ref[rows_of(c), vcol] = (
                    o[half * C:(half + 1) * C] * sr_ref[rows_of(c), vcol]).astype(o_ref.dtype)

    for c in range(nc + 2):
        if c < nc:
            score_stage(c)
        if 0 <= c - 1 < nc:
            value_stage(c - 1)
        if 0 <= c - 2 < nc:
            norm_stage(c - 2)


def _gla(qe, ke, kd, dec, gv, sr, gn, layer, batch, seq):
    tl = min(GLA_TL, seq)
    nt = seq // tl
    row = lambda w: pl.BlockSpec((tl, w), lambda b, t: (b * nt + t, 0))
    return pl.pallas_call(
        _gla_kernel,
        grid=(batch, nt),
        in_specs=[row(GLA_QK), row(GLA_QK), row(GLA_QK),
                  pl.BlockSpec((tl // GLA_CHUNK, GLA_QK), lambda b, t: (b * nt + t, 0)),
                  row(GLA_VW), row(GLA_VW),
                  pl.BlockSpec((None, 1, GLA_DV), lambda b, t: (layer, 0, 0))],
        out_specs=row(GLA_VW),
        out_shape=jax.ShapeDtypeStruct((batch * seq, GLA_VW), BF16),
        scratch_shapes=[
            pltpu.VMEM((GLA_HEADS // 2, GLA_DV, LANES), F32),
            pltpu.VMEM((tl // GLA_CHUNK, GLA_HEADS // 2, GLA_DV, LANES), BF16),
        ],
        compiler_params=pltpu.CompilerParams(
            dimension_semantics=("parallel", "arbitrary"), vmem_limit_bytes=VMEM_LIMIT),
        name="gla",
    )(qe, ke, kd, dec, gv, sr, gn)


def _swa_kernel(sink_ref, q_ref, kc_ref, kp_ref, vc_ref, vp_ref, bias_ref, o_ref, *, layer):
    W = SWA_WINDOW
    tq = q_ref.shape[0]
    first = pl.program_id(1) == 0
    lo = _lane_lo((W, LANES))
    causal = (lax.broadcasted_iota(jnp.int32, (W, W), 1)
              <= lax.broadcasted_iota(jnp.int32, (W, W), 0))
    nt = (((1,), (1,)), ((), ()))
    pairs_per_kv = SWA_GROUP // 2
    units = [(j, kv) for j in range(tq // W) for kv in range(SWA_KV_HEADS)]
    rows_of = lambda j: slice(j * W, (j + 1) * W)
    lanes_of = lambda p: slice(p * LANES, (p + 1) * LANES)

    def window(cur_ref, prev_ref, j, kv):
        prev = prev_ref[:, lanes_of(kv)] if j == 0 else cur_ref[rows_of(j - 1), lanes_of(kv)]
        return jnp.concatenate([prev, cur_ref[rows_of(j), lanes_of(kv)]], axis=0)

    scores, outs = {}, {}

    def score_stage(j, kv):
        parts = []
        for p in range(kv * pairs_per_kv, (kv + 1) * pairs_per_kv):
            qp = q_ref[rows_of(j), lanes_of(p)]
            parts.append(jnp.where(lo, qp, jnp.zeros_like(qp)))
            parts.append(jnp.where(lo, jnp.zeros_like(qp), qp))
        qs = jnp.concatenate(parts, axis=0)
        scores[j, kv] = lax.dot_general(qs, window(kc_ref, kp_ref, j, kv), nt,
                                        preferred_element_type=F32)

    def softmax_stage(j, kv):
        s_all = scores.pop((j, kv))
        pes, denoms = [], []
        for r in range(SWA_GROUP):
            head = kv * SWA_GROUP + r
            table = jnp.where(first, SWA_HEADS + head, head) if j == 0 else head
            s2 = s_all[r * W:(r + 1) * W]
            s = jnp.where(causal, s2[:, W:], s2[:, :W]) + bias_ref[table]
            sink = sink_ref[layer, head] * LOG2E
            m = jnp.maximum(jnp.max(s, axis=-1, keepdims=True), sink)
            e = jnp.exp2(s - m)
            denoms.append(jnp.sum(e, axis=-1, keepdims=True) + jnp.exp2(sink - m))
            eb = e.astype(BF16)
            zero = jnp.zeros_like(eb)
            pes.append(jnp.concatenate(
                [jnp.where(causal, zero, eb), jnp.where(causal, eb, zero)], axis=1))
        o_all = jnp.dot(jnp.concatenate(pes, axis=0), window(vc_ref, vp_ref, j, kv),
                        preferred_element_type=F32)
        for r in range(SWA_GROUP):
            outs[j, kv * SWA_GROUP + r] = o_all[r * W:(r + 1) * W] / denoms[r]

    def store_stage(j):
        for p in range(SWA_QW // LANES):
            o_ref[rows_of(j), lanes_of(p)] = jnp.where(
                lo, outs.pop((j, 2 * p)), outs.pop((j, 2 * p + 1))).astype(o_ref.dtype)

    score_stage(*units[0])
    for i, (j, kv) in enumerate(units):
        if i + 1 < len(units):
            score_stage(*units[i + 1])
        softmax_stage(j, kv)
        if kv == SWA_KV_HEADS - 1:
            store_stage(j)


def _swa(sinks, sq, sk, sv, bias, layer, batch, seq):
    W = SWA_WINDOW
    tq = min(SWA_TQ, seq)
    nq = seq // tq
    bpt = tq // W
    bps = seq // W
    cur = lambda w: pl.BlockSpec((tq, w), lambda b, i: (b * nq + i, 0))
    prev = pl.BlockSpec((W, 2 * SWA_KW), lambda b, i: (b * bps + jnp.maximum(i * bpt - 1, 0), 0))
    return pl.pallas_call(
        functools.partial(_swa_kernel, layer=layer),
        grid=(batch, nq),
        in_specs=[
            pl.BlockSpec(memory_space=pltpu.SMEM),
            cur(SWA_QW), cur(2 * SWA_KW), prev, cur(2 * SWA_KW), prev,
            pl.BlockSpec(bias.shape, lambda b, i: (0, 0, 0)),
        ],
        out_specs=cur(SWA_QW),
        out_shape=jax.ShapeDtypeStruct((batch * seq, SWA_QW), BF16),
        compiler_params=pltpu.CompilerParams(
            dimension_semantics=("parallel", "parallel"), vmem_limit_bytes=VMEM_LIMIT),
        name="swa",
    )(sinks, sq, sk, sk, sv, sv, bias)


def _swa_bias_tables():
    W = SWA_WINDOW
    i = np.arange(W)[:, None]
    j = np.arange(W)[None, :]
    dist = np.where(j <= i, i - j, W + i - j).astype(np.float32)
    slopes = _alibi_slopes(SWA_HEADS)
    base = (-slopes[:, None, None] * dist[None] * LOG2E).astype(np.float32)
    first = np.where((j <= i)[None], base, -np.inf).astype(np.float32)
    return np.concatenate([base, first], axis=0)


def _load_bf16(w_hbm, w_ref, stage_ref, sem_ref):
    rows = stage_ref.shape[1]
    n_chunks = w_ref.shape[0] // rows

    def chunk_copy(i, slot):
        return pltpu.make_async_copy(
            w_hbm.at[pl.ds(pl.multiple_of(i * rows, rows), rows)], stage_ref.at[slot], sem_ref.at[slot])

    chunk_copy(0, 0).start()

    def body(i, carry):
        slot = i % 2

        @pl.when(i + 1 < n_chunks)
        def _():
            chunk_copy(i + 1, 1 - slot).start()

        chunk_copy(i, slot).wait()
        w_ref[pl.ds(pl.multiple_of(i * rows, rows), rows), :] = stage_ref[slot].astype(BF16)
        return carry

    lax.fori_loop(0, n_chunks, body, 0)


def _ffn_kernel(x_ref, og_ref, os_ref, wo_hbm, g_ref, wu_hbm, cw_ref, wd_hbm, y_ref,
                x1_ref, h_ref, t3_ref, ubuf_ref, gbuf_ref, halo_ref,
                wo_ref, wu_ref, wd_ref, wide_stage_ref, stage_ref, sem_ref, *, layer):
    tm = x_ref.shape[0]
    S = SUBLANES
    nv = tm // S
    nb = nv // S
    ng = D_MODEL // LANES
    first = pl.program_id(1) == 0

    @pl.when((pl.program_id(0) == 0) & first)
    def _():
        _load_bf16(wo_hbm.at[layer], wo_ref, stage_ref, sem_ref)
        _load_bf16(wu_hbm.at[layer], wu_ref, wide_stage_ref, sem_ref)
        _load_bf16(wd_hbm.at[layer], wd_ref, stage_ref, sem_ref)

    x1 = (x_ref[...]
          + jnp.dot(og_ref[...], wo_ref[:GLA_VW, :], preferred_element_type=F32)
          + jnp.dot(os_ref[...], wo_ref[GLA_VW:, :], preferred_element_type=F32))
    x1_ref[...] = x1
    ms = jnp.mean(x1 * x1, axis=-1, keepdims=True)
    hn = x1 * lax.rsqrt(ms + EPS) * g_ref[...]
    for a in range(S):
        for b in range(nb):
            src = slice(S * (nb * a + b), S * (nb * a + b) + S)
            dst = slice(S * (S * b + a), S * (S * b + a) + S)
            for g in range(ng):
                t3_ref[g, dst, :] = hn[src, g * LANES:(g + 1) * LANES]
    for k in range(nv // 2):
        rows = []
        for v in (2 * k, 2 * k + 1):
            b, c = v // S, v % S
            rows.append(jnp.concatenate(
                [t3_ref[g, pl.ds(S * S * b + c, S, stride=S), :] for g in range(ng)], axis=1))
        h_ref[2 * S * k:2 * S * (k + 1), :] = jnp.concatenate(rows, axis=0).astype(BF16)

    sub = lax.broadcasted_iota(jnp.int32, (S, 2 * FFN_FC), 0)

    def up_stage(c):
        slot = c % 2
        h = h_ref[...]
        u = jnp.concatenate([
            jnp.dot(h, wu_ref[:, c * FFN_FC:(c + 1) * FFN_FC], preferred_element_type=F32),
            jnp.dot(h, wu_ref[:, D_FF + c * FFN_FC:D_FF + (c + 1) * FFN_FC],
                    preferred_element_type=F32)], axis=1)
        halo = jnp.where(first, 0.0, halo_ref[c])
        halo_ref[c] = u[tm - 2 * S:, :]
        fix2 = jnp.where(sub == 0, pltpu.roll(halo[:S], 1, axis=0),
                         pltpu.roll(u[tm - 2 * S:tm - S], 1, axis=0))
        fix1 = jnp.where(sub == 0, pltpu.roll(halo[S:], 1, axis=0),
                         pltpu.roll(u[tm - S:], 1, axis=0))
        ubuf_ref[slot, 0:S, :] = fix2
        ubuf_ref[slot, S:2 * S, :] = fix1
        ubuf_ref[slot, 2 * S:2 * S + tm, :] = u

    def gate_stage(c):
        slot = c % 2
        ca = slice(c * FFN_FC, (c + 1) * FFN_FC)
        cb = slice(D_FF + c * FFN_FC, D_FF + (c + 1) * FFN_FC)
        cw = jnp.concatenate([cw_ref[:, ca], cw_ref[:, cb]], axis=1)
        for r in range(0, tm, FFN_RB):
            u0 = ubuf_ref[slot, 2 * S + r:2 * S + r + FFN_RB, :]
            u1 = ubuf_ref[slot, S + r:S + r + FFN_RB, :]
            u2 = ubuf_ref[slot, r:r + FFN_RB, :]
            y = u2 * cw[0:1] + u1 * cw[1:2] + u0 * cw[2:3] + cw[3:4]
            a = y[:, :FFN_FC]
            gate = (a / (1.0 + jnp.exp(-a))) * y[:, FFN_FC:]
            gbuf_ref[r:r + FFN_RB, ca] = gate.astype(BF16)

    up_stage(0)
    for c in range(FFN_NC):
        if c + 1 < FFN_NC:
            up_stage(c + 1)
        gate_stage(c)
    d = jnp.dot(gbuf_ref[...], wd_ref[...], preferred_element_type=F32)
    for g in range(ng):
        t3_ref[g] = d[:, g * LANES:(g + 1) * LANES]

    for a in range(S):
        for b in range(nb):
            rows = slice(S * (nb * a + b), S * (nb * a + b) + S)
            ffn = jnp.concatenate(
                [t3_ref[g, pl.ds(S * S * b + a, S, stride=S), :] for g in range(ng)], axis=1)
            y_ref[rows, :] = x1_ref[rows, :] + ffn


def _out_ffn(x2, o_gla, o_swa, wo, g, wu, cw, wd, layer, batch, seq):
    tm = min(FFN_TM, seq)
    assert tm % (SUBLANES * SUBLANES) == 0
    nt = seq // tm
    row = lambda w: pl.BlockSpec((tm, w), lambda b, t: (b * nt + t, 0))
    lay = lambda a: pl.BlockSpec((None,) + a.shape[1:], lambda b, t: (layer,) + (0,) * (a.ndim - 1),
                                 pipeline_mode=pl.Buffered(1))
    hbm = pl.BlockSpec(memory_space=pl.ANY)
    return pl.pallas_call(
        functools.partial(_ffn_kernel, layer=layer),
        grid=(batch, nt),
        in_specs=[row(D_MODEL), row(GLA_VW), row(SWA_QW), hbm, lay(g), hbm, lay(cw), hbm],
        out_specs=row(D_MODEL),
        out_shape=jax.ShapeDtypeStruct((batch * seq, D_MODEL), F32),
        scratch_shapes=[
            pltpu.VMEM((tm, D_MODEL), F32),
            pltpu.VMEM((tm, D_MODEL), BF16),
            pltpu.VMEM((D_MODEL // LANES, tm, LANES), F32),
            pltpu.VMEM((2, tm + 2 * SUBLANES, 2 * FFN_FC), F32),
            pltpu.VMEM((tm, D_FF), BF16),
            pltpu.VMEM((FFN_NC, 2 * SUBLANES, 2 * FFN_FC), F32),
            pltpu.VMEM((D_MODEL, D_MODEL), BF16),
            pltpu.VMEM((D_MODEL, 2 * D_FF), BF16),
            pltpu.VMEM((D_FF, D_MODEL), BF16),
            pltpu.VMEM((2, W_STAGE_ROWS // 2, 2 * D_FF), F32),
            pltpu.VMEM((2, 2 * W_STAGE_ROWS, D_MODEL), F32),
            pltpu.SemaphoreType.DMA((2,)),
        ],
        compiler_params=pltpu.CompilerParams(
            dimension_semantics=("arbitrary", "arbitrary"), vmem_limit_bytes=VMEM_LIMIT),
        name="out_ffn",
    )(x2, o_gla, o_swa, wo, g, wu, cw, wd)


def kernel(x, mix_norm, w_in, w_alpha2, b_alpha, gla_norm, q_norm, k_norm, sinks, w_out, ffn_norm,
           w_up, conv_w, conv_b, w_down):
    batch, seq, d = x.shape
    depth = w_in.shape[0]
    assert d == D_MODEL and w_in.shape[2] == P_IN
    assert all(seq % min(t, seq) == 0 for t in (GLA_TL, SWA_TQ, FFN_TM))
    assert seq % IN_SUB == 0 and (batch * seq) % min(IN_TM, batch * seq) == 0

    wa_pad = jnp.concatenate(
        [w_alpha2, jnp.zeros((depth, LANES - GLA_LOWRANK, GLA_QK), w_alpha2.dtype)],
        axis=1).astype(BF16)
    cw = jnp.concatenate([
        conv_w, conv_b[:, None, :],
        jnp.zeros((depth, SUBLANES - CONV_K - 1, 2 * D_FF), conv_w.dtype)], axis=1)
    row3 = lambda a: a.reshape(depth, 1, a.shape[-1])
    qn2 = row3(jnp.tile(q_norm, (1, 2)))
    kn2 = row3(jnp.tile(k_norm, (1, 2)))

    pos = np.arange(GLA_TRI)
    tri = jnp.asarray(
        (pos[:, None] >= pos[None, :]) & (pos[:, None] // GLA_CHUNK == pos[None, :] // GLA_CHUNK),
        dtype=BF16)
    bias = jnp.asarray(_swa_bias_tables())

    x2 = x.reshape(batch * seq, D_MODEL)
    for l in range(depth):
        qe, ke, kd, dec, gv, sr, sq, sk, sv = _in_proj(
            x2, row3(mix_norm), w_in, wa_pad, row3(b_alpha), qn2, kn2, tri, l)
        o_gla = _gla(qe, ke, kd, dec, gv, sr, row3(gla_norm), l, batch, seq)
        o_swa = _swa(sinks, sq, sk, sv, bias, l, batch, seq)
        x2 = _out_ffn(x2, o_gla, o_swa, w_out, row3(ffn_norm), w_up, cw, w_down, l, batch, seq)
    return x2.reshape(batch, seq, D_MODEL)
```

```python
import functools

import numpy as np
import jax
import jax.numpy as jnp
from jax import lax
from jax.experimental import pallas as pl
from jax.experimental.pallas import tpu as pltpu

F32 = jnp.float32
BF16 = jnp.bfloat16

D_MODEL = 1024
GLA_HEADS = 4
GLA_DV = 128
GLA_DK = 64
GLA_LOWRANK = 16
GLA_TAU = 16.0
GLA_CHUNK = 64
SWA_HEADS = 8
SWA_KV_HEADS = 2
SWA_HD = 64
SWA_WINDOW = 128
D_FF = 2816
CONV_K = 3
EPS = 1e-6
LOG2E = 1.4426950408889634

GLA_QK = GLA_HEADS * GLA_DK
GLA_VW = GLA_HEADS * GLA_DV
SWA_QW = SWA_HEADS * SWA_HD
SWA_KW = SWA_KV_HEADS * SWA_HD
SWA_GROUP = SWA_HEADS // SWA_KV_HEADS

LANES = 128
SUBLANES = 8
VMEM_LIMIT = 56 * 1024 * 1024

_R_LR = 2 * GLA_QK + 2 * GLA_VW
_R_SQ = _R_LR + GLA_LOWRANK
P_IN = _R_SQ + SWA_QW + 2 * SWA_KW
_C_GQ = 0
_C_GK = _C_GQ + GLA_QK
_C_GV = _C_GK + GLA_QK
_C_GR = _C_GV + GLA_VW
_C_SQ = _C_GR + GLA_VW
_C_SK = _C_SQ + SWA_QW
_C_SV = _C_SK + SWA_KW
_C_LR = _C_SV + SWA_KW
P_PACK = _C_LR + LANES

IN_TM = 2048
W_STAGE_ROWS = 128
IN_SUB = 512
GLA_TRI = 128
GLA_TL = 4096
SWA_TQ = 4096
FFN_TM = 512
FFN_FC = 256
FFN_NC = D_FF // FFN_FC
FFN_RB = 64


def _alibi_slopes(n_heads):
    return np.array([2.0 ** (-8.0 * (h + 1) / n_heads) for h in range(n_heads)], dtype=np.float32)


def _lane_lo(shape):
    return lax.broadcasted_iota(jnp.int32, shape, len(shape) - 1) % LANES < (LANES // 2)


def _half_rms_inv(v):
    lo = _lane_lo(v.shape)
    sq = v * v
    ss_lo = jnp.sum(jnp.where(lo, sq, 0.0), axis=-1, keepdims=True)
    ss_hi = jnp.sum(jnp.where(lo, 0.0, sq), axis=-1, keepdims=True)
    inv_lo = lax.rsqrt(ss_lo * (1.0 / SWA_HD) + EPS)
    inv_hi = lax.rsqrt(ss_hi * (1.0 / SWA_HD) + EPS)
    return jnp.where(lo, inv_lo, inv_hi)


def _dup_halves(v):
    lo = _lane_lo(v.shape)
    swapped = pltpu.roll(v, LANES // 2, axis=1)
    return jnp.concatenate([jnp.where(lo, v, swapped), jnp.where(lo, swapped, v)], axis=1)


def _load_packed_w_in(w_hbm, w_ref, stage_ref, sem_ref):
    n_chunks = D_MODEL // W_STAGE_ROWS

    def chunk_copy(i):
        return pltpu.make_async_copy(
            w_hbm.at[pl.ds(i * W_STAGE_ROWS, W_STAGE_ROWS)], stage_ref.at[i % 2], sem_ref.at[i % 2])

    chunk_copy(0).start()
    for i in range(n_chunks):
        if i + 1 < n_chunks:
            chunk_copy(i + 1).start()
        chunk_copy(i).wait()
        w = stage_ref[i % 2]
        rows = slice(i * W_STAGE_ROWS, (i + 1) * W_STAGE_ROWS)
        w_ref[rows, :_R_LR] = w[:, :_R_LR].astype(BF16)
        w_ref[rows, _C_SQ:_C_LR] = w[:, _R_SQ:P_IN].astype(BF16)
        tail = jnp.concatenate(
            [w[:, _R_LR:_R_SQ], jnp.zeros((W_STAGE_ROWS, LANES - GLA_LOWRANK), w.dtype)], axis=1)
        w_ref[rows, _C_LR:] = tail.astype(BF16)


def _inproj_kernel(x_ref, g_ref, w_hbm, wa_ref, ba_ref, qn_ref, kn_ref, tri_ref,
                   qe_ref, ke_ref, kd_ref, dec_ref, gv_ref, sr_ref, sq_ref, sk_ref, sv_ref,
                   w_ref, stage_ref, sem_ref, *, layer):
    tm = x_ref.shape[0]
    C = GLA_CHUNK
    tri = tri_ref[...]
    tb = tri.shape[0]
    qn = qn_ref[...]

    @pl.when(pl.program_id(0) == 0)
    def _():
        _load_packed_w_in(w_hbm.at[layer], w_ref, stage_ref, sem_ref)

    def norm_stage(r0):
        x = x_ref[r0:r0 + IN_SUB, :]
        ms = jnp.mean(x * x, axis=-1, keepdims=True)
        return (x * lax.rsqrt(ms + EPS) * g_ref[...]).astype(BF16)

    def matmul_stage(r0, h):
        rows = slice(r0, r0 + IN_SUB)
        proj = lambda c0, width: jnp.dot(h, w_ref[:, c0:c0 + width], preferred_element_type=F32)
        glr = proj(_C_LR, LANES).astype(BF16)
        gq = proj(_C_GQ, GLA_QK) * (GLA_DK ** -0.5)
        pre = jnp.dot(glr, wa_ref[...], preferred_element_type=F32) + ba_ref[...]
        la = (jnp.minimum(pre, 0.0) - jnp.log(1.0 + jnp.exp(-jnp.abs(pre)))) * (1.0 / GLA_TAU)
        la_hi = la.astype(BF16)
        la_lo = (la - la_hi.astype(F32)).astype(BF16)
        gk = proj(_C_GK, GLA_QK)
        b_blks = [jnp.dot(tri, la_hi[t0:t0 + tb], preferred_element_type=F32)
                  + jnp.dot(tri, la_lo[t0:t0 + tb], preferred_element_type=F32)
                  for t0 in range(0, IN_SUB, tb)]
        q_all = proj(_C_SQ, SWA_QW)
        kv = proj(_C_SK, 2 * SWA_KW)
        r = proj(_C_GR, GLA_VW)
        gv_ref[rows, :] = proj(_C_GV, GLA_VW).astype(BF16)
        return dict(r=r, b_blks=b_blks, gq=gq, gk=gk, q_all=q_all, kv=kv)

    def tail_stage(r0, v):
        rows = slice(r0, r0 + IN_SUB)
        r = v["r"]
        sr_ref[rows, :] = (r / (1.0 + jnp.exp(-r))).astype(sr_ref.dtype)
        for i, t0 in enumerate(range(0, IN_SUB, tb)):
            for c0 in range(0, tb, C):
                crow = slice(r0 + t0 + c0, r0 + t0 + c0 + C)
                lrow = slice(t0 + c0, t0 + c0 + C)
                b = v["b_blks"][i][c0:c0 + C]
                b_last = b[C - 1:C]
                qe_ref[crow, :] = (v["gq"][lrow] * jnp.exp(b)).astype(BF16)
                ke_ref[crow, :] = (v["gk"][lrow] * jnp.exp(-b)).astype(BF16)
                kd_ref[crow, :] = (v["gk"][lrow] * jnp.exp(b_last - b)).astype(BF16)
                ci = (r0 + t0 + c0) // C
                dec_ref[ci:ci + 1, :] = jnp.exp(b_last)
        for p in range(SWA_QW // LANES):
            q = v["q_all"][:, p * LANES:(p + 1) * LANES]
            sq_ref[rows, p * LANES:(p + 1) * LANES] = (
                q * _half_rms_inv(q) * qn * (SWA_HD ** -0.5 * LOG2E)).astype(BF16)
        k = v["kv"][:, :SWA_KW]
        sk_ref[rows, :] = _dup_halves(k * _half_rms_inv(k) * kn_ref[...]).astype(BF16)
        sv_ref[rows, :] = _dup_halves(v["kv"][:, SWA_KW:]).astype(BF16)

    starts = list(range(0, tm, IN_SUB))
    hs = [norm_stage(r0) for r0 in starts]
    vals = matmul_stage(starts[0], hs[0])
    for i, r0 in enumerate(starts):
        nxt = matmul_stage(starts[i + 1], hs[i + 1]) if i + 1 < len(starts) else None
        tail_stage(r0, vals)
        vals = nxt


def _in_proj(x2, g, w_in, wa_pad, ba, qn2, kn2, tri, layer):
    n = x2.shape[0]
    tm = min(IN_TM, n)
    row = lambda w: pl.BlockSpec((tm, w), lambda i: (i, 0))
    lay = lambda a: pl.BlockSpec((None,) + a.shape[1:], lambda i: (layer,) + (0,) * (a.ndim - 1),
                                 pipeline_mode=pl.Buffered(1))
    outs = [
        (GLA_QK, BF16), (GLA_QK, BF16), (GLA_QK, BF16), None,
        (GLA_VW, BF16), (GLA_VW, BF16), (SWA_QW, BF16), (2 * SWA_KW, BF16), (2 * SWA_KW, BF16),
    ]
    dec_spec = pl.BlockSpec((tm // GLA_CHUNK, GLA_QK), lambda i: (i, 0))
    dec_shape = jax.ShapeDtypeStruct((n // GLA_CHUNK, GLA_QK), F32)
    return pl.pallas_call(
        functools.partial(_inproj_kernel, layer=layer),
        grid=(n // tm,),
        in_specs=[row(D_MODEL), lay(g), pl.BlockSpec(memory_space=pl.ANY), lay(wa_pad), lay(ba),
                  lay(qn2), lay(kn2), pl.BlockSpec(tri.shape, lambda i: (0, 0))],
        out_specs=[dec_spec if o is None else row(o[0]) for o in outs],
        out_shape=[dec_shape if o is None else jax.ShapeDtypeStruct((n, o[0]), o[1]) for o in outs],
        scratch_shapes=[
            pltpu.VMEM((D_MODEL, P_PACK), BF16),
            pltpu.VMEM((2, W_STAGE_ROWS, P_PACK), F32),
            pltpu.SemaphoreType.DMA((2,)),
        ],
        compiler_params=pltpu.CompilerParams(
            dimension_semantics=("arbitrary",), vmem_limit_bytes=VMEM_LIMIT),
        name="in_proj",
    )(x2, g, w_in, wa_pad, ba, qn2, kn2, tri)


def _gla_kernel(qe_ref, ke_ref, kd_ref, dec_ref, v_ref, sr_ref, gn_ref, o_ref, st_ref, sall_ref):
    tl = v_ref.shape[0]
    C = GLA_CHUNK
    nc = tl // C
    npair = GLA_HEADS // 2
    units = [(c, p) for c in range(nc) for p in range(npair)]
    rows_of = lambda c: slice(c * C, (c + 1) * C)
    lanes_of = lambda p: slice(p * LANES, (p + 1) * LANES)
    vcol_of = lambda hd: slice(hd * GLA_DV, (hd + 1) * GLA_DV)
    nt = (((1,), (1,)), ((), ()))
    tn = (((0,), (0,)), ((), ()))

    @pl.when(pl.program_id(1) == 0)
    def _():
        st_ref[...] = jnp.zeros_like(st_ref)

    ri = lax.broadcasted_iota(jnp.int32, (2 * C, 2 * C), 0)
    ci = lax.broadcasted_iota(jnp.int32, (2 * C, 2 * C), 1)
    blockdiag_causal = (ri // C == ci // C) & (ci % C <= ri % C)
    gn = gn_ref[...]
    lo = _lane_lo((C, LANES))

    def split_pair(ref, c, p):
        x = ref[rows_of(c), lanes_of(p)]
        zero = jnp.zeros_like(x)
        return jnp.concatenate([jnp.where(lo, x, zero), jnp.where(lo, zero, x)], axis=0)

    def v_pair(c, p):
        return jnp.concatenate([v_ref[rows_of(c), vcol_of(2 * p)],
                                v_ref[rows_of(c), vcol_of(2 * p + 1)]], axis=0)

    upd = {}
    for c, p in units:
        upd[c, p] = lax.dot_general(v_pair(c, p), split_pair(kd_ref, c, p), tn,
                                    preferred_element_type=F32)
    for p in range(npair):
        st = st_ref[p]
        for c in range(nc):
            sall_ref[c, p] = st.astype(BF16)
            st = st * dec_ref[c:c + 1, lanes_of(p)] + upd.pop((c, p))
        st_ref[p] = st

    sc, out = {}, {}

    def score_stage(c):
        for p in range(npair):
            q2 = split_pair(qe_ref, c, p)
            ke = ke_ref[rows_of(c), lanes_of(p)]
            rhs = jnp.concatenate([ke, ke, sall_ref[c, p]], axis=0)
            sc[c, p] = lax.dot_general(q2, rhs, nt, preferred_element_type=F32)

    def value_stage(c):
        for p in range(npair):
            s2 = sc.pop((c, p))
            am = jnp.where(blockdiag_causal, s2[:, :2 * C], 0.0).astype(BF16)
            out[c, p] = jnp.dot(am, v_pair(c, p), preferred_element_type=F32) + s2[:, 2 * C:]

    def norm_stage(c):
        for p in range(npair):
            o = out.pop((c, p))
            o = o * lax.rsqrt(jnp.mean(o * o, axis=-1, keepdims=True) + EPS) * gn
            for half in range(2):
                vcol = vcol_of(2 * p + half)
                o_ref[rows_of(c), vcol] = (
                    o[half * C:(half + 1) * C] * sr_ref[rows_of(c), vcol]).astype(o_ref.dtype)

    for c in range(nc + 2):
        if c < nc:
            score_stage(c)
        if 0 <= c - 1 < nc:
            value_stage(c - 1)
        if 0 <= c - 2 < nc:
            norm_stage(c - 2)


def _gla(qe, ke, kd, dec, gv, sr, gn, layer, batch, seq):
    tl = min(GLA_TL, seq)
    nt = seq // tl
    row = lambda w: pl.BlockSpec((tl, w), lambda b, t: (b * nt + t, 0))
    return pl.pallas_call(
        _gla_kernel,
        grid=(batch, nt),
        in_specs=[row(GLA_QK), row(GLA_QK), row(GLA_QK),
                  pl.BlockSpec((tl // GLA_CHUNK, GLA_QK), lambda b, t: (b * nt + t, 0)),
                  row(GLA_VW), row(GLA_VW),
                  pl.BlockSpec((None, 1, GLA_DV), lambda b, t: (layer, 0, 0))],
        out_specs=row(GLA_VW),
        out_shape=jax.ShapeDtypeStruct((batch * seq, GLA_VW), BF16),
        scratch_shapes=[
            pltpu.VMEM((GLA_HEADS // 2, GLA_DV, LANES), F32),
            pltpu.VMEM((tl // GLA_CHUNK, GLA_HEADS // 2, GLA_DV, LANES), BF16),
        ],
        compiler_params=pltpu.CompilerParams(
            dimension_semantics=("parallel", "arbitrary"), vmem_limit_bytes=VMEM_LIMIT),
        name="gla",
    )(qe, ke, kd, dec, gv, sr, gn)


def _swa_kernel(sink_ref, q_ref, kc_ref, kp_ref, vc_ref, vp_ref, bias_ref, o_ref, *, layer):
    W = SWA_WINDOW
    tq = q_ref.shape[0]
    first = pl.program_id(1) == 0
    lo = _lane_lo((W, LANES))
    causal = (lax.broadcasted_iota(jnp.int32, (W, W), 1)
              <= lax.broadcasted_iota(jnp.int32, (W, W), 0))
    nt = (((1,), (1,)), ((), ()))
    pairs_per_kv = SWA_GROUP // 2
    units = [(j, kv) for j in range(tq // W) for kv in range(SWA_KV_HEADS)]
    rows_of = lambda j: slice(j * W, (j + 1) * W)
    lanes_of = lambda p: slice(p * LANES, (p + 1) * LANES)

    def window(cur_ref, prev_ref, j, kv):
        prev = prev_ref[:, lanes_of(kv)] if j == 0 else cur_ref[rows_of(j - 1), lanes_of(kv)]
        return jnp.concatenate([prev, cur_ref[rows_of(j), lanes_of(kv)]], axis=0)

    scores, outs = {}, {}

    def score_stage(j, kv):
        parts = []
        for p in range(kv * pairs_per_kv, (kv + 1) * pairs_per_kv):
            qp = q_ref[rows_of(j), lanes_of(p)]
            parts.append(jnp.where(lo, qp, jnp.zeros_like(qp)))
            parts.append(jnp.where(lo, jnp.zeros_like(qp), qp))
        qs = jnp.concatenate(parts, axis=0)
        scores[j, kv] = lax.dot_general(qs, window(kc_ref, kp_ref, j, kv), nt,
                                        preferred_element_type=F32)

    def softmax_stage(j, kv):
        s_all = scores.pop((j, kv))
        pes, denoms = [], []
        for r in range(SWA_GROUP):
            head = kv * SWA_GROUP + r
            table = jnp.where(first, SWA_HEADS + head, head) if j == 0 else head
            s2 = s_all[r * W:(r + 1) * W]
            s = jnp.where(causal, s2[:, W:], s2[:, :W]) + bias_ref[table]
            sink = sink_ref[layer, head] * LOG2E
            m = jnp.maximum(jnp.max(s, axis=-1, keepdims=True), sink)
            e = jnp.exp2(s - m)
            denoms.append(jnp.sum(e, axis=-1, keepdims=True) + jnp.exp2(sink - m))
            eb = e.astype(BF16)
            zero = jnp.zeros_like(eb)
            pes.append(jnp.concatenate(
                [jnp.where(causal, zero, eb), jnp.where(causal, eb, zero)], axis=1))
        o_all = jnp.dot(jnp.concatenate(pes, axis=0), window(vc_ref, vp_ref, j, kv),
                        preferred_element_type=F32)
        for r in range(SWA_GROUP):
            outs[j, kv * SWA_GROUP + r] = o_all[r * W:(r + 1) * W] / denoms[r]

    def store_stage(j):
        for p in range(SWA_QW // LANES):
            o_ref[rows_of(j), lanes_of(p)] = jnp.where(
                lo, outs.pop((j, 2 * p)), outs.pop((j, 2 * p + 1))).astype(o_ref.dtype)

    score_stage(*units[0])
    for i, (j, kv) in enumerate(units):
        if i + 1 < len(units):
            score_stage(*units[i + 1])
        softmax_stage(j, kv)
        if kv == SWA_KV_HEADS - 1:
            store_stage(j)


def _swa(sinks, sq, sk, sv, bias, layer, batch, seq):
    W = SWA_WINDOW
    tq = min(SWA_TQ, seq)
    nq = seq // tq
    bpt = tq // W
    bps = seq // W
    cur = lambda w: pl.BlockSpec((tq, w), lambda b, i: (b * nq + i, 0))
    prev = pl.BlockSpec((W, 2 * SWA_KW), lambda b, i: (b * bps + jnp.maximum(i * bpt - 1, 0), 0))
    return pl.pallas_call(
        functools.partial(_swa_kernel, layer=layer),
        grid=(batch, nq),
        in_specs=[
            pl.BlockSpec(memory_space=pltpu.SMEM),
            cur(SWA_QW), cur(2 * SWA_KW), prev, cur(2 * SWA_KW), prev,
            pl.BlockSpec(bias.shape, lambda b, i: (0, 0, 0)),
        ],
        out_specs=cur(SWA_QW),
        out_shape=jax.ShapeDtypeStruct((batch * seq, SWA_QW), BF16),
        compiler_params=pltpu.CompilerParams(
            dimension_semantics=("parallel", "parallel"), vmem_limit_bytes=VMEM_LIMIT),
        name="swa",
    )(sinks, sq, sk, sk, sv, sv, bias)


def _swa_bias_tables():
    W = SWA_WINDOW
    i = np.arange(W)[:, None]
    j = np.arange(W)[None, :]
    dist = np.where(j <= i, i - j, W + i - j).astype(np.float32)
    slopes = _alibi_slopes(SWA_HEADS)
    base = (-slopes[:, None, None] * dist[None] * LOG2E).astype(np.float32)
    first = np.where((j <= i)[None], base, -np.inf).astype(np.float32)
    return np.concatenate([base, first], axis=0)


def _load_bf16(w_hbm, w_ref, stage_ref, sem_ref):
    rows = stage_ref.shape[1]
    n_chunks = w_ref.shape[0] // rows

    def chunk_copy(i, slot):
        return pltpu.make_async_copy(
            w_hbm.at[pl.ds(pl.multiple_of(i * rows, rows), rows)], stage_ref.at[slot], sem_ref.at[slot])

    chunk_copy(0, 0).start()

    def body(i, carry):
        slot = i % 2

        @pl.when(i + 1 < n_chunks)
        def _():
            chunk_copy(i + 1, 1 - slot).start()

        chunk_copy(i, slot).wait()
        w_ref[pl.ds(pl.multiple_of(i * rows, rows), rows), :] = stage_ref[slot].astype(BF16)
        return carry

    lax.fori_loop(0, n_chunks, body, 0)


def _ffn_kernel(x_ref, og_ref, os_ref, wo_hbm, g_ref, wu_hbm, cw_ref, wd_hbm, y_ref,
                x1_ref, h_ref, t3_ref, ubuf_ref, gbuf_ref, halo_ref,
                wo_ref, wu_ref, wd_ref, wide_stage_ref, stage_ref, sem_ref, *, layer):
    tm = x_ref.shape[0]
    S = SUBLANES
    nv = tm // S
    nb = nv // S
    ng = D_MODEL // LANES
    first = pl.program_id(1) == 0

    @pl.when((pl.program_id(0) == 0) & first)
    def _():
        _load_bf16(wo_hbm.at[layer], wo_ref, stage_ref, sem_ref)
        _load_bf16(wu_hbm.at[layer], wu_ref, wide_stage_ref, sem_ref)
        _load_bf16(wd_hbm.at[layer], wd_ref, stage_ref, sem_ref)

    x1 = (x_ref[...]
          + jnp.dot(og_ref[...], wo_ref[:GLA_VW, :], preferred_element_type=F32)
          + jnp.dot(os_ref[...], wo_ref[GLA_VW:, :], preferred_element_type=F32))
    x1_ref[...] = x1
    ms = jnp.mean(x1 * x1, axis=-1, keepdims=True)
    hn = x1 * lax.rsqrt(ms + EPS) * g_ref[...]
    for a in range(S):
        for b in range(nb):
            src = slice(S * (nb * a + b), S * (nb * a + b) + S)
            dst = slice(S * (S * b + a), S * (S * b + a) + S)
            for g in range(ng):
                t3_ref[g, dst, :] = hn[src, g * LANES:(g + 1) * LANES]
    for k in range(nv // 2):
        rows = []
        for v in (2 * k, 2 * k + 1):
            b, c = v // S, v % S
            rows.append(jnp.concatenate(
                [t3_ref[g, pl.ds(S * S * b + c, S, stride=S), :] for g in range(ng)], axis=1))
        h_ref[2 * S * k:2 * S * (k + 1), :] = jnp.concatenate(rows, axis=0).astype(BF16)

    sub = lax.broadcasted_iota(jnp.int32, (S, 2 * FFN_FC), 0)

    def up_stage(c):
        slot = c % 2
        h = h_ref[...]
        u = jnp.concatenate([
            jnp.dot(h, wu_ref[:, c * FFN_FC:(c + 1) * FFN_FC], preferred_element_type=F32),
            jnp.dot(h, wu_ref[:, D_FF + c * FFN_FC:D_FF + (c + 1) * FFN_FC],
                    preferred_element_type=F32)], axis=1)
        halo = jnp.where(first, 0.0, halo_ref[c])
        halo_ref[c] = u[tm - 2 * S:, :]
        fix2 = jnp.where(sub == 0, pltpu.roll(halo[:S], 1, axis=0),
                         pltpu.roll(u[tm - 2 * S:tm - S], 1, axis=0))
        fix1 = jnp.where(sub == 0, pltpu.roll(halo[S:], 1, axis=0),
                         pltpu.roll(u[tm - S:], 1, axis=0))
        ubuf_ref[slot, 0:S, :] = fix2
        ubuf_ref[slot, S:2 * S, :] = fix1
        ubuf_ref[slot, 2 * S:2 * S + tm, :] = u

    def gate_stage(c):
        slot = c % 2
        ca = slice(c * FFN_FC, (c + 1) * FFN_FC)
        cb = slice(D_FF + c * FFN_FC, D_FF + (c + 1) * FFN_FC)
        cw = jnp.concatenate([cw_ref[:, ca], cw_ref[:, cb]], axis=1)
        for r in range(0, tm, FFN_RB):
            u0 = ubuf_ref[slot, 2 * S + r:2 * S + r + FFN_RB, :]
            u1 = ubuf_ref[slot, S + r:S + r + FFN_RB, :]
            u2 = ubuf_ref[slot, r:r + FFN_RB, :]
            y = u2 * cw[0:1] + u1 * cw[1:2] + u0 * cw[2:3] + cw[3:4]
            a = y[:, :FFN_FC]
            gate = (a / (1.0 + jnp.exp(-a))) * y[:, FFN_FC:]
            gbuf_ref[r:r + FFN_RB, ca] = gate.astype(BF16)

    up_stage(0)
    for c in range(FFN_NC):
        if c + 1 < FFN_NC:
            up_stage(c + 1)
        gate_stage(c)
    d = jnp.dot(gbuf_ref[...], wd_ref[...], preferred_element_type=F32)
    for g in range(ng):
        t3_ref[g] = d[:, g * LANES:(g + 1) * LANES]

    for a in range(S):
        for b in range(nb):
            rows = slice(S * (nb * a + b), S * (nb * a + b) + S)
            ffn = jnp.concatenate(
                [t3_ref[g, pl.ds(S * S * b + a, S, stride=S), :] for g in range(ng)], axis=1)
            y_ref[rows, :] = x1_ref[rows, :] + ffn


def _out_ffn(x2, o_gla, o_swa, wo, g, wu, cw, wd, layer, batch, seq):
    tm = min(FFN_TM, seq)
    assert tm % (SUBLANES * SUBLANES) == 0
    nt = seq // tm
    row = lambda w: pl.BlockSpec((tm, w), lambda b, t: (b * nt + t, 0))
    lay = lambda a: pl.BlockSpec((None,) + a.shape[1:], lambda b, t: (layer,) + (0,) * (a.ndim - 1),
                                 pipeline_mode=pl.Buffered(1))
    hbm = pl.BlockSpec(memory_space=pl.ANY)
    return pl.pallas_call(
        functools.partial(_ffn_kernel, layer=layer),
        grid=(batch, nt),
        in_specs=[row(D_MODEL), row(GLA_VW), row(SWA_QW), hbm, lay(g), hbm, lay(cw), hbm],
        out_specs=row(D_MODEL),
        out_shape=jax.ShapeDtypeStruct((batch * seq, D_MODEL), F32),
        scratch_shapes=[
            pltpu.VMEM((tm, D_MODEL), F32),
            pltpu.VMEM((tm, D_MODEL), BF16),
            pltpu.VMEM((D_MODEL // LANES, tm, LANES), F32),
            pltpu.VMEM((2, tm + 2 * SUBLANES, 2 * FFN_FC), F32),
            pltpu.VMEM((tm, D_FF), BF16),
            pltpu.VMEM((FFN_NC, 2 * SUBLANES, 2 * FFN_FC), F32),
            pltpu.VMEM((D_MODEL, D_MODEL), BF16),
            pltpu.VMEM((D_MODEL, 2 * D_FF), BF16),
            pltpu.VMEM((D_FF, D_MODEL), BF16),
            pltpu.VMEM((2, W_STAGE_ROWS // 2, 2 * D_FF), F32),
            pltpu.VMEM((2, 2 * W_STAGE_ROWS, D_MODEL), F32),
            pltpu.SemaphoreType.DMA((2,)),
        ],
        compiler_params=pltpu.CompilerParams(
            dimension_semantics=("arbitrary", "arbitrary"), vmem_limit_bytes=VMEM_LIMIT),
        name="out_ffn",
    )(x2, o_gla, o_swa, wo, g, wu, cw, wd)


def kernel(x, mix_norm, w_in, w_alpha2, b_alpha, gla_norm, q_norm, k_norm, sinks, w_out, ffn_norm,
           w_up, conv_w, conv_b, w_down):
    batch, seq, d = x.shape
    depth = w_in.shape[0]
    assert d == D_MODEL and w_in.shape[2] == P_IN
    assert all(seq % min(t, seq) == 0 for t in (GLA_TL, SWA_TQ, FFN_TM))
    assert seq % IN_SUB == 0 and (batch * seq) % min(IN_TM, batch * seq) == 0

    w_in_p = jnp.pad(w_in, ((0, 0), (0, 0), (0, P_PACK - P_IN)))
    wa_pad = jnp.concatenate(
        [w_alpha2, jnp.zeros((depth, LANES - GLA_LOWRANK, GLA_QK), w_alpha2.dtype)],
        axis=1).astype(BF16)
    cw = jnp.concatenate([
        conv_w, conv_b[:, None, :],
        jnp.zeros((depth, SUBLANES - CONV_K - 1, 2 * D_FF), conv_w.dtype)], axis=1)
    row3 = lambda a: a.reshape(depth, 1, a.shape[-1])
    qn2 = row3(jnp.tile(q_norm, (1, 2)))
    kn2 = row3(jnp.tile(k_norm, (1, 2)))

    pos = np.arange(GLA_TRI)
    tri = jnp.asarray(
        (pos[:, None] >= pos[None, :]) & (pos[:, None] // GLA_CHUNK == pos[None, :] // GLA_CHUNK),
        dtype=BF16)
    bias = jnp.asarray(_swa_bias_tables())

    x2 = x.reshape(batch * seq, D_MODEL)
    for l in range(depth):
        qe, ke, kd, dec, gv, sr, sq, sk, sv = _in_proj(
            x2, row3(mix_norm), w_in_p, wa_pad, row3(b_alpha), qn2, kn2, tri, l)
        o_gla = _gla(qe, ke, kd, dec, gv, sr, row3(gla_norm), l, batch, seq)
        o_swa = _swa(sinks, sq, sk, sv, bias, l, batch, seq)
        x2 = _out_ffn(x2, o_gla, o_swa, w_out, row3(ffn_norm), w_up, cw, w_down, l, batch, seq)
    return x2.reshape(batch, seq, D_MODEL)
```

```python
import functools

import numpy as np
import jax
import jax.numpy as jnp
from jax import lax
from jax.experimental import pallas as pl
from jax.experimental.pallas import tpu as pltpu

F32 = jnp.float32
BF16 = jnp.bfloat16

D_MODEL = 1024
GLA_HEADS = 4
GLA_DV = 128
GLA_DK = 64
GLA_LOWRANK = 16
GLA_TAU = 16.0
GLA_CHUNK = 64
SWA_HEADS = 8
SWA_KV_HEADS = 2
SWA_HD = 64
SWA_WINDOW = 128
D_FF = 2816
CONV_K = 3
EPS = 1e-6
LOG2E = 1.4426950408889634

GLA_QK = GLA_HEADS * GLA_DK
GLA_VW = GLA_HEADS * GLA_DV
SWA_QW = SWA_HEADS * SWA_HD
SWA_KW = SWA_KV_HEADS * SWA_HD
SWA_GROUP = SWA_HEADS // SWA_KV_HEADS

LANES = 128
SUBLANES = 8
VMEM_LIMIT = 56 * 1024 * 1024

_R_LR = 2 * GLA_QK + 2 * GLA_VW
_R_SQ = _R_LR + GLA_LOWRANK
P_IN = _R_SQ + SWA_QW + 2 * SWA_KW
_C_GQ = 0
_C_GK = _C_GQ + GLA_QK
_C_GV = _C_GK + GLA_QK
_C_GR = _C_GV + GLA_VW
_C_SQ = _C_GR + GLA_VW
_C_SK = _C_SQ + SWA_QW
_C_SV = _C_SK + SWA_KW
_C_LR = _C_SV + SWA_KW
P_PACK = _C_LR + LANES

IN_TM = 2048
W_STAGE_ROWS = 128
IN_SUB = 512
GLA_TRI = 128
GLA_TL = 4096
SWA_TQ = 4096
FFN_TM = 512
FFN_FC = 256
FFN_NC = D_FF // FFN_FC
FFN_RB = 64


def _alibi_slopes(n_heads):
    return np.array([2.0 ** (-8.0 * (h + 1) / n_heads) for h in range(n_heads)], dtype=np.float32)


def _lane_lo(shape):
    return lax.broadcasted_iota(jnp.int32, shape, len(shape) - 1) % LANES < (LANES // 2)


def _half_rms_inv(v):
    lo = _lane_lo(v.shape)
    sq = v * v
    ss_lo = jnp.sum(jnp.where(lo, sq, 0.0), axis=-1, keepdims=True)
    ss_hi = jnp.sum(jnp.where(lo, 0.0, sq), axis=-1, keepdims=True)
    inv_lo = lax.rsqrt(ss_lo * (1.0 / SWA_HD) + EPS)
    inv_hi = lax.rsqrt(ss_hi * (1.0 / SWA_HD) + EPS)
    return jnp.where(lo, inv_lo, inv_hi)


def _dup_halves(v):
    lo = _lane_lo(v.shape)
    swapped = pltpu.roll(v, LANES // 2, axis=1)
    return jnp.concatenate([jnp.where(lo, v, swapped), jnp.where(lo, swapped, v)], axis=1)


def _load_packed_w_in(w_hbm, w_ref, stage_ref, sem_ref):
    n_chunks = D_MODEL // W_STAGE_ROWS

    def chunk_copy(i):
        return pltpu.make_async_copy(
            w_hbm.at[pl.ds(i * W_STAGE_ROWS, W_STAGE_ROWS)], stage_ref.at[i % 2], sem_ref.at[i % 2])

    chunk_copy(0).start()
    for i in range(n_chunks):
        if i + 1 < n_chunks:
            chunk_copy(i + 1).start()
        chunk_copy(i).wait()
        w = stage_ref[i % 2]
        rows = slice(i * W_STAGE_ROWS, (i + 1) * W_STAGE_ROWS)
        w_ref[rows, :_R_LR] = w[:, :_R_LR].astype(BF16)
        w_ref[rows, _C_SQ:_C_LR] = w[:, _R_SQ:].astype(BF16)
        tail = jnp.concatenate(
            [w[:, _R_LR:_R_SQ], jnp.zeros((W_STAGE_ROWS, LANES - GLA_LOWRANK), w.dtype)], axis=1)
        w_ref[rows, _C_LR:] = tail.astype(BF16)


def _inproj_kernel(x_ref, g_ref, w_hbm, wa_ref, ba_ref, qn_ref, kn_ref, tri_ref,
                   qe_ref, ke_ref, kd_ref, dec_ref, gv_ref, sr_ref, sq_ref, sk_ref, sv_ref,
                   w_ref, stage_ref, sem_ref, *, layer):
    tm = x_ref.shape[0]
    C = GLA_CHUNK
    tri = tri_ref[...]
    tb = tri.shape[0]
    qn = qn_ref[...]

    @pl.when(pl.program_id(0) == 0)
    def _():
        _load_packed_w_in(w_hbm.at[layer], w_ref, stage_ref, sem_ref)

    def norm_stage(r0):
        x = x_ref[r0:r0 + IN_SUB, :]
        ms = jnp.mean(x * x, axis=-1, keepdims=True)
        return (x * lax.rsqrt(ms + EPS) * g_ref[...]).astype(BF16)

    def matmul_stage(r0, h):
        rows = slice(r0, r0 + IN_SUB)
        proj = lambda c0, width: jnp.dot(h, w_ref[:, c0:c0 + width], preferred_element_type=F32)
        glr = proj(_C_LR, LANES).astype(BF16)
        gq = proj(_C_GQ, GLA_QK) * (GLA_DK ** -0.5)
        pre = jnp.dot(glr, wa_ref[...], preferred_element_type=F32) + ba_ref[...]
        la = (jnp.minimum(pre, 0.0) - jnp.log(1.0 + jnp.exp(-jnp.abs(pre)))) * (1.0 / GLA_TAU)
        la_hi = la.astype(BF16)
        la_lo = (la - la_hi.astype(F32)).astype(BF16)
        gk = proj(_C_GK, GLA_QK)
        b_blks = [jnp.dot(tri, la_hi[t0:t0 + tb], preferred_element_type=F32)
                  + jnp.dot(tri, la_lo[t0:t0 + tb], preferred_element_type=F32)
                  for t0 in range(0, IN_SUB, tb)]
        q_all = proj(_C_SQ, SWA_QW)
        kv = proj(_C_SK, 2 * SWA_KW)
        r = proj(_C_GR, GLA_VW)
        gv_ref[rows, :] = proj(_C_GV, GLA_VW).astype(BF16)
        return dict(r=r, b_blks=b_blks, gq=gq, gk=gk, q_all=q_all, kv=kv)

    def tail_stage(r0, v):
        rows = slice(r0, r0 + IN_SUB)
        r = v["r"]
        sr_ref[rows, :] = (r / (1.0 + jnp.exp(-r))).astype(sr_ref.dtype)
        for i, t0 in enumerate(range(0, IN_SUB, tb)):
            for c0 in range(0, tb, C):
                crow = slice(r0 + t0 + c0, r0 + t0 + c0 + C)
                lrow = slice(t0 + c0, t0 + c0 + C)
                b = v["b_blks"][i][c0:c0 + C]
                b_last = b[C - 1:C]
                qe_ref[crow, :] = (v["gq"][lrow] * jnp.exp(b)).astype(BF16)
                ke_ref[crow, :] = (v["gk"][lrow] * jnp.exp(-b)).astype(BF16)
                kd_ref[crow, :] = (v["gk"][lrow] * jnp.exp(b_last - b)).astype(BF16)
                ci = (r0 + t0 + c0) // C
                dec_ref[ci:ci + 1, :] = jnp.exp(b_last)
        for p in range(SWA_QW // LANES):
            q = v["q_all"][:, p * LANES:(p + 1) * LANES]
            sq_ref[rows, p * LANES:(p + 1) * LANES] = (
                q * _half_rms_inv(q) * qn * (SWA_HD ** -0.5 * LOG2E)).astype(BF16)
        k = v["kv"][:, :SWA_KW]
        sk_ref[rows, :] = _dup_halves(k * _half_rms_inv(k) * kn_ref[...]).astype(BF16)
        sv_ref[rows, :] = _dup_halves(v["kv"][:, SWA_KW:]).astype(BF16)

    starts = list(range(0, tm, IN_SUB))
    hs = [norm_stage(r0) for r0 in starts]
    vals = matmul_stage(starts[0], hs[0])
    for i, r0 in enumerate(starts):
        nxt = matmul_stage(starts[i + 1], hs[i + 1]) if i + 1 < len(starts) else None
        tail_stage(r0, vals)
        vals = nxt


def _in_proj(x2, g, w_in, wa_pad, ba, qn2, kn2, tri, layer):
    n = x2.shape[0]
    tm = min(IN_TM, n)
    row = lambda w: pl.BlockSpec((tm, w), lambda i: (i, 0))
    lay = lambda a: pl.BlockSpec((None,) + a.shape[1:], lambda i: (layer,) + (0,) * (a.ndim - 1),
                                 pipeline_mode=pl.Buffered(1))
    outs = [
        (GLA_QK, BF16), (GLA_QK, BF16), (GLA_QK, BF16), None,
        (GLA_VW, BF16), (GLA_VW, BF16), (SWA_QW, BF16), (2 * SWA_KW, BF16), (2 * SWA_KW, BF16),
    ]
    dec_spec = pl.BlockSpec((tm // GLA_CHUNK, GLA_QK), lambda i: (i, 0))
    dec_shape = jax.ShapeDtypeStruct((n // GLA_CHUNK, GLA_QK), F32)
    return pl.pallas_call(
        functools.partial(_inproj_kernel, layer=layer),
        grid=(n // tm,),
        in_specs=[row(D_MODEL), lay(g), pl.BlockSpec(memory_space=pl.ANY), lay(wa_pad), lay(ba),
                  lay(qn2), lay(kn2), pl.BlockSpec(tri.shape, lambda i: (0, 0))],
        out_specs=[dec_spec if o is None else row(o[0]) for o in outs],
        out_shape=[dec_shape if o is None else jax.ShapeDtypeStruct((n, o[0]), o[1]) for o in outs],
        scratch_shapes=[
            pltpu.VMEM((D_MODEL, P_PACK), BF16),
            pltpu.VMEM((2, W_STAGE_ROWS, P_IN), F32),
            pltpu.SemaphoreType.DMA((2,)),
        ],
        compiler_params=pltpu.CompilerParams(
            dimension_semantics=("arbitrary",), vmem_limit_bytes=VMEM_LIMIT),
        name="in_proj",
    )(x2, g, w_in, wa_pad, ba, qn2, kn2, tri)


def _gla_kernel(qe_ref, ke_ref, kd_ref, dec_ref, v_ref, sr_ref, gn_ref, o_ref, st_ref, sall_ref):
    tl = v_ref.shape[0]
    C = GLA_CHUNK
    nc = tl // C
    npair = GLA_HEADS // 2
    units = [(c, p) for c in range(nc) for p in range(npair)]
    rows_of = lambda c: slice(c * C, (c + 1) * C)
    lanes_of = lambda p: slice(p * LANES, (p + 1) * LANES)
    vcol_of = lambda hd: slice(hd * GLA_DV, (hd + 1) * GLA_DV)
    nt = (((1,), (1,)), ((), ()))
    tn = (((0,), (0,)), ((), ()))

    @pl.when(pl.program_id(1) == 0)
    def _():
        st_ref[...] = jnp.zeros_like(st_ref)

    ri = lax.broadcasted_iota(jnp.int32, (2 * C, 2 * C), 0)
    ci = lax.broadcasted_iota(jnp.int32, (2 * C, 2 * C), 1)
    blockdiag_causal = (ri // C == ci // C) & (ci % C <= ri % C)
    gn = gn_ref[...]
    lo = _lane_lo((C, LANES))

    def split_pair(ref, c, p):
        x = ref[rows_of(c), lanes_of(p)]
        zero = jnp.zeros_like(x)
        return jnp.concatenate([jnp.where(lo, x, zero), jnp.where(lo, zero, x)], axis=0)

    def v_pair(c, p):
        return jnp.concatenate([v_ref[rows_of(c), vcol_of(2 * p)],
                                v_ref[rows_of(c), vcol_of(2 * p + 1)]], axis=0)

    upd = {}
    for c, p in units:
        upd[c, p] = lax.dot_general(v_pair(c, p), split_pair(kd_ref, c, p), tn,
                                    preferred_element_type=F32)
    for p in range(npair):
        st = st_ref[p]
        for c in range(nc):
            sall_ref[c, p] = st.astype(BF16)
            st = st * dec_ref[c:c + 1, lanes_of(p)] + upd.pop((c, p))
        st_ref[p] = st

    sc, out = {}, {}

    def score_stage(c):
        for p in range(npair):
            q2 = split_pair(qe_ref, c, p)
            ke = ke_ref[rows_of(c), lanes_of(p)]
            rhs = jnp.concatenate([ke, ke, sall_ref[c, p]], axis=0)
            sc[c, p] = lax.dot_general(q2, rhs, nt, preferred_element_type=F32)

    def value_stage(c):
        for p in range(npair):
            s2 = sc.pop((c, p))
            am = jnp.where(blockdiag_causal, s2[:, :2 * C], 0.0).astype(BF16)
            out[c, p] = jnp.dot(am, v_pair(c, p), preferred_element_type=F32) + s2[:, 2 * C:]

    def norm_stage(c):
        for p in range(npair):
            o = out.pop((c, p))
            o = o * lax.rsqrt(jnp.mean(o * o, axis=-1, keepdims=True) + EPS) * gn
            for half in range(2):
                vcol = vcol_of(2 * p + half)
                o_ref[rows_of(c), vcol] = (
                    o[half * C:(half + 1) * C] * sr_ref[rows_of(c), vcol]).astype(o_ref.dtype)

    for c in range(nc + 2):
        if c < nc:
            score_stage(c)
        if 0 <= c - 1 < nc:
            value_stage(c - 1)
        if 0 <= c - 2 < nc:
            norm_stage(c - 2)


def _gla(qe, ke, kd, dec, gv, sr, gn, layer, batch, seq):
    tl = min(GLA_TL, seq)
    nt = seq // tl
    row = lambda w: pl.BlockSpec((tl, w), lambda b, t: (b * nt + t, 0))
    return pl.pallas_call(
        _gla_kernel,
        grid=(batch, nt),
        in_specs=[row(GLA_QK), row(GLA_QK), row(GLA_QK),
                  pl.BlockSpec((tl // GLA_CHUNK, GLA_QK), lambda b, t: (b * nt + t, 0)),
                  row(GLA_VW), row(GLA_VW),
                  pl.BlockSpec((None, 1, GLA_DV), lambda b, t: (layer, 0, 0))],
        out_specs=row(GLA_VW),
        out_shape=jax.ShapeDtypeStruct((batch * seq, GLA_VW), BF16),
        scratch_shapes=[
            pltpu.VMEM((GLA_HEADS // 2, GLA_DV, LANES), F32),
            pltpu.VMEM((tl // GLA_CHUNK, GLA_HEADS // 2, GLA_DV, LANES), BF16),
        ],
        compiler_params=pltpu.CompilerParams(
            dimension_semantics=("parallel", "arbitrary"), vmem_limit_bytes=VMEM_LIMIT),
        name="gla",
    )(qe, ke, kd, dec, gv, sr, gn)


def _swa_kernel(sink_ref, q_ref, kc_ref, kp_ref, vc_ref, vp_ref, bias_ref, o_ref, *, layer):
    W = SWA_WINDOW
    tq = q_ref.shape[0]
    first = pl.program_id(1) == 0
    lo = _lane_lo((W, LANES))
    causal = (lax.broadcasted_iota(jnp.int32, (W, W), 1)
              <= lax.broadcasted_iota(jnp.int32, (W, W), 0))
    nt = (((1,), (1,)), ((), ()))
    pairs_per_kv = SWA_GROUP // 2
    units = [(j, kv) for j in range(tq // W) for kv in range(SWA_KV_HEADS)]
    rows_of = lambda j: slice(j * W, (j + 1) * W)
    lanes_of = lambda p: slice(p * LANES, (p + 1) * LANES)

    def window(cur_ref, prev_ref, j, kv):
        prev = prev_ref[:, lanes_of(kv)] if j == 0 else cur_ref[rows_of(j - 1), lanes_of(kv)]
        return jnp.concatenate([prev, cur_ref[rows_of(j), lanes_of(kv)]], axis=0)

    scores, outs = {}, {}

    def score_stage(j, kv):
        parts = []
        for p in range(kv * pairs_per_kv, (kv + 1) * pairs_per_kv):
            qp = q_ref[rows_of(j), lanes_of(p)]
            parts.append(jnp.where(lo, qp, jnp.zeros_like(qp)))
            parts.append(jnp.where(lo, jnp.zeros_like(qp), qp))
        qs = jnp.concatenate(parts, axis=0)
        scores[j, kv] = lax.dot_general(qs, window(kc_ref, kp_ref, j, kv), nt,
                                        preferred_element_type=F32)

    def softmax_stage(j, kv):
        s_all = scores.pop((j, kv))
        pes, denoms = [], []
        for r in range(SWA_GROUP):
            head = kv * SWA_GROUP + r
            table = jnp.where(first, SWA_HEADS + head, head) if j == 0 else head
            s2 = s_all[r * W:(r + 1) * W]
            s = jnp.where(causal, s2[:, W:], s2[:, :W]) + bias_ref[table]
            sink = sink_ref[layer, head] * LOG2E
            m = jnp.maximum(jnp.max(s, axis=-1, keepdims=True), sink)
            e = jnp.exp2(s - m)
            denoms.append(jnp.sum(e, axis=-1, keepdims=True) + jnp.exp2(sink - m))
            eb = e.astype(BF16)
            zero = jnp.zeros_like(eb)
            pes.append(jnp.concatenate(
                [jnp.where(causal, zero, eb), jnp.where(causal, eb, zero)], axis=1))
        o_all = jnp.dot(jnp.concatenate(pes, axis=0), window(vc_ref, vp_ref, j, kv),
                        preferred_element_type=F32)
        for r in range(SWA_GROUP):
            outs[j, kv * SWA_GROUP + r] = o_all[r * W:(r + 1) * W] / denoms[r]

    def store_stage(j):
        for p in range(SWA_QW // LANES):
            o_ref[rows_of(j), lanes_of(p)] = jnp.where(
                lo, outs.pop((j, 2 * p)), outs.pop((j, 2 * p + 1))).astype(o_ref.dtype)

    score_stage(*units[0])
    for i, (j, kv) in enumerate(units):
        if i + 1 < len(units):
            score_stage(*units[i + 1])
        softmax_stage(j, kv)
        if kv == SWA_KV_HEADS - 1:
            store_stage(j)


def _swa(sinks, sq, sk, sv, bias, layer, batch, seq):
    W = SWA_WINDOW
    tq = min(SWA_TQ, seq)
    nq = seq // tq
    bpt = tq // W
    bps = seq // W
    cur = lambda w: pl.BlockSpec((tq, w), lambda b, i: (b * nq + i, 0))
    prev = pl.BlockSpec((W, 2 * SWA_KW), lambda b, i: (b * bps + jnp.maximum(i * bpt - 1, 0), 0))
    return pl.pallas_call(
        functools.partial(_swa_kernel, layer=layer),
        grid=(batch, nq),
        in_specs=[
            pl.BlockSpec(memory_space=pltpu.SMEM),
            cur(SWA_QW), cur(2 * SWA_KW), prev, cur(2 * SWA_KW), prev,
            pl.BlockSpec(bias.shape, lambda b, i: (0, 0, 0)),
        ],
        out_specs=cur(SWA_QW),
        out_shape=jax.ShapeDtypeStruct((batch * seq, SWA_QW), BF16),
        compiler_params=pltpu.CompilerParams(
            dimension_semantics=("parallel", "parallel"), vmem_limit_bytes=VMEM_LIMIT),
        name="swa",
    )(sinks, sq, sk, sk, sv, sv, bias)


def _swa_bias_tables():
    W = SWA_WINDOW
    i = np.arange(W)[:, None]
    j = np.arange(W)[None, :]
    dist = np.where(j <= i, i - j, W + i - j).astype(np.float32)
    slopes = _alibi_slopes(SWA_HEADS)
    base = (-slopes[:, None, None] * dist[None] * LOG2E).astype(np.float32)
    first = np.where((j <= i)[None], base, -np.inf).astype(np.float32)
    return np.concatenate([base, first], axis=0)


def _load_bf16(w_hbm, w_ref, stage_ref, sem_ref):
    rows = stage_ref.shape[1]
    n_chunks = w_ref.shape[0] // rows

    def chunk_copy(i, slot):
        return pltpu.make_async_copy(
            w_hbm.at[pl.ds(pl.multiple_of(i * rows, rows), rows)], stage_ref.at[slot], sem_ref.at[slot])

    chunk_copy(0, 0).start()

    def body(i, carry):
        slot = i % 2

        @pl.when(i + 1 < n_chunks)
        def _():
            chunk_copy(i + 1, 1 - slot).start()

        chunk_copy(i, slot).wait()
        w_ref[pl.ds(pl.multiple_of(i * rows, rows), rows), :] = stage_ref[slot].astype(BF16)
        return carry

    lax.fori_loop(0, n_chunks, body, 0)


def _ffn_kernel(x_ref, og_ref, os_ref, wo_hbm, g_ref, wu_hbm, cw_ref, wd_hbm, y_ref,
                x1_ref, h_ref, t3_ref, ubuf_ref, gbuf_ref, halo_ref,
                wo_ref, wu_ref, wd_ref, wide_stage_ref, stage_ref, sem_ref, *, layer):
    tm = x_ref.shape[0]
    S = SUBLANES
    nv = tm // S
    nb = nv // S
    ng = D_MODEL // LANES
    first = pl.program_id(1) == 0

    @pl.when((pl.program_id(0) == 0) & first)
    def _():
        _load_bf16(wo_hbm.at[layer], wo_ref, stage_ref, sem_ref)
        _load_bf16(wu_hbm.at[layer], wu_ref, wide_stage_ref, sem_ref)
        _load_bf16(wd_hbm.at[layer], wd_ref, stage_ref, sem_ref)

    x1 = (x_ref[...]
          + jnp.dot(og_ref[...], wo_ref[:GLA_VW, :], preferred_element_type=F32)
          + jnp.dot(os_ref[...], wo_ref[GLA_VW:, :], preferred_element_type=F32))
    x1_ref[...] = x1
    ms = jnp.mean(x1 * x1, axis=-1, keepdims=True)
    hn = x1 * lax.rsqrt(ms + EPS) * g_ref[...]
    for a in range(S):
        for b in range(nb):
            src = slice(S * (nb * a + b), S * (nb * a + b) + S)
            dst = slice(S * (S * b + a), S * (S * b + a) + S)
            for g in range(ng):
                t3_ref[g, dst, :] = hn[src, g * LANES:(g + 1) * LANES]
    for k in range(nv // 2):
        rows = []
        for v in (2 * k, 2 * k + 1):
            b, c = v // S, v % S
            rows.append(jnp.concatenate(
                [t3_ref[g, pl.ds(S * S * b + c, S, stride=S), :] for g in range(ng)], axis=1))
        h_ref[2 * S * k:2 * S * (k + 1), :] = jnp.concatenate(rows, axis=0).astype(BF16)

    sub = lax.broadcasted_iota(jnp.int32, (S, 2 * FFN_FC), 0)

    def up_stage(c):
        slot = c % 2
        h = h_ref[...]
        u = jnp.concatenate([
            jnp.dot(h, wu_ref[:, c * FFN_FC:(c + 1) * FFN_FC], preferred_element_type=F32),
            jnp.dot(h, wu_ref[:, D_FF + c * FFN_FC:D_FF + (c + 1) * FFN_FC],
                    preferred_element_type=F32)], axis=1)
        halo = jnp.where(first, 0.0, halo_ref[c])
        halo_ref[c] = u[tm - 2 * S:, :]
        fix2 = jnp.where(sub == 0, pltpu.roll(halo[:S], 1, axis=0),
                         pltpu.roll(u[tm - 2 * S:tm - S], 1, axis=0))
        fix1 = jnp.where(sub == 0, pltpu.roll(halo[S:], 1, axis=0),
                         pltpu.roll(u[tm - S:], 1, axis=0))
        ubuf_ref[slot, 0:S, :] = fix2
        ubuf_ref[slot, S:2 * S, :] = fix1
        ubuf_ref[slot, 2 * S:2 * S + tm, :] = u

    def gate_stage(c):
        slot = c % 2
        ca = slice(c * FFN_FC, (c + 1) * FFN_FC)
        cb = slice(D_FF + c * FFN_FC, D_FF + (c + 1) * FFN_FC)
        cw = jnp.concatenate([cw_ref[:, ca], cw_ref[:, cb]], axis=1)
        for r in range(0, tm, FFN_RB):
            u0 = ubuf_ref[slot, 2 * S + r:2 * S + r + FFN_RB, :]
            u1 = ubuf_ref[slot, S + r:S + r + FFN_RB, :]
            u2 = ubuf_ref[slot, r:r + FFN_RB, :]
            y = u2 * cw[0:1] + u1 * cw[1:2] + u0 * cw[2:3] + cw[3:4]
            a = y[:, :FFN_FC]
            gate = (a / (1.0 + jnp.exp(-a))) * y[:, FFN_FC:]
            gbuf_ref[r:r + FFN_RB, ca] = gate.astype(BF16)

    up_stage(0)
    for c in range(FFN_NC):
        if c + 1 < FFN_NC:
            up_stage(c + 1)
        gate_stage(c)
    d = jnp.dot(gbuf_ref[...], wd_ref[...], preferred_element_type=F32)
    for g in range(ng):
        t3_ref[g] = d[:, g * LANES:(g + 1) * LANES]

    for a in range(S):
        for b in range(nb):
            rows = slice(S * (nb * a + b), S * (nb * a + b) + S)
            ffn = jnp.concatenate(
                [t3_ref[g, pl.ds(S * S * b + a, S, stride=S), :] for g in range(ng)], axis=1)
            y_ref[rows, :] = x1_ref[rows, :] + ffn


def _out_ffn(x2, o_gla, o_swa, wo, g, wu, cw, wd, layer, batch, seq):
    tm = min(FFN_TM, seq)
    assert tm % (SUBLANES * SUBLANES) == 0
    nt = seq // tm
    row = lambda w: pl.BlockSpec((tm, w), lambda b, t: (b * nt + t, 0))
    lay = lambda a: pl.BlockSpec((None,) + a.shape[1:], lambda b, t: (layer,) + (0,) * (a.ndim - 1),
                                 pipeline_mode=pl.Buffered(1))
    hbm = pl.BlockSpec(memory_space=pl.ANY)
    return pl.pallas_call(
        functools.partial(_ffn_kernel, layer=layer),
        grid=(batch, nt),
        in_specs=[row(D_MODEL), row(GLA_VW), row(SWA_QW), hbm, lay(g), hbm, lay(cw), hbm],
        out_specs=row(D_MODEL),
        out_shape=jax.ShapeDtypeStruct((batch * seq, D_MODEL), F32),
        scratch_shapes=[
            pltpu.VMEM((tm, D_MODEL), F32),
            pltpu.VMEM((tm, D_MODEL), BF16),
            pltpu.VMEM((D_MODEL // LANES, tm, LANES), F32),
            pltpu.VMEM((2, tm + 2 * SUBLANES, 2 * FFN_FC), F32),
            pltpu.VMEM((tm, D_FF), BF16),
            pltpu.VMEM((FFN_NC, 2 * SUBLANES, 2 * FFN_FC), F32),
            pltpu.VMEM((D_MODEL, D_MODEL), BF16),
            pltpu.VMEM((D_MODEL, 2 * D_FF), BF16),
            pltpu.VMEM((D_FF, D_MODEL), BF16),
            pltpu.VMEM((2, W_STAGE_ROWS // 2, 2 * D_FF), F32),
            pltpu.VMEM((2, 2 * W_STAGE_ROWS, D_MODEL), F32),
            pltpu.SemaphoreType.DMA((2,)),
        ],
        compiler_params=pltpu.CompilerParams(
            dimension_semantics=("arbitrary", "arbitrary"), vmem_limit_bytes=VMEM_LIMIT),
        name="out_ffn",
    )(x2, o_gla, o_swa, wo, g, wu, cw, wd)


def kernel(x, mix_norm, w_in, w_alpha2, b_alpha, gla_norm, q_norm, k_norm, sinks, w_out, ffn_norm,
           w_up, conv_w, conv_b, w_down):
    batch, seq, d = x.shape
    depth = w_in.shape[0]
    assert d == D_MODEL and w_in.shape[2] == P_IN
    assert all(seq % min(t, seq) == 0 for t in (GLA_TL, SWA_TQ, FFN_TM))
    assert seq % IN_SUB == 0 and (batch * seq) % min(IN_TM, batch * seq) == 0

    wa_pad = jnp.concatenate(
        [w_alpha2, jnp.zeros((depth, LANES - GLA_LOWRANK, GLA_QK), w_alpha2.dtype)],
        axis=1).astype(BF16)
    cw = jnp.concatenate([
        conv_w, conv_b[:, None, :],
        jnp.zeros((depth, SUBLANES - CONV_K - 1, 2 * D_FF), conv_w.dtype)], axis=1)
    row3 = lambda a: a.reshape(depth, 1, a.shape[-1])
    qn2 = row3(jnp.tile(q_norm, (1, 2)))
    kn2 = row3(jnp.tile(k_norm, (1, 2)))

    pos = np.arange(GLA_TRI)
    tri = jnp.asarray(
        (pos[:, None] >= pos[None, :]) & (pos[:, None] // GLA_CHUNK == pos[None, :] // GLA_CHUNK),
        dtype=BF16)
    bias = jnp.asarray(_swa_bias_tables())

    x2 = x.reshape(batch * seq, D_MODEL)
    for l in range(depth):
        qe, ke, kd, dec, gv, sr, sq, sk, sv = _in_proj(
            x2, row3(mix_norm), w_in, wa_pad, row3(b_alpha), qn2, kn2, tri, l)
        o_gla = _gla(qe, ke, kd, dec, gv, sr, row3(gla_norm), l, batch, seq)
        o_swa = _swa(sinks, sq, sk, sv, bias, l, batch, seq)
        x2 = _out_ffn(x2, o_gla, o_swa, w_out, row3(ffn_norm), w_up, cw, w_down, l, batch, seq)
    return x2.reshape(batch, seq, D_MODEL)
```

```python
import functools

import numpy as np
import jax
import jax.numpy as jnp
from jax import lax
from jax.experimental import pallas as pl
from jax.experimental.pallas import tpu as pltpu

F32 = jnp.float32
BF16 = jnp.bfloat16

D_MODEL = 1024
GLA_HEADS = 4
GLA_DV = 128
GLA_DK = 64
GLA_LOWRANK = 16
GLA_TAU = 16.0
GLA_CHUNK = 64
SWA_HEADS = 8
SWA_KV_HEADS = 2
SWA_HD = 64
SWA_WINDOW = 128
D_FF = 2816
CONV_K = 3
EPS = 1e-6
LOG2E = 1.4426950408889634

GLA_QK = GLA_HEADS * GLA_DK
GLA_VW = GLA_HEADS * GLA_DV
SWA_QW = SWA_HEADS * SWA_HD
SWA_KW = SWA_KV_HEADS * SWA_HD
SWA_GROUP = SWA_HEADS // SWA_KV_HEADS

LANES = 128
SUBLANES = 8
VMEM_LIMIT = 56 * 1024 * 1024

_R_LR = 2 * GLA_QK + 2 * GLA_VW
_R_SQ = _R_LR + GLA_LOWRANK
P_IN = _R_SQ + SWA_QW + 2 * SWA_KW
_C_GQ = 0
_C_GK = _C_GQ + GLA_QK
_C_GV = _C_GK + GLA_QK
_C_GR = _C_GV + GLA_VW
_C_SQ = _C_GR + GLA_VW
_C_SK = _C_SQ + SWA_QW
_C_SV = _C_SK + SWA_KW
_C_LR = _C_SV + SWA_KW
P_PACK = _C_LR + LANES

IN_TM = 2048
W_STAGE_ROWS = 128
IN_SUB = 512
GLA_TRI = 128
GLA_TL = 4096
SWA_TQ = 4096
FFN_TM = 512
FFN_FC = 256
FFN_NC = D_FF // FFN_FC
FFN_RB = 64


def _alibi_slopes(n_heads):
    return np.array([2.0 ** (-8.0 * (h + 1) / n_heads) for h in range(n_heads)], dtype=np.float32)


def _lane_lo(shape):
    return lax.broadcasted_iota(jnp.int32, shape, len(shape) - 1) % LANES < (LANES // 2)


def _half_rms_inv(v):
    lo = _lane_lo(v.shape)
    sq = v * v
    ss_lo = jnp.sum(jnp.where(lo, sq, 0.0), axis=-1, keepdims=True)
    ss_hi = jnp.sum(jnp.where(lo, 0.0, sq), axis=-1, keepdims=True)
    inv_lo = lax.rsqrt(ss_lo * (1.0 / SWA_HD) + EPS)
    inv_hi = lax.rsqrt(ss_hi * (1.0 / SWA_HD) + EPS)
    return jnp.where(lo, inv_lo, inv_hi)


def _dup_halves(v):
    lo = _lane_lo(v.shape)
    swapped = pltpu.roll(v, LANES // 2, axis=1)
    return jnp.concatenate([jnp.where(lo, v, swapped), jnp.where(lo, swapped, v)], axis=1)


def _load_packed_w_in(wt_hbm, w_ref, stage_ref, sem_ref):
    n_chunks = P_PACK // LANES
    lr_chunk = n_chunks - 1

    def chunk_copy(j):
        slot = j % 2
        if j == lr_chunk:
            return pltpu.make_async_copy(wt_hbm.at[pl.ds(_R_LR, GLA_LOWRANK)],
                                         stage_ref.at[slot, pl.ds(0, GLA_LOWRANK)], sem_ref.at[slot])
        src = j * LANES if j * LANES < _R_LR else j * LANES + GLA_LOWRANK
        return pltpu.make_async_copy(wt_hbm.at[pl.ds(src, LANES)], stage_ref.at[slot], sem_ref.at[slot])

    row = lax.broadcasted_iota(jnp.int32, (LANES, D_MODEL), 0)
    chunk_copy(0).start()
    for j in range(n_chunks):
        if j + 1 < n_chunks:
            chunk_copy(j + 1).start()
        chunk_copy(j).wait()
        t = stage_ref[j % 2]
        if j == lr_chunk:
            t = jnp.where(row < GLA_LOWRANK, t, 0.0)
        w_ref[:, j * LANES:(j + 1) * LANES] = t.T.astype(BF16)


def _inproj_kernel(x_ref, g_ref, w_hbm, wa_ref, ba_ref, qn_ref, kn_ref, tri_ref,
                   qe_ref, ke_ref, kd_ref, dec_ref, gv_ref, sr_ref, sq_ref, sk_ref, sv_ref,
                   w_ref, stage_ref, sem_ref, *, layer):
    tm = x_ref.shape[0]
    C = GLA_CHUNK
    tri = tri_ref[...]
    tb = tri.shape[0]
    qn = qn_ref[...]

    @pl.when(pl.program_id(0) == 0)
    def _():
        _load_packed_w_in(w_hbm.at[layer], w_ref, stage_ref, sem_ref)

    def norm_stage(r0):
        x = x_ref[r0:r0 + IN_SUB, :]
        ms = jnp.mean(x * x, axis=-1, keepdims=True)
        return (x * lax.rsqrt(ms + EPS) * g_ref[...]).astype(BF16)

    def matmul_stage(r0, h):
        rows = slice(r0, r0 + IN_SUB)
        proj = lambda c0, width: jnp.dot(h, w_ref[:, c0:c0 + width], preferred_element_type=F32)
        glr = proj(_C_LR, LANES).astype(BF16)
        gq = proj(_C_GQ, GLA_QK) * (GLA_DK ** -0.5)
        pre = jnp.dot(glr, wa_ref[...], preferred_element_type=F32) + ba_ref[...]
        la = (jnp.minimum(pre, 0.0) - jnp.log(1.0 + jnp.exp(-jnp.abs(pre)))) * (1.0 / GLA_TAU)
        la_hi = la.astype(BF16)
        la_lo = (la - la_hi.astype(F32)).astype(BF16)
        gk = proj(_C_GK, GLA_QK)
        b_blks = [jnp.dot(tri, la_hi[t0:t0 + tb], preferred_element_type=F32)
                  + jnp.dot(tri, la_lo[t0:t0 + tb], preferred_element_type=F32)
                  for t0 in range(0, IN_SUB, tb)]
        q_all = proj(_C_SQ, SWA_QW)
        kv = proj(_C_SK, 2 * SWA_KW)
        r = proj(_C_GR, GLA_VW)
        gv_ref[rows, :] = proj(_C_GV, GLA_VW).astype(BF16)
        return dict(r=r, b_blks=b_blks, gq=gq, gk=gk, q_all=q_all, kv=kv)

    def tail_stage(r0, v):
        rows = slice(r0, r0 + IN_SUB)
        r = v["r"]
        sr_ref[rows, :] = (r / (1.0 + jnp.exp(-r))).astype(sr_ref.dtype)
        for i, t0 in enumerate(range(0, IN_SUB, tb)):
            for c0 in range(0, tb, C):
                crow = slice(r0 + t0 + c0, r0 + t0 + c0 + C)
                lrow = slice(t0 + c0, t0 + c0 + C)
                b = v["b_blks"][i][c0:c0 + C]
                b_last = b[C - 1:C]
                qe_ref[crow, :] = (v["gq"][lrow] * jnp.exp(b)).astype(BF16)
                ke_ref[crow, :] = (v["gk"][lrow] * jnp.exp(-b)).astype(BF16)
                kd_ref[crow, :] = (v["gk"][lrow] * jnp.exp(b_last - b)).astype(BF16)
                ci = (r0 + t0 + c0) // C
                dec_ref[ci:ci + 1, :] = jnp.exp(b_last)
        for p in range(SWA_QW // LANES):
            q = v["q_all"][:, p * LANES:(p + 1) * LANES]
            sq_ref[rows, p * LANES:(p + 1) * LANES] = (
                q * _half_rms_inv(q) * qn * (SWA_HD ** -0.5 * LOG2E)).astype(BF16)
        k = v["kv"][:, :SWA_KW]
        sk_ref[rows, :] = _dup_halves(k * _half_rms_inv(k) * kn_ref[...]).astype(BF16)
        sv_ref[rows, :] = _dup_halves(v["kv"][:, SWA_KW:]).astype(BF16)

    starts = list(range(0, tm, IN_SUB))
    hs = [norm_stage(r0) for r0 in starts]
    vals = matmul_stage(starts[0], hs[0])
    for i, r0 in enumerate(starts):
        nxt = matmul_stage(starts[i + 1], hs[i + 1]) if i + 1 < len(starts) else None
        tail_stage(r0, vals)
        vals = nxt


def _in_proj(x2, g, w_in, wa_pad, ba, qn2, kn2, tri, layer):
    n = x2.shape[0]
    tm = min(IN_TM, n)
    row = lambda w: pl.BlockSpec((tm, w), lambda i: (i, 0))
    lay = lambda a: pl.BlockSpec((None,) + a.shape[1:], lambda i: (layer,) + (0,) * (a.ndim - 1),
                                 pipeline_mode=pl.Buffered(1))
    outs = [
        (GLA_QK, BF16), (GLA_QK, BF16), (GLA_QK, BF16), None,
        (GLA_VW, BF16), (GLA_VW, BF16), (SWA_QW, BF16), (2 * SWA_KW, BF16), (2 * SWA_KW, BF16),
    ]
    dec_spec = pl.BlockSpec((tm // GLA_CHUNK, GLA_QK), lambda i: (i, 0))
    dec_shape = jax.ShapeDtypeStruct((n // GLA_CHUNK, GLA_QK), F32)
    return pl.pallas_call(
        functools.partial(_inproj_kernel, layer=layer),
        grid=(n // tm,),
        in_specs=[row(D_MODEL), lay(g), pl.BlockSpec(memory_space=pl.ANY), lay(wa_pad), lay(ba),
                  lay(qn2), lay(kn2), pl.BlockSpec(tri.shape, lambda i: (0, 0))],
        out_specs=[dec_spec if o is None else row(o[0]) for o in outs],
        out_shape=[dec_shape if o is None else jax.ShapeDtypeStruct((n, o[0]), o[1]) for o in outs],
        scratch_shapes=[
            pltpu.VMEM((D_MODEL, P_PACK), BF16),
            pltpu.VMEM((2, LANES, D_MODEL), F32),
            pltpu.SemaphoreType.DMA((2,)),
        ],
        compiler_params=pltpu.CompilerParams(
            dimension_semantics=("arbitrary",), vmem_limit_bytes=VMEM_LIMIT),
        name="in_proj",
    )(x2, g, w_in, wa_pad, ba, qn2, kn2, tri)


def _gla_kernel(qe_ref, ke_ref, kd_ref, dec_ref, v_ref, sr_ref, gn_ref, o_ref, st_ref, sall_ref):
    tl = v_ref.shape[0]
    C = GLA_CHUNK
    nc = tl // C
    npair = GLA_HEADS // 2
    units = [(c, p) for c in range(nc) for p in range(npair)]
    rows_of = lambda c: slice(c * C, (c + 1) * C)
    lanes_of = lambda p: slice(p * LANES, (p + 1) * LANES)
    vcol_of = lambda hd: slice(hd * GLA_DV, (hd + 1) * GLA_DV)
    nt = (((1,), (1,)), ((), ()))
    tn = (((0,), (0,)), ((), ()))

    @pl.when(pl.program_id(1) == 0)
    def _():
        st_ref[...] = jnp.zeros_like(st_ref)

    ri = lax.broadcasted_iota(jnp.int32, (2 * C, 2 * C), 0)
    ci = lax.broadcasted_iota(jnp.int32, (2 * C, 2 * C), 1)
    blockdiag_causal = (ri // C == ci // C) & (ci % C <= ri % C)
    gn = gn_ref[...]
    lo = _lane_lo((C, LANES))

    def split_pair(ref, c, p):
        x = ref[rows_of(c), lanes_of(p)]
        zero = jnp.zeros_like(x)
        return jnp.concatenate([jnp.where(lo, x, zero), jnp.where(lo, zero, x)], axis=0)

    def v_pair(c, p):
        return jnp.concatenate([v_ref[rows_of(c), vcol_of(2 * p)],
                                v_ref[rows_of(c), vcol_of(2 * p + 1)]], axis=0)

    upd = {}
    for c, p in units:
        upd[c, p] = lax.dot_general(v_pair(c, p), split_pair(kd_ref, c, p), tn,
                                    preferred_element_type=F32)
    for p in range(npair):
        st = st_ref[p]
        for c in range(nc):
            sall_ref[c, p] = st.astype(BF16)
            st = st * dec_ref[c:c + 1, lanes_of(p)] + upd.pop((c, p))
        st_ref[p] = st

    sc, out = {}, {}

    def score_stage(c):
        for p in range(npair):
            q2 = split_pair(qe_ref, c, p)
            ke = ke_ref[rows_of(c), lanes_of(p)]
            rhs = jnp.concatenate([ke, ke, sall_ref[c, p]], axis=0)
            sc[c, p] = lax.dot_general(q2, rhs, nt, preferred_element_type=F32)

    def value_stage(c):
        for p in range(npair):
            s2 = sc.pop((c, p))
            am = jnp.where(blockdiag_causal, s2[:, :2 * C], 0.0).astype(BF16)
            out[c, p] = jnp.dot(am, v_pair(c, p), preferred_element_type=F32) + s2[:, 2 * C:]

    def norm_stage(c):
        for p in range(npair):
            o = out.pop((c, p))
            o = o * lax.rsqrt(jnp.mean(o * o, axis=-1, keepdims=True) + EPS) * gn
            for half in range(2):
                vcol = vcol_of(2 * p + half)
                o_ref[rows_of(c), vcol] = (
                    o[half * C:(half + 1) * C] * sr_ref[rows_of(c), vcol]).astype(o_ref.dtype)

    for c in range(nc + 2):
        if c < nc:
            score_stage(c)
        if 0 <= c - 1 < nc:
            value_stage(c - 1)
        if 0 <= c - 2 < nc:
            norm_stage(c - 2)


def _gla(qe, ke, kd, dec, gv, sr, gn, layer, batch, seq):
    tl = min(GLA_TL, seq)
    nt = seq // tl
    row = lambda w: pl.BlockSpec((tl, w), lambda b, t: (b * nt + t, 0))
    return pl.pallas_call(
        _gla_kernel,
        grid=(batch, nt),
        in_specs=[row(GLA_QK), row(GLA_QK), row(GLA_QK),
                  pl.BlockSpec((tl // GLA_CHUNK, GLA_QK), lambda b, t: (b * nt + t, 0)),
                  row(GLA_VW), row(GLA_VW),
                  pl.BlockSpec((None, 1, GLA_DV), lambda b, t: (layer, 0, 0))],
        out_specs=row(GLA_VW),
        out_shape=jax.ShapeDtypeStruct((batch * seq, GLA_VW), BF16),
        scratch_shapes=[
            pltpu.VMEM((GLA_HEADS // 2, GLA_DV, LANES), F32),
            pltpu.VMEM((tl // GLA_CHUNK, GLA_HEADS // 2, GLA_DV, LANES), BF16),
        ],
        compiler_params=pltpu.CompilerParams(
            dimension_semantics=("parallel", "arbitrary"), vmem_limit_bytes=VMEM_LIMIT),
        name="gla",
    )(qe, ke, kd, dec, gv, sr, gn)


def _swa_kernel(sink_ref, q_ref, kc_ref, kp_ref, vc_ref, vp_ref, bias_ref, o_ref, *, layer):
    W = SWA_WINDOW
    tq = q_ref.shape[0]
    first = pl.program_id(1) == 0
    lo = _lane_lo((W, LANES))
    causal = (lax.broadcasted_iota(jnp.int32, (W, W), 1)
              <= lax.broadcasted_iota(jnp.int32, (W, W), 0))
    nt = (((1,), (1,)), ((), ()))
    pairs_per_kv = SWA_GROUP // 2
    units = [(j, kv) for j in range(tq // W) for kv in range(SWA_KV_HEADS)]
    rows_of = lambda j: slice(j * W, (j + 1) * W)
    lanes_of = lambda p: slice(p * LANES, (p + 1) * LANES)

    def window(cur_ref, prev_ref, j, kv):
        prev = prev_ref[:, lanes_of(kv)] if j == 0 else cur_ref[rows_of(j - 1), lanes_of(kv)]
        return jnp.concatenate([prev, cur_ref[rows_of(j), lanes_of(kv)]], axis=0)

    scores, outs = {}, {}

    def score_stage(j, kv):
        parts = []
        for p in range(kv * pairs_per_kv, (kv + 1) * pairs_per_kv):
            qp = q_ref[rows_of(j), lanes_of(p)]
            parts.append(jnp.where(lo, qp, jnp.zeros_like(qp)))
            parts.append(jnp.where(lo, jnp.zeros_like(qp), qp))
        qs = jnp.concatenate(parts, axis=0)
        scores[j, kv] = lax.dot_general(qs, window(kc_ref, kp_ref, j, kv), nt,
                                        preferred_element_type=F32)

    def softmax_stage(j, kv):
        s_all = scores.pop((j, kv))
        pes, denoms = [], []
        for r in range(SWA_GROUP):
            head = kv * SWA_GROUP + r
            table = jnp.where(first, SWA_HEADS + head, head) if j == 0 else head
            s2 = s_all[r * W:(r + 1) * W]
            s = jnp.where(causal, s2[:, W:], s2[:, :W]) + bias_ref[table]
            sink = sink_ref[layer, head] * LOG2E
            m = jnp.maximum(jnp.max(s, axis=-1, keepdims=True), sink)
            e = jnp.exp2(s - m)
            denoms.append(jnp.sum(e, axis=-1, keepdims=True) + jnp.exp2(sink - m))
            eb = e.astype(BF16)
            zero = jnp.zeros_like(eb)
            pes.append(jnp.concatenate(
                [jnp.where(causal, zero, eb), jnp.where(causal, eb, zero)], axis=1))
        o_all = jnp.dot(jnp.concatenate(pes, axis=0), window(vc_ref, vp_ref, j, kv),
                        preferred_element_type=F32)
        for r in range(SWA_GROUP):
            outs[j, kv * SWA_GROUP + r] = o_all[r * W:(r + 1) * W] / denoms[r]

    def store_stage(j):
        for p in range(SWA_QW // LANES):
            o_ref[rows_of(j), lanes_of(p)] = jnp.where(
                lo, outs.pop((j, 2 * p)), outs.pop((j, 2 * p + 1))).astype(o_ref.dtype)

    score_stage(*units[0])
    for i, (j, kv) in enumerate(units):
        if i + 1 < len(units):
            score_stage(*units[i + 1])
        softmax_stage(j, kv)
        if kv == SWA_KV_HEADS - 1:
            store_stage(j)


def _swa(sinks, sq, sk, sv, bias, layer, batch, seq):
    W = SWA_WINDOW
    tq = min(SWA_TQ, seq)
    nq = seq // tq
    bpt = tq // W
    bps = seq // W
    cur = lambda w: pl.BlockSpec((tq, w), lambda b, i: (b * nq + i, 0))
    prev = pl.BlockSpec((W, 2 * SWA_KW), lambda b, i: (b * bps + jnp.maximum(i * bpt - 1, 0), 0))
    return pl.pallas_call(
        functools.partial(_swa_kernel, layer=layer),
        grid=(batch, nq),
        in_specs=[
            pl.BlockSpec(memory_space=pltpu.SMEM),
            cur(SWA_QW), cur(2 * SWA_KW), prev, cur(2 * SWA_KW), prev,
            pl.BlockSpec(bias.shape, lambda b, i: (0, 0, 0)),
        ],
        out_specs=cur(SWA_QW),
        out_shape=jax.ShapeDtypeStruct((batch * seq, SWA_QW), BF16),
        compiler_params=pltpu.CompilerParams(
            dimension_semantics=("parallel", "parallel"), vmem_limit_bytes=VMEM_LIMIT),
        name="swa",
    )(sinks, sq, sk, sk, sv, sv, bias)


def _swa_bias_tables():
    W = SWA_WINDOW
    i = np.arange(W)[:, None]
    j = np.arange(W)[None, :]
    dist = np.where(j <= i, i - j, W + i - j).astype(np.float32)
    slopes = _alibi_slopes(SWA_HEADS)
    base = (-slopes[:, None, None] * dist[None] * LOG2E).astype(np.float32)
    first = np.where((j <= i)[None], base, -np.inf).astype(np.float32)
    return np.concatenate([base, first], axis=0)


def _load_bf16(w_hbm, w_ref, stage_ref, sem_ref):
    rows = stage_ref.shape[1]
    n_chunks = w_ref.shape[0] // rows

    def chunk_copy(i, slot):
        return pltpu.make_async_copy(
            w_hbm.at[pl.ds(pl.multiple_of(i * rows, rows), rows)], stage_ref.at[slot], sem_ref.at[slot])

    chunk_copy(0, 0).start()

    def body(i, carry):
        slot = i % 2

        @pl.when(i + 1 < n_chunks)
        def _():
            chunk_copy(i + 1, 1 - slot).start()

        chunk_copy(i, slot).wait()
        w_ref[pl.ds(pl.multiple_of(i * rows, rows), rows), :] = stage_ref[slot].astype(BF16)
        return carry

    lax.fori_loop(0, n_chunks, body, 0)


def _ffn_kernel(x_ref, og_ref, os_ref, wo_hbm, g_ref, wu_hbm, cw_ref, wd_hbm, y_ref,
                x1_ref, h_ref, t3_ref, ubuf_ref, gbuf_ref, halo_ref,
                wo_ref, wu_ref, wd_ref, wide_stage_ref, stage_ref, sem_ref, *, layer):
    tm = x_ref.shape[0]
    S = SUBLANES
    nv = tm // S
    nb = nv // S
    ng = D_MODEL // LANES
    first = pl.program_id(1) == 0

    @pl.when((pl.program_id(0) == 0) & first)
    def _():
        _load_bf16(wo_hbm.at[layer], wo_ref, stage_ref, sem_ref)
        _load_bf16(wu_hbm.at[layer], wu_ref, wide_stage_ref, sem_ref)
        _load_bf16(wd_hbm.at[layer], wd_ref, stage_ref, sem_ref)

    x1 = (x_ref[...]
          + jnp.dot(og_ref[...], wo_ref[:GLA_VW, :], preferred_element_type=F32)
          + jnp.dot(os_ref[...], wo_ref[GLA_VW:, :], preferred_element_type=F32))
    x1_ref[...] = x1
    ms = jnp.mean(x1 * x1, axis=-1, keepdims=True)
    hn = x1 * lax.rsqrt(ms + EPS) * g_ref[...]
    for a in range(S):
        for b in range(nb):
            src = slice(S * (nb * a + b), S * (nb * a + b) + S)
            dst = slice(S * (S * b + a), S * (S * b + a) + S)
            for g in range(ng):
                t3_ref[g, dst, :] = hn[src, g * LANES:(g + 1) * LANES]
    for k in range(nv // 2):
        rows = []
        for v in (2 * k, 2 * k + 1):
            b, c = v // S, v % S
            rows.append(jnp.concatenate(
                [t3_ref[g, pl.ds(S * S * b + c, S, stride=S), :] for g in range(ng)], axis=1))
        h_ref[2 * S * k:2 * S * (k + 1), :] = jnp.concatenate(rows, axis=0).astype(BF16)

    sub = lax.broadcasted_iota(jnp.int32, (S, 2 * FFN_FC), 0)

    def up_stage(c):
        slot = c % 2
        h = h_ref[...]
        u = jnp.concatenate([
            jnp.dot(h, wu_ref[:, c * FFN_FC:(c + 1) * FFN_FC], preferred_element_type=F32),
            jnp.dot(h, wu_ref[:, D_FF + c * FFN_FC:D_FF + (c + 1) * FFN_FC],
                    preferred_element_type=F32)], axis=1)
        halo = jnp.where(first, 0.0, halo_ref[c])
        halo_ref[c] = u[tm - 2 * S:, :]
        fix2 = jnp.where(sub == 0, pltpu.roll(halo[:S], 1, axis=0),
                         pltpu.roll(u[tm - 2 * S:tm - S], 1, axis=0))
        fix1 = jnp.where(sub == 0, pltpu.roll(halo[S:], 1, axis=0),
                         pltpu.roll(u[tm - S:], 1, axis=0))
        ubuf_ref[slot, 0:S, :] = fix2
        ubuf_ref[slot, S:2 * S, :] = fix1
        ubuf_ref[slot, 2 * S:2 * S + tm, :] = u

    def gate_stage(c):
        slot = c % 2
        ca = slice(c * FFN_FC, (c + 1) * FFN_FC)
        cb = slice(D_FF + c * FFN_FC, D_FF + (c + 1) * FFN_FC)
        cw = jnp.concatenate([cw_ref[:, ca], cw_ref[:, cb]], axis=1)
        for r in range(0, tm, FFN_RB):
            u0 = ubuf_ref[slot, 2 * S + r:2 * S + r + FFN_RB, :]
            u1 = ubuf_ref[slot, S + r:S + r + FFN_RB, :]
            u2 = ubuf_ref[slot, r:r + FFN_RB, :]
            y = u2 * cw[0:1] + u1 * cw[1:2] + u0 * cw[2:3] + cw[3:4]
            a = y[:, :FFN_FC]
            gate = (a / (1.0 + jnp.exp(-a))) * y[:, FFN_FC:]
            gbuf_ref[r:r + FFN_RB, ca] = gate.astype(BF16)

    up_stage(0)
    for c in range(FFN_NC):
        if c + 1 < FFN_NC:
            up_stage(c + 1)
        gate_stage(c)
    d = jnp.dot(gbuf_ref[...], wd_ref[...], preferred_element_type=F32)
    for g in range(ng):
        t3_ref[g] = d[:, g * LANES:(g + 1) * LANES]

    for a in range(S):
        for b in range(nb):
            rows = slice(S * (nb * a + b), S * (nb * a + b) + S)
            ffn = jnp.concatenate(
                [t3_ref[g, pl.ds(S * S * b + a, S, stride=S), :] for g in range(ng)], axis=1)
            y_ref[rows, :] = x1_ref[rows, :] + ffn


def _out_ffn(x2, o_gla, o_swa, wo, g, wu, cw, wd, layer, batch, seq):
    tm = min(FFN_TM, seq)
    assert tm % (SUBLANES * SUBLANES) == 0
    nt = seq // tm
    row = lambda w: pl.BlockSpec((tm, w), lambda b, t: (b * nt + t, 0))
    lay = lambda a: pl.BlockSpec((None,) + a.shape[1:], lambda b, t: (layer,) + (0,) * (a.ndim - 1),
                                 pipeline_mode=pl.Buffered(1))
    hbm = pl.BlockSpec(memory_space=pl.ANY)
    return pl.pallas_call(
        functools.partial(_ffn_kernel, layer=layer),
        grid=(batch, nt),
        in_specs=[row(D_MODEL), row(GLA_VW), row(SWA_QW), hbm, lay(g), hbm, lay(cw), hbm],
        out_specs=row(D_MODEL),
        out_shape=jax.ShapeDtypeStruct((batch * seq, D_MODEL), F32),
        scratch_shapes=[
            pltpu.VMEM((tm, D_MODEL), F32),
            pltpu.VMEM((tm, D_MODEL), BF16),
            pltpu.VMEM((D_MODEL // LANES, tm, LANES), F32),
            pltpu.VMEM((2, tm + 2 * SUBLANES, 2 * FFN_FC), F32),
            pltpu.VMEM((tm, D_FF), BF16),
            pltpu.VMEM((FFN_NC, 2 * SUBLANES, 2 * FFN_FC), F32),
            pltpu.VMEM((D_MODEL, D_MODEL), BF16),
            pltpu.VMEM((D_MODEL, 2 * D_FF), BF16),
            pltpu.VMEM((D_FF, D_MODEL), BF16),
            pltpu.VMEM((2, W_STAGE_ROWS // 2, 2 * D_FF), F32),
            pltpu.VMEM((2, 2 * W_STAGE_ROWS, D_MODEL), F32),
            pltpu.SemaphoreType.DMA((2,)),
        ],
        compiler_params=pltpu.CompilerParams(
            dimension_semantics=("arbitrary", "arbitrary"), vmem_limit_bytes=VMEM_LIMIT),
        name="out_ffn",
    )(x2, o_gla, o_swa, wo, g, wu, cw, wd)


def kernel(x, mix_norm, w_in, w_alpha2, b_alpha, gla_norm, q_norm, k_norm, sinks, w_out, ffn_norm,
           w_up, conv_w, conv_b, w_down):
    batch, seq, d = x.shape
    depth = w_in.shape[0]
    assert d == D_MODEL and w_in.shape[2] == P_IN
    assert all(seq % min(t, seq) == 0 for t in (GLA_TL, SWA_TQ, FFN_TM))
    assert seq % IN_SUB == 0 and (batch * seq) % min(IN_TM, batch * seq) == 0

    w_in_t = jnp.swapaxes(w_in, 1, 2)
    wa_pad = jnp.concatenate(
        [w_alpha2, jnp.zeros((depth, LANES - GLA_LOWRANK, GLA_QK), w_alpha2.dtype)],
        axis=1).astype(BF16)
    cw = jnp.concatenate([
        conv_w, conv_b[:, None, :],
        jnp.zeros((depth, SUBLANES - CONV_K - 1, 2 * D_FF), conv_w.dtype)], axis=1)
    row3 = lambda a: a.reshape(depth, 1, a.shape[-1])
    qn2 = row3(jnp.tile(q_norm, (1, 2)))
    kn2 = row3(jnp.tile(k_norm, (1, 2)))

    pos = np.arange(GLA_TRI)
    tri = jnp.asarray(
        (pos[:, None] >= pos[None, :]) & (pos[:, None] // GLA_CHUNK == pos[None, :] // GLA_CHUNK),
        dtype=BF16)
    bias = jnp.asarray(_swa_bias_tables())

    x2 = x.reshape(batch * seq, D_MODEL)
    for l in range(depth):
        qe, ke, kd, dec, gv, sr, sq, sk, sv = _in_proj(
            x2, row3(mix_norm), w_in_t, wa_pad, row3(b_alpha), qn2, kn2, tri, l)
        o_gla = _gla(qe, ke, kd, dec, gv, sr, row3(gla_norm), l, batch, seq)
        o_swa = _swa(sinks, sq, sk, sv, bias, l, batch, seq)
        x2 = _out_ffn(x2, o_gla, o_swa, w_out, row3(ffn_norm), w_up, cw, w_down, l, batch, seq)
    return x2.reshape(batch, seq, D_MODEL)
```

```python
import functools

import numpy as np
import jax
import jax.numpy as jnp
from jax import lax
from jax.experimental import pallas as pl
from jax.experimental.pallas import tpu as pltpu

F32 = jnp.float32
BF16 = jnp.bfloat16

D_MODEL = 1024
GLA_HEADS = 4
GLA_DV = 128
GLA_DK = 64
GLA_LOWRANK = 16
GLA_TAU = 16.0
GLA_CHUNK = 64
SWA_HEADS = 8
SWA_KV_HEADS = 2
SWA_HD = 64
SWA_WINDOW = 128
D_FF = 2816
CONV_K = 3
EPS = 1e-6
LOG2E = 1.4426950408889634

GLA_QK = GLA_HEADS * GLA_DK
GLA_VW = GLA_HEADS * GLA_DV
SWA_QW = SWA_HEADS * SWA_HD
SWA_KW = SWA_KV_HEADS * SWA_HD
SWA_GROUP = SWA_HEADS // SWA_KV_HEADS

LANES = 128
SUBLANES = 8
VMEM_LIMIT = 56 * 1024 * 1024

_R_LR = 2 * GLA_QK + 2 * GLA_VW
_R_SQ = _R_LR + GLA_LOWRANK
P_IN = _R_SQ + SWA_QW + 2 * SWA_KW
_C_GQ = 0
_C_GK = _C_GQ + GLA_QK
_C_GV = _C_GK + GLA_QK
_C_GR = _C_GV + GLA_VW
_C_SQ = _C_GR + GLA_VW
_C_SK = _C_SQ + SWA_QW
_C_SV = _C_SK + SWA_KW
_C_LR = _C_SV + SWA_KW
P_PACK = _C_LR + LANES

IN_TM = 2048
W_STAGE_ROWS = 128
IN_SUB = 512
GLA_TRI = 128
GLA_TL = 4096
SWA_TQ = 4096
FFN_TM = 512
FFN_FC = 256
FFN_NC = D_FF // FFN_FC
FFN_RB = 64


def _alibi_slopes(n_heads):
    return np.array([2.0 ** (-8.0 * (h + 1) / n_heads) for h in range(n_heads)], dtype=np.float32)


def _lane_lo(shape):
    return lax.broadcasted_iota(jnp.int32, shape, len(shape) - 1) % LANES < (LANES // 2)


def _half_rms_inv(v):
    lo = _lane_lo(v.shape)
    sq = v * v
    ss_lo = jnp.sum(jnp.where(lo, sq, 0.0), axis=-1, keepdims=True)
    ss_hi = jnp.sum(jnp.where(lo, 0.0, sq), axis=-1, keepdims=True)
    inv_lo = lax.rsqrt(ss_lo * (1.0 / SWA_HD) + EPS)
    inv_hi = lax.rsqrt(ss_hi * (1.0 / SWA_HD) + EPS)
    return jnp.where(lo, inv_lo, inv_hi)


def _dup_halves(v):
    lo = _lane_lo(v.shape)
    swapped = pltpu.roll(v, LANES // 2, axis=1)
    return jnp.concatenate([jnp.where(lo, v, swapped), jnp.where(lo, swapped, v)], axis=1)


def _load_packed_w_in(wt_hbm, w_ref, stage_ref, sem_ref):
    n_chunks = P_PACK // LANES
    lr_chunk = n_chunks - 1

    def chunk_copy(j):
        slot = j % 2
        if j == lr_chunk:
            return pltpu.make_async_copy(wt_hbm.at[pl.ds(_R_LR, GLA_LOWRANK)],
                                         stage_ref.at[slot, pl.ds(0, GLA_LOWRANK)], sem_ref.at[slot])
        src = j * LANES if j * LANES < _R_LR else j * LANES + GLA_LOWRANK
        return pltpu.make_async_copy(wt_hbm.at[pl.ds(src, LANES)], stage_ref.at[slot], sem_ref.at[slot])

    row = lax.broadcasted_iota(jnp.int32, (LANES, D_MODEL), 0)
    chunk_copy(0).start()
    for j in range(n_chunks):
        if j + 1 < n_chunks:
            chunk_copy(j + 1).start()
        chunk_copy(j).wait()
        t = stage_ref[j % 2]
        if j == lr_chunk:
            t = jnp.where(row < GLA_LOWRANK, t, 0.0)
        w_ref[j * LANES:(j + 1) * LANES, :] = t.astype(BF16)


def _inproj_kernel(x_ref, g_ref, w_hbm, wa_ref, ba_ref, qn_ref, kn_ref, tri_ref,
                   qe_ref, ke_ref, kd_ref, dec_ref, gv_ref, sr_ref, sq_ref, sk_ref, sv_ref,
                   w_ref, stage_ref, sem_ref, *, layer):
    tm = x_ref.shape[0]
    C = GLA_CHUNK
    tri = tri_ref[...]
    tb = tri.shape[0]
    qn = qn_ref[...]

    @pl.when(pl.program_id(0) == 0)
    def _():
        _load_packed_w_in(w_hbm.at[layer], w_ref, stage_ref, sem_ref)

    def norm_stage(r0):
        x = x_ref[r0:r0 + IN_SUB, :]
        ms = jnp.mean(x * x, axis=-1, keepdims=True)
        return (x * lax.rsqrt(ms + EPS) * g_ref[...]).astype(BF16)

    def matmul_stage(r0, h):
        rows = slice(r0, r0 + IN_SUB)
        proj = lambda c0, width: lax.dot_general(h, w_ref[c0:c0 + width, :], (((1,), (1,)), ((), ())),
                                                 preferred_element_type=F32)
        glr = proj(_C_LR, LANES).astype(BF16)
        gq = proj(_C_GQ, GLA_QK) * (GLA_DK ** -0.5)
        pre = jnp.dot(glr, wa_ref[...], preferred_element_type=F32) + ba_ref[...]
        la = (jnp.minimum(pre, 0.0) - jnp.log(1.0 + jnp.exp(-jnp.abs(pre)))) * (1.0 / GLA_TAU)
        la_hi = la.astype(BF16)
        la_lo = (la - la_hi.astype(F32)).astype(BF16)
        gk = proj(_C_GK, GLA_QK)
        b_blks = [jnp.dot(tri, la_hi[t0:t0 + tb], preferred_element_type=F32)
                  + jnp.dot(tri, la_lo[t0:t0 + tb], preferred_element_type=F32)
                  for t0 in range(0, IN_SUB, tb)]
        q_all = proj(_C_SQ, SWA_QW)
        kv = proj(_C_SK, 2 * SWA_KW)
        r = proj(_C_GR, GLA_VW)
        gv_ref[rows, :] = proj(_C_GV, GLA_VW).astype(BF16)
        return dict(r=r, b_blks=b_blks, gq=gq, gk=gk, q_all=q_all, kv=kv)

    def tail_stage(r0, v):
        rows = slice(r0, r0 + IN_SUB)
        r = v["r"]
        sr_ref[rows, :] = (r / (1.0 + jnp.exp(-r))).astype(sr_ref.dtype)
        for i, t0 in enumerate(range(0, IN_SUB, tb)):
            for c0 in range(0, tb, C):
                crow = slice(r0 + t0 + c0, r0 + t0 + c0 + C)
                lrow = slice(t0 + c0, t0 + c0 + C)
                b = v["b_blks"][i][c0:c0 + C]
                b_last = b[C - 1:C]
                qe_ref[crow, :] = (v["gq"][lrow] * jnp.exp(b)).astype(BF16)
                ke_ref[crow, :] = (v["gk"][lrow] * jnp.exp(-b)).astype(BF16)
                kd_ref[crow, :] = (v["gk"][lrow] * jnp.exp(b_last - b)).astype(BF16)
                ci = (r0 + t0 + c0) // C
                dec_ref[ci:ci + 1, :] = jnp.exp(b_last)
        for p in range(SWA_QW // LANES):
            q = v["q_all"][:, p * LANES:(p + 1) * LANES]
            sq_ref[rows, p * LANES:(p + 1) * LANES] = (
                q * _half_rms_inv(q) * qn * (SWA_HD ** -0.5 * LOG2E)).astype(BF16)
        k = v["kv"][:, :SWA_KW]
        sk_ref[rows, :] = _dup_halves(k * _half_rms_inv(k) * kn_ref[...]).astype(BF16)
        sv_ref[rows, :] = _dup_halves(v["kv"][:, SWA_KW:]).astype(BF16)

    starts = list(range(0, tm, IN_SUB))
    hs = [norm_stage(r0) for r0 in starts]
    vals = matmul_stage(starts[0], hs[0])
    for i, r0 in enumerate(starts):
        nxt = matmul_stage(starts[i + 1], hs[i + 1]) if i + 1 < len(starts) else None
        tail_stage(r0, vals)
        vals = nxt


def _in_proj(x2, g, w_in, wa_pad, ba, qn2, kn2, tri, layer):
    n = x2.shape[0]
    tm = min(IN_TM, n)
    row = lambda w: pl.BlockSpec((tm, w), lambda i: (i, 0))
    lay = lambda a: pl.BlockSpec((None,) + a.shape[1:], lambda i: (layer,) + (0,) * (a.ndim - 1),
                                 pipeline_mode=pl.Buffered(1))
    outs = [
        (GLA_QK, BF16), (GLA_QK, BF16), (GLA_QK, BF16), None,
        (GLA_VW, BF16), (GLA_VW, BF16), (SWA_QW, BF16), (2 * SWA_KW, BF16), (2 * SWA_KW, BF16),
    ]
    dec_spec = pl.BlockSpec((tm // GLA_CHUNK, GLA_QK), lambda i: (i, 0))
    dec_shape = jax.ShapeDtypeStruct((n // GLA_CHUNK, GLA_QK), F32)
    return pl.pallas_call(
        functools.partial(_inproj_kernel, layer=layer),
        grid=(n // tm,),
        in_specs=[row(D_MODEL), lay(g), pl.BlockSpec(memory_space=pl.ANY), lay(wa_pad), lay(ba),
                  lay(qn2), lay(kn2), pl.BlockSpec(tri.shape, lambda i: (0, 0))],
        out_specs=[dec_spec if o is None else row(o[0]) for o in outs],
        out_shape=[dec_shape if o is None else jax.ShapeDtypeStruct((n, o[0]), o[1]) for o in outs],
        scratch_shapes=[
            pltpu.VMEM((P_PACK, D_MODEL), BF16),
            pltpu.VMEM((2, LANES, D_MODEL), F32),
            pltpu.SemaphoreType.DMA((2,)),
        ],
        compiler_params=pltpu.CompilerParams(
            dimension_semantics=("arbitrary",), vmem_limit_bytes=VMEM_LIMIT),
        name="in_proj",
    )(x2, g, w_in, wa_pad, ba, qn2, kn2, tri)


def _gla_kernel(qe_ref, ke_ref, kd_ref, dec_ref, v_ref, sr_ref, gn_ref, o_ref, st_ref, sall_ref):
    tl = v_ref.shape[0]
    C = GLA_CHUNK
    nc = tl // C
    npair = GLA_HEADS // 2
    units = [(c, p) for c in range(nc) for p in range(npair)]
    rows_of = lambda c: slice(c * C, (c + 1) * C)
    lanes_of = lambda p: slice(p * LANES, (p + 1) * LANES)
    vcol_of = lambda hd: slice(hd * GLA_DV, (hd + 1) * GLA_DV)
    nt = (((1,), (1,)), ((), ()))
    tn = (((0,), (0,)), ((), ()))

    @pl.when(pl.program_id(1) == 0)
    def _():
        st_ref[...] = jnp.zeros_like(st_ref)

    ri = lax.broadcasted_iota(jnp.int32, (2 * C, 2 * C), 0)
    ci = lax.broadcasted_iota(jnp.int32, (2 * C, 2 * C), 1)
    blockdiag_causal = (ri // C == ci // C) & (ci % C <= ri % C)
    gn = gn_ref[...]
    lo = _lane_lo((C, LANES))

    def split_pair(ref, c, p):
        x = ref[rows_of(c), lanes_of(p)]
        zero = jnp.zeros_like(x)
        return jnp.concatenate([jnp.where(lo, x, zero), jnp.where(lo, zero, x)], axis=0)

    def v_pair(c, p):
        return jnp.concatenate([v_ref[rows_of(c), vcol_of(2 * p)],
                                v_ref[rows_of(c), vcol_of(2 * p + 1)]], axis=0)

    upd = {}
    for c, p in units:
        upd[c, p] = lax.dot_general(v_pair(c, p), split_pair(kd_ref, c, p), tn,
                                    preferred_element_type=F32)
    for p in range(npair):
        st = st_ref[p]
        for c in range(nc):
            sall_ref[c, p] = st.astype(BF16)
            st = st * dec_ref[c:c + 1, lanes_of(p)] + upd.pop((c, p))
        st_ref[p] = st

    sc, out = {}, {}

    def score_stage(c):
        for p in range(npair):
            q2 = split_pair(qe_ref, c, p)
            ke = ke_ref[rows_of(c), lanes_of(p)]
            rhs = jnp.concatenate([ke, ke, sall_ref[c, p]], axis=0)
            sc[c, p] = lax.dot_general(q2, rhs, nt, preferred_element_type=F32)

    def value_stage(c):
        for p in range(npair):
            s2 = sc.pop((c, p))
            am = jnp.where(blockdiag_causal, s2[:, :2 * C], 0.0).astype(BF16)
            out[c, p] = jnp.dot(am, v_pair(c, p), preferred_element_type=F32) + s2[:, 2 * C:]

    def norm_stage(c):
        for p in range(npair):
            o = out.pop((c, p))
            o = o * lax.rsqrt(jnp.mean(o * o, axis=-1, keepdims=True) + EPS) * gn
            for half in range(2):
                vcol = vcol_of(2 * p + half)
                o_ref[rows_of(c), vcol] = (
                    o[half * C:(half + 1) * C] * sr_ref[rows_of(c), vcol]).astype(o_ref.dtype)

    for c in range(nc + 2):
        if c < nc:
            score_stage(c)
        if 0 <= c - 1 < nc:
            value_stage(c - 1)
        if 0 <= c - 2 < nc:
            norm_stage(c - 2)


def _gla(qe, ke, kd, dec, gv, sr, gn, layer, batch, seq):
    tl = min(GLA_TL, seq)
    nt = seq // tl
    row = lambda w: pl.BlockSpec((tl, w), lambda b, t: (b * nt + t, 0))
    return pl.pallas_call(
        _gla_kernel,
        grid=(batch, nt),
        in_specs=[row(GLA_QK), row(GLA_QK), row(GLA_QK),
                  pl.BlockSpec((tl // GLA_CHUNK, GLA_QK), lambda b, t: (b * nt + t, 0)),
                  row(GLA_VW), row(GLA_VW),
                  pl.BlockSpec((None, 1, GLA_DV), lambda b, t: (layer, 0, 0))],
        out_specs=row(GLA_VW),
        out_shape=jax.ShapeDtypeStruct((batch * seq, GLA_VW), BF16),
        scratch_shapes=[
            pltpu.VMEM((GLA_HEADS // 2, GLA_DV, LANES), F32),
            pltpu.VMEM((tl // GLA_CHUNK, GLA_HEADS // 2, GLA_DV, LANES), BF16),
        ],
        compiler_params=pltpu.CompilerParams(
            dimension_semantics=("parallel", "arbitrary"), vmem_limit_bytes=VMEM_LIMIT),
        name="gla",
    )(qe, ke, kd, dec, gv, sr, gn)


def _swa_kernel(sink_ref, q_ref, kc_ref, kp_ref, vc_ref, vp_ref, bias_ref, o_ref, *, layer):
    W = SWA_WINDOW
    tq = q_ref.shape[0]
    first = pl.program_id(1) == 0
    lo = _lane_lo((W, LANES))
    causal = (lax.broadcasted_iota(jnp.int32, (W, W), 1)
              <= lax.broadcasted_iota(jnp.int32, (W, W), 0))
    nt = (((1,), (1,)), ((), ()))
    pairs_per_kv = SWA_GROUP // 2
    units = [(j, kv) for j in range(tq // W) for kv in range(SWA_KV_HEADS)]
    rows_of = lambda j: slice(j * W, (j + 1) * W)
    lanes_of = lambda p: slice(p * LANES, (p + 1) * LANES)

    def window(cur_ref, prev_ref, j, kv):
        prev = prev_ref[:, lanes_of(kv)] if j == 0 else cur_ref[rows_of(j - 1), lanes_of(kv)]
        return jnp.concatenate([prev, cur_ref[rows_of(j), lanes_of(kv)]], axis=0)

    scores, outs = {}, {}

    def score_stage(j, kv):
        parts = []
        for p in range(kv * pairs_per_kv, (kv + 1) * pairs_per_kv):
            qp = q_ref[rows_of(j), lanes_of(p)]
            parts.append(jnp.where(lo, qp, jnp.zeros_like(qp)))
            parts.append(jnp.where(lo, jnp.zeros_like(qp), qp))
        qs = jnp.concatenate(parts, axis=0)
        scores[j, kv] = lax.dot_general(qs, window(kc_ref, kp_ref, j, kv), nt,
                                        preferred_element_type=F32)

    def softmax_stage(j, kv):
        s_all = scores.pop((j, kv))
        pes, denoms = [], []
        for r in range(SWA_GROUP):
            head = kv * SWA_GROUP + r
            table = jnp.where(first, SWA_HEADS + head, head) if j == 0 else head
            s2 = s_all[r * W:(r + 1) * W]
            s = jnp.where(causal, s2[:, W:], s2[:, :W]) + bias_ref[table]
            sink = sink_ref[layer, head] * LOG2E
            m = jnp.maximum(jnp.max(s, axis=-1, keepdims=True), sink)
            e = jnp.exp2(s - m)
            denoms.append(jnp.sum(e, axis=-1, keepdims=True) + jnp.exp2(sink - m))
            eb = e.astype(BF16)
            zero = jnp.zeros_like(eb)
            pes.append(jnp.concatenate(
                [jnp.where(causal, zero, eb), jnp.where(causal, eb, zero)], axis=1))
        o_all = jnp.dot(jnp.concatenate(pes, axis=0), window(vc_ref, vp_ref, j, kv),
                        preferred_element_type=F32)
        for r in range(SWA_GROUP):
            outs[j, kv * SWA_GROUP + r] = o_all[r * W:(r + 1) * W] / denoms[r]

    def store_stage(j):
        for p in range(SWA_QW // LANES):
            o_ref[rows_of(j), lanes_of(p)] = jnp.where(
                lo, outs.pop((j, 2 * p)), outs.pop((j, 2 * p + 1))).astype(o_ref.dtype)

    score_stage(*units[0])
    for i, (j, kv) in enumerate(units):
        if i + 1 < len(units):
            score_stage(*units[i + 1])
        softmax_stage(j, kv)
        if kv == SWA_KV_HEADS - 1:
            store_stage(j)


def _swa(sinks, sq, sk, sv, bias, layer, batch, seq):
    W = SWA_WINDOW
    tq = min(SWA_TQ, seq)
    nq = seq // tq
    bpt = tq // W
    bps = seq // W
    cur = lambda w: pl.BlockSpec((tq, w), lambda b, i: (b * nq + i, 0))
    prev = pl.BlockSpec((W, 2 * SWA_KW), lambda b, i: (b * bps + jnp.maximum(i * bpt - 1, 0), 0))
    return pl.pallas_call(
        functools.partial(_swa_kernel, layer=layer),
        grid=(batch, nq),
        in_specs=[
            pl.BlockSpec(memory_space=pltpu.SMEM),
            cur(SWA_QW), cur(2 * SWA_KW), prev, cur(2 * SWA_KW), prev,
            pl.BlockSpec(bias.shape, lambda b, i: (0, 0, 0)),
        ],
        out_specs=cur(SWA_QW),
        out_shape=jax.ShapeDtypeStruct((batch * seq, SWA_QW), BF16),
        compiler_params=pltpu.CompilerParams(
            dimension_semantics=("parallel", "parallel"), vmem_limit_bytes=VMEM_LIMIT),
        name="swa",
    )(sinks, sq, sk, sk, sv, sv, bias)


def _swa_bias_tables():
    W = SWA_WINDOW
    i = np.arange(W)[:, None]
    j = np.arange(W)[None, :]
    dist = np.where(j <= i, i - j, W + i - j).astype(np.float32)
    slopes = _alibi_slopes(SWA_HEADS)
    base = (-slopes[:, None, None] * dist[None] * LOG2E).astype(np.float32)
    first = np.where((j <= i)[None], base, -np.inf).astype(np.float32)
    return np.concatenate([base, first], axis=0)


def _load_bf16(w_hbm, w_ref, stage_ref, sem_ref):
    rows = stage_ref.shape[1]
    n_chunks = w_ref.shape[0] // rows

    def chunk_copy(i, slot):
        return pltpu.make_async_copy(
            w_hbm.at[pl.ds(pl.multiple_of(i * rows, rows), rows)], stage_ref.at[slot], sem_ref.at[slot])

    chunk_copy(0, 0).start()

    def body(i, carry):
        slot = i % 2

        @pl.when(i + 1 < n_chunks)
        def _():
            chunk_copy(i + 1, 1 - slot).start()

        chunk_copy(i, slot).wait()
        w_ref[pl.ds(pl.multiple_of(i * rows, rows), rows), :] = stage_ref[slot].astype(BF16)
        return carry

    lax.fori_loop(0, n_chunks, body, 0)


def _ffn_kernel(x_ref, og_ref, os_ref, wo_hbm, g_ref, wu_hbm, cw_ref, wd_hbm, y_ref,
                x1_ref, h_ref, t3_ref, ubuf_ref, gbuf_ref, halo_ref,
                wo_ref, wu_ref, wd_ref, wide_stage_ref, stage_ref, sem_ref, *, layer):
    tm = x_ref.shape[0]
    S = SUBLANES
    nv = tm // S
    nb = nv // S
    ng = D_MODEL // LANES
    first = pl.program_id(1) == 0

    @pl.when((pl.program_id(0) == 0) & first)
    def _():
        _load_bf16(wo_hbm.at[layer], wo_ref, stage_ref, sem_ref)
        _load_bf16(wu_hbm.at[layer], wu_ref, wide_stage_ref, sem_ref)
        _load_bf16(wd_hbm.at[layer], wd_ref, stage_ref, sem_ref)

    x1 = (x_ref[...]
          + jnp.dot(og_ref[...], wo_ref[:GLA_VW, :], preferred_element_type=F32)
          + jnp.dot(os_ref[...], wo_ref[GLA_VW:, :], preferred_element_type=F32))
    x1_ref[...] = x1
    ms = jnp.mean(x1 * x1, axis=-1, keepdims=True)
    hn = x1 * lax.rsqrt(ms + EPS) * g_ref[...]
    for a in range(S):
        for b in range(nb):
            src = slice(S * (nb * a + b), S * (nb * a + b) + S)
            dst = slice(S * (S * b + a), S * (S * b + a) + S)
            for g in range(ng):
                t3_ref[g, dst, :] = hn[src, g * LANES:(g + 1) * LANES]
    for k in range(nv // 2):
        rows = []
        for v in (2 * k, 2 * k + 1):
            b, c = v // S, v % S
            rows.append(jnp.concatenate(
                [t3_ref[g, pl.ds(S * S * b + c, S, stride=S), :] for g in range(ng)], axis=1))
        h_ref[2 * S * k:2 * S * (k + 1), :] = jnp.concatenate(rows, axis=0).astype(BF16)

    sub = lax.broadcasted_iota(jnp.int32, (S, 2 * FFN_FC), 0)

    def up_stage(c):
        slot = c % 2
        h = h_ref[...]
        u = jnp.concatenate([
            jnp.dot(h, wu_ref[:, c * FFN_FC:(c + 1) * FFN_FC], preferred_element_type=F32),
            jnp.dot(h, wu_ref[:, D_FF + c * FFN_FC:D_FF + (c + 1) * FFN_FC],
                    preferred_element_type=F32)], axis=1)
        halo = jnp.where(first, 0.0, halo_ref[c])
        halo_ref[c] = u[tm - 2 * S:, :]
        fix2 = jnp.where(sub == 0, pltpu.roll(halo[:S], 1, axis=0),
                         pltpu.roll(u[tm - 2 * S:tm - S], 1, axis=0))
        fix1 = jnp.where(sub == 0, pltpu.roll(halo[S:], 1, axis=0),
                         pltpu.roll(u[tm - S:], 1, axis=0))
        ubuf_ref[slot, 0:S, :] = fix2
        ubuf_ref[slot, S:2 * S, :] = fix1
        ubuf_ref[slot, 2 * S:2 * S + tm, :] = u

    def gate_stage(c):
        slot = c % 2
        ca = slice(c * FFN_FC, (c + 1) * FFN_FC)
        cb = slice(D_FF + c * FFN_FC, D_FF + (c + 1) * FFN_FC)
        cw = jnp.concatenate([cw_ref[:, ca], cw_ref[:, cb]], axis=1)
        for r in range(0, tm, FFN_RB):
            u0 = ubuf_ref[slot, 2 * S + r:2 * S + r + FFN_RB, :]
            u1 = ubuf_ref[slot, S + r:S + r + FFN_RB, :]
            u2 = ubuf_ref[slot, r:r + FFN_RB, :]
            y = u2 * cw[0:1] + u1 * cw[1:2] + u0 * cw[2:3] + cw[3:4]
            a = y[:, :FFN_FC]
            gate = (a / (1.0 + jnp.exp(-a))) * y[:, FFN_FC:]
            gbuf_ref[r:r + FFN_RB, ca] = gate.astype(BF16)

    up_stage(0)
    for c in range(FFN_NC):
        if c + 1 < FFN_NC:
            up_stage(c + 1)
        gate_stage(c)
    d = jnp.dot(gbuf_ref[...], wd_ref[...], preferred_element_type=F32)
    for g in range(ng):
        t3_ref[g] = d[:, g * LANES:(g + 1) * LANES]

    for a in range(S):
        for b in range(nb):
            rows = slice(S * (nb * a + b), S * (nb * a + b) + S)
            ffn = jnp.concatenate(
                [t3_ref[g, pl.ds(S * S * b + a, S, stride=S), :] for g in range(ng)], axis=1)
            y_ref[rows, :] = x1_ref[rows, :] + ffn


def _out_ffn(x2, o_gla, o_swa, wo, g, wu, cw, wd, layer, batch, seq):
    tm = min(FFN_TM, seq)
    assert tm % (SUBLANES * SUBLANES) == 0
    nt = seq // tm
    row = lambda w: pl.BlockSpec((tm, w), lambda b, t: (b * nt + t, 0))
    lay = lambda a: pl.BlockSpec((None,) + a.shape[1:], lambda b, t: (layer,) + (0,) * (a.ndim - 1),
                                 pipeline_mode=pl.Buffered(1))
    hbm = pl.BlockSpec(memory_space=pl.ANY)
    return pl.pallas_call(
        functools.partial(_ffn_kernel, layer=layer),
        grid=(batch, nt),
        in_specs=[row(D_MODEL), row(GLA_VW), row(SWA_QW), hbm, lay(g), hbm, lay(cw), hbm],
        out_specs=row(D_MODEL),
        out_shape=jax.ShapeDtypeStruct((batch * seq, D_MODEL), F32),
        scratch_shapes=[
            pltpu.VMEM((tm, D_MODEL), F32),
            pltpu.VMEM((tm, D_MODEL), BF16),
            pltpu.VMEM((D_MODEL // LANES, tm, LANES), F32),
            pltpu.VMEM((2, tm + 2 * SUBLANES, 2 * FFN_FC), F32),
            pltpu.VMEM((tm, D_FF), BF16),
            pltpu.VMEM((FFN_NC, 2 * SUBLANES, 2 * FFN_FC), F32),
            pltpu.VMEM((D_MODEL, D_MODEL), BF16),
            pltpu.VMEM((D_MODEL, 2 * D_FF), BF16),
            pltpu.VMEM((D_FF, D_MODEL), BF16),
            pltpu.VMEM((2, W_STAGE_ROWS // 2, 2 * D_FF), F32),
            pltpu.VMEM((2, 2 * W_STAGE_ROWS, D_MODEL), F32),
            pltpu.SemaphoreType.DMA((2,)),
        ],
        compiler_params=pltpu.CompilerParams(
            dimension_semantics=("arbitrary", "arbitrary"), vmem_limit_bytes=VMEM_LIMIT),
        name="out_ffn",
    )(x2, o_gla, o_swa, wo, g, wu, cw, wd)


def kernel(x, mix_norm, w_in, w_alpha2, b_alpha, gla_norm, q_norm, k_norm, sinks, w_out, ffn_norm,
           w_up, conv_w, conv_b, w_down):
    batch, seq, d = x.shape
    depth = w_in.shape[0]
    assert d == D_MODEL and w_in.shape[2] == P_IN
    assert all(seq % min(t, seq) == 0 for t in (GLA_TL, SWA_TQ, FFN_TM))
    assert seq % IN_SUB == 0 and (batch * seq) % min(IN_TM, batch * seq) == 0

    w_in_t = jnp.swapaxes(w_in, 1, 2)
    wa_pad = jnp.concatenate(
        [w_alpha2, jnp.zeros((depth, LANES - GLA_LOWRANK, GLA_QK), w_alpha2.dtype)],
        axis=1).astype(BF16)
    cw = jnp.concatenate([
        conv_w, conv_b[:, None, :],
        jnp.zeros((depth, SUBLANES - CONV_K - 1, 2 * D_FF), conv_w.dtype)], axis=1)
    row3 = lambda a: a.reshape(depth, 1, a.shape[-1])
    qn2 = row3(jnp.tile(q_norm, (1, 2)))
    kn2 = row3(jnp.tile(k_norm, (1, 2)))

    pos = np.arange(GLA_TRI)
    tri = jnp.asarray(
        (pos[:, None] >= pos[None, :]) & (pos[:, None] // GLA_CHUNK == pos[None, :] // GLA_CHUNK),
        dtype=BF16)
    bias = jnp.asarray(_swa_bias_tables())

    x2 = x.reshape(batch * seq, D_MODEL)
    for l in range(depth):
        qe, ke, kd, dec, gv, sr, sq, sk, sv = _in_proj(
            x2, row3(mix_norm), w_in_t, wa_pad, row3(b_alpha), qn2, kn2, tri, l)
        o_gla = _gla(qe, ke, kd, dec, gv, sr, row3(gla_norm), l, batch, seq)
        o_swa = _swa(sinks, sq, sk, sv, bias, l, batch, seq)
        x2 = _out_ffn(x2, o_gla, o_swa, w_out, row3(ffn_norm), w_up, cw, w_down, l, batch, seq)
    return x2.reshape(batch, seq, D_MODEL)
```

```python
import functools

import numpy as np
import jax
import jax.numpy as jnp
from jax import lax
from jax.experimental import pallas as pl
from jax.experimental.pallas import tpu as pltpu

F32 = jnp.float32
BF16 = jnp.bfloat16

D_MODEL = 1024
GLA_HEADS = 4
GLA_DV = 128
GLA_DK = 64
GLA_LOWRANK = 16
GLA_TAU = 16.0
GLA_CHUNK = 64
SWA_HEADS = 8
SWA_KV_HEADS = 2
SWA_HD = 64
SWA_WINDOW = 128
D_FF = 2816
CONV_K = 3
EPS = 1e-6
LOG2E = 1.4426950408889634

GLA_QK = GLA_HEADS * GLA_DK
GLA_VW = GLA_HEADS * GLA_DV
SWA_QW = SWA_HEADS * SWA_HD
SWA_KW = SWA_KV_HEADS * SWA_HD
SWA_GROUP = SWA_HEADS // SWA_KV_HEADS

LANES = 128
SUBLANES = 8
VMEM_LIMIT = 56 * 1024 * 1024

_R_LR = 2 * GLA_QK + 2 * GLA_VW
_R_SQ = _R_LR + GLA_LOWRANK
P_IN = _R_SQ + SWA_QW + 2 * SWA_KW
_C_GQ = 0
_C_GK = _C_GQ + GLA_QK
_C_GV = _C_GK + GLA_QK
_C_GR = _C_GV + GLA_VW
_C_SQ = _C_GR + GLA_VW
_C_SK = _C_SQ + SWA_QW
_C_SV = _C_SK + SWA_KW
_C_LR = _C_SV + SWA_KW
P_PACK = _C_LR + LANES

IN_TM = 2048
W_STAGE_ROWS = 128
IN_SUB = 512
GLA_TRI = 128
ATT_T = 2048
ATT_GLA_STEPS = 3
ATT_SWA_STEPS = 2
FFN_TM = 512
FFN_FC = 256
FFN_NC = D_FF // FFN_FC
FFN_RB = 64


def _alibi_slopes(n_heads):
    return np.array([2.0 ** (-8.0 * (h + 1) / n_heads) for h in range(n_heads)], dtype=np.float32)


def _lane_lo(shape):
    return lax.broadcasted_iota(jnp.int32, shape, len(shape) - 1) % LANES < (LANES // 2)


def _half_rms_inv(v):
    lo = _lane_lo(v.shape)
    sq = v * v
    ss_lo = jnp.sum(jnp.where(lo, sq, 0.0), axis=-1, keepdims=True)
    ss_hi = jnp.sum(jnp.where(lo, 0.0, sq), axis=-1, keepdims=True)
    inv_lo = lax.rsqrt(ss_lo * (1.0 / SWA_HD) + EPS)
    inv_hi = lax.rsqrt(ss_hi * (1.0 / SWA_HD) + EPS)
    return jnp.where(lo, inv_lo, inv_hi)


def _dup_halves(v):
    lo = _lane_lo(v.shape)
    swapped = pltpu.roll(v, LANES // 2, axis=1)
    return jnp.concatenate([jnp.where(lo, v, swapped), jnp.where(lo, swapped, v)], axis=1)


def _load_packed_w_in(wt_hbm, w_ref, stage_ref, sem_ref):
    n_chunks = P_PACK // LANES
    lr_chunk = n_chunks - 1

    def chunk_copy(j):
        slot = j % 2
        if j == lr_chunk:
            return pltpu.make_async_copy(wt_hbm.at[pl.ds(_R_LR, GLA_LOWRANK)],
                                         stage_ref.at[slot, pl.ds(0, GLA_LOWRANK)], sem_ref.at[slot])
        src = j * LANES if j * LANES < _R_LR else j * LANES + GLA_LOWRANK
        return pltpu.make_async_copy(wt_hbm.at[pl.ds(src, LANES)], stage_ref.at[slot], sem_ref.at[slot])

    row = lax.broadcasted_iota(jnp.int32, (LANES, D_MODEL), 0)
    chunk_copy(0).start()
    for j in range(n_chunks):
        if j + 1 < n_chunks:
            chunk_copy(j + 1).start()
        chunk_copy(j).wait()
        t = stage_ref[j % 2]
        if j == lr_chunk:
            t = jnp.where(row < GLA_LOWRANK, t, 0.0)
        w_ref[:, j * LANES:(j + 1) * LANES] = t.T.astype(BF16)


def _inproj_kernel(x_ref, g_ref, w_hbm, wa_ref, ba_ref, qn_ref, kn_ref, tri_ref,
                   qe_ref, ke_ref, kd_ref, dec_ref, gv_ref, sr_ref, sq_ref, sk_ref, sv_ref,
                   w_ref, stage_ref, sem_ref, *, layer):
    tm = x_ref.shape[0]
    C = GLA_CHUNK
    tri = tri_ref[...]
    tb = tri.shape[0]
    qn = qn_ref[...]

    @pl.when(pl.program_id(0) == 0)
    def _():
        _load_packed_w_in(w_hbm.at[layer], w_ref, stage_ref, sem_ref)

    def norm_stage(r0):
        x = x_ref[r0:r0 + IN_SUB, :]
        ms = jnp.mean(x * x, axis=-1, keepdims=True)
        return (x * lax.rsqrt(ms + EPS) * g_ref[...]).astype(BF16)

    def matmul_stage(r0, h):
        rows = slice(r0, r0 + IN_SUB)
        proj = lambda c0, width: jnp.dot(h, w_ref[:, c0:c0 + width], preferred_element_type=F32)
        glr = proj(_C_LR, LANES).astype(BF16)
        gq = proj(_C_GQ, GLA_QK) * (GLA_DK ** -0.5)
        pre = jnp.dot(glr, wa_ref[...], preferred_element_type=F32) + ba_ref[...]
        la = (jnp.minimum(pre, 0.0) - jnp.log(1.0 + jnp.exp(-jnp.abs(pre)))) * (1.0 / GLA_TAU)
        la_hi = la.astype(BF16)
        la_lo = (la - la_hi.astype(F32)).astype(BF16)
        gk = proj(_C_GK, GLA_QK)
        b_blks = [jnp.dot(tri, la_hi[t0:t0 + tb], preferred_element_type=F32)
                  + jnp.dot(tri, la_lo[t0:t0 + tb], preferred_element_type=F32)
                  for t0 in range(0, IN_SUB, tb)]
        q_all = proj(_C_SQ, SWA_QW)
        kv = proj(_C_SK, 2 * SWA_KW)
        r = proj(_C_GR, GLA_VW)
        gv_ref[rows, :] = proj(_C_GV, GLA_VW).astype(BF16)
        return dict(r=r, b_blks=b_blks, gq=gq, gk=gk, q_all=q_all, kv=kv)

    def tail_stage(r0, v):
        rows = slice(r0, r0 + IN_SUB)
        r = v["r"]
        sr_ref[rows, :] = (r / (1.0 + jnp.exp(-r))).astype(sr_ref.dtype)
        for i, t0 in enumerate(range(0, IN_SUB, tb)):
            for c0 in range(0, tb, C):
                crow = slice(r0 + t0 + c0, r0 + t0 + c0 + C)
                lrow = slice(t0 + c0, t0 + c0 + C)
                b = v["b_blks"][i][c0:c0 + C]
                b_last = b[C - 1:C]
                qe_ref[crow, :] = (v["gq"][lrow] * jnp.exp(b)).astype(BF16)
                ke_ref[crow, :] = (v["gk"][lrow] * jnp.exp(-b)).astype(BF16)
                kd_ref[crow, :] = (v["gk"][lrow] * jnp.exp(b_last - b)).astype(BF16)
                ci = (r0 + t0 + c0) // C
                dec_ref[ci:ci + 1, :] = jnp.exp(b_last)
        for p in range(SWA_QW // LANES):
            q = v["q_all"][:, p * LANES:(p + 1) * LANES]
            sq_ref[rows, p * LANES:(p + 1) * LANES] = (
                q * _half_rms_inv(q) * qn * (SWA_HD ** -0.5 * LOG2E)).astype(BF16)
        k = v["kv"][:, :SWA_KW]
        sk_ref[rows, :] = _dup_halves(k * _half_rms_inv(k) * kn_ref[...]).astype(BF16)
        sv_ref[rows, :] = _dup_halves(v["kv"][:, SWA_KW:]).astype(BF16)

    starts = list(range(0, tm, IN_SUB))
    hs = [norm_stage(r0) for r0 in starts]
    vals = matmul_stage(starts[0], hs[0])
    for i, r0 in enumerate(starts):
        nxt = matmul_stage(starts[i + 1], hs[i + 1]) if i + 1 < len(starts) else None
        tail_stage(r0, vals)
        vals = nxt


def _in_proj(x2, g, w_in, wa_pad, ba, qn2, kn2, tri, layer):
    n = x2.shape[0]
    tm = min(IN_TM, n)
    row = lambda w: pl.BlockSpec((tm, w), lambda i: (i, 0))
    lay = lambda a: pl.BlockSpec((None,) + a.shape[1:], lambda i: (layer,) + (0,) * (a.ndim - 1),
                                 pipeline_mode=pl.Buffered(1))
    outs = [
        (GLA_QK, BF16), (GLA_QK, BF16), (GLA_QK, BF16), None,
        (GLA_VW, BF16), (GLA_VW, BF16), (SWA_QW, BF16), (2 * SWA_KW, BF16), (2 * SWA_KW, BF16),
    ]
    dec_spec = pl.BlockSpec((tm // GLA_CHUNK, GLA_QK), lambda i: (i, 0))
    dec_shape = jax.ShapeDtypeStruct((n // GLA_CHUNK, GLA_QK), F32)
    return pl.pallas_call(
        functools.partial(_inproj_kernel, layer=layer),
        grid=(n // tm,),
        in_specs=[row(D_MODEL), lay(g), pl.BlockSpec(memory_space=pl.ANY), lay(wa_pad), lay(ba),
                  lay(qn2), lay(kn2), pl.BlockSpec(tri.shape, lambda i: (0, 0))],
        out_specs=[dec_spec if o is None else row(o[0]) for o in outs],
        out_shape=[dec_shape if o is None else jax.ShapeDtypeStruct((n, o[0]), o[1]) for o in outs],
        scratch_shapes=[
            pltpu.VMEM((D_MODEL, P_PACK), BF16),
            pltpu.VMEM((2, LANES, D_MODEL), F32),
            pltpu.SemaphoreType.DMA((2,)),
        ],
        compiler_params=pltpu.CompilerParams(
            dimension_semantics=("arbitrary",), vmem_limit_bytes=VMEM_LIMIT),
        name="in_proj",
    )(x2, g, w_in, wa_pad, ba, qn2, kn2, tri)


def _gla_gen(qe_ref, ke_ref, kd_ref, dec_ref, v_ref, sr_ref, gn_ref, o_ref, st_ref, sall_ref):
    tl = v_ref.shape[0]
    C = GLA_CHUNK
    nc = tl // C
    npair = GLA_HEADS // 2
    units = [(c, p) for c in range(nc) for p in range(npair)]
    rows_of = lambda c: slice(c * C, (c + 1) * C)
    lanes_of = lambda p: slice(p * LANES, (p + 1) * LANES)
    vcol_of = lambda hd: slice(hd * GLA_DV, (hd + 1) * GLA_DV)
    nt = (((1,), (1,)), ((), ()))
    tn = (((0,), (0,)), ((), ()))

    ri = lax.broadcasted_iota(jnp.int32, (2 * C, 2 * C), 0)
    ci = lax.broadcasted_iota(jnp.int32, (2 * C, 2 * C), 1)
    blockdiag_causal = (ri // C == ci // C) & (ci % C <= ri % C)
    gn = gn_ref[...]
    lo = _lane_lo((C, LANES))

    def split_pair(ref, c, p):
        x = ref[rows_of(c), lanes_of(p)]
        zero = jnp.zeros_like(x)
        return jnp.concatenate([jnp.where(lo, x, zero), jnp.where(lo, zero, x)], axis=0)

    def v_pair(c, p):
        return jnp.concatenate([v_ref[rows_of(c), vcol_of(2 * p)],
                                v_ref[rows_of(c), vcol_of(2 * p + 1)]], axis=0)

    upd = {}
    for c, p in units:
        upd[c, p] = lax.dot_general(v_pair(c, p), split_pair(kd_ref, c, p), tn,
                                    preferred_element_type=F32)
        if p == npair - 1 and c % 4 == 3:
            yield
    for p in range(npair):
        st = st_ref[p]
        for c in range(nc):
            sall_ref[c, p] = st.astype(BF16)
            st = st * dec_ref[c:c + 1, lanes_of(p)] + upd.pop((c, p))
        st_ref[p] = st
        yield

    sc, out = {}, {}

    def score_stage(c):
        for p in range(npair):
            q2 = split_pair(qe_ref, c, p)
            ke = ke_ref[rows_of(c), lanes_of(p)]
            rhs = jnp.concatenate([ke, ke, sall_ref[c, p]], axis=0)
            sc[c, p] = lax.dot_general(q2, rhs, nt, preferred_element_type=F32)

    def value_stage(c):
        for p in range(npair):
            s2 = sc.pop((c, p))
            am = jnp.where(blockdiag_causal, s2[:, :2 * C], 0.0).astype(BF16)
            out[c, p] = jnp.dot(am, v_pair(c, p), preferred_element_type=F32) + s2[:, 2 * C:]

    def norm_stage(c):
        for p in range(npair):
            o = out.pop((c, p))
            o = o * lax.rsqrt(jnp.mean(o * o, axis=-1, keepdims=True) + EPS) * gn
            for half in range(2):
                vcol = vcol_of(2 * p + half)
                o_ref[rows_of(c), vcol] = (
                    o[half * C:(half + 1) * C] * sr_ref[rows_of(c), vcol]).astype(o_ref.dtype)

    for c in range(nc + 2):
        if c < nc:
            score_stage(c)
        if 0 <= c - 1 < nc:
            value_stage(c - 1)
        if 0 <= c - 2 < nc:
            norm_stage(c - 2)
        yield


def _swa_gen(sink_ref, q_ref, kc_ref, kp_ref, vc_ref, vp_ref, bias_ref, o_ref, first, layer):
    W = SWA_WINDOW
    tq = q_ref.shape[0]
    lo = _lane_lo((W, LANES))
    causal = (lax.broadcasted_iota(jnp.int32, (W, W), 1)
              <= lax.broadcasted_iota(jnp.int32, (W, W), 0))
    nt = (((1,), (1,)), ((), ()))
    pairs_per_kv = SWA_GROUP // 2
    units = [(j, kv) for j in range(tq // W) for kv in range(SWA_KV_HEADS)]
    rows_of = lambda j: slice(j * W, (j + 1) * W)
    lanes_of = lambda p: slice(p * LANES, (p + 1) * LANES)

    def window(cur_ref, prev_ref, j, kv):
        prev = prev_ref[:, lanes_of(kv)] if j == 0 else cur_ref[rows_of(j - 1), lanes_of(kv)]
        return jnp.concatenate([prev, cur_ref[rows_of(j), lanes_of(kv)]], axis=0)

    scores, outs = {}, {}

    def score_stage(j, kv):
        parts = []
        for p in range(kv * pairs_per_kv, (kv + 1) * pairs_per_kv):
            qp = q_ref[rows_of(j), lanes_of(p)]
            parts.append(jnp.where(lo, qp, jnp.zeros_like(qp)))
            parts.append(jnp.where(lo, jnp.zeros_like(qp), qp))
        qs = jnp.concatenate(parts, axis=0)
        scores[j, kv] = lax.dot_general(qs, window(kc_ref, kp_ref, j, kv), nt,
                                        preferred_element_type=F32)

    def softmax_stage(j, kv):
        s_all = scores.pop((j, kv))
        pes, denoms = [], []
        for r in range(SWA_GROUP):
            head = kv * SWA_GROUP + r
            table = jnp.where(first, SWA_HEADS + head, head) if j == 0 else head
            s2 = s_all[r * W:(r + 1) * W]
            s = jnp.where(causal, s2[:, W:], s2[:, :W]) + bias_ref[table]
            sink = sink_ref[layer, head] * LOG2E
            m = jnp.maximum(jnp.max(s, axis=-1, keepdims=True), sink)
            e = jnp.exp2(s - m)
            denoms.append(jnp.sum(e, axis=-1, keepdims=True) + jnp.exp2(sink - m))
            eb = e.astype(BF16)
            zero = jnp.zeros_like(eb)
            pes.append(jnp.concatenate(
                [jnp.where(causal, zero, eb), jnp.where(causal, eb, zero)], axis=1))
        o_all = jnp.dot(jnp.concatenate(pes, axis=0), window(vc_ref, vp_ref, j, kv),
                        preferred_element_type=F32)
        for r in range(SWA_GROUP):
            outs[j, kv * SWA_GROUP + r] = o_all[r * W:(r + 1) * W] / denoms[r]

    def store_stage(j):
        for p in range(SWA_QW // LANES):
            o_ref[rows_of(j), lanes_of(p)] = jnp.where(
                lo, outs.pop((j, 2 * p)), outs.pop((j, 2 * p + 1))).astype(o_ref.dtype)

    score_stage(*units[0])
    for i, (j, kv) in enumerate(units):
        if i + 1 < len(units):
            score_stage(*units[i + 1])
        softmax_stage(j, kv)
        if kv == SWA_KV_HEADS - 1:
            store_stage(j)
        yield


def _attn_kernel(sink_ref, qe_ref, ke_ref, kd_ref, dec_ref, v_ref, sr_ref, gn_ref,
                 q_ref, kc_ref, kp_ref, vc_ref, vp_ref, bias_ref, og_ref, os_ref,
                 st_ref, sall_ref, *, layer):
    first = pl.program_id(1) == 0

    @pl.when(first)
    def _():
        st_ref[...] = jnp.zeros_like(st_ref)

    gens = [(_gla_gen(qe_ref, ke_ref, kd_ref, dec_ref, v_ref, sr_ref, gn_ref, og_ref, st_ref, sall_ref),
             ATT_GLA_STEPS),
            (_swa_gen(sink_ref, q_ref, kc_ref, kp_ref, vc_ref, vp_ref, bias_ref, os_ref, first, layer),
             ATT_SWA_STEPS)]
    while gens:
        for item in list(gens):
            g, n = item
            for _ in range(n):
                if next(g, StopIteration) is StopIteration:
                    gens.remove(item)
                    break


def _attn(sinks, qe, ke, kd, dec, gv, sr, gn, sq, sk, sv, bias, layer, batch, seq):
    W = SWA_WINDOW
    tl = min(ATT_T, seq)
    nt = seq // tl
    bpt = tl // W
    bps = seq // W
    row = lambda w: pl.BlockSpec((tl, w), lambda b, t: (b * nt + t, 0))
    prev = pl.BlockSpec((W, 2 * SWA_KW), lambda b, t: (b * bps + jnp.maximum(t * bpt - 1, 0), 0))
    return pl.pallas_call(
        functools.partial(_attn_kernel, layer=layer),
        grid=(batch, nt),
        in_specs=[pl.BlockSpec(memory_space=pltpu.SMEM),
                  row(GLA_QK), row(GLA_QK), row(GLA_QK),
                  pl.BlockSpec((tl // GLA_CHUNK, GLA_QK), lambda b, t: (b * nt + t, 0)),
                  row(GLA_VW), row(GLA_VW),
                  pl.BlockSpec((None, 1, GLA_DV), lambda b, t: (layer, 0, 0)),
                  row(SWA_QW), row(2 * SWA_KW), prev, row(2 * SWA_KW), prev,
                  pl.BlockSpec(bias.shape, lambda b, t: (0, 0, 0))],
        out_specs=[row(GLA_VW), row(SWA_QW)],
        out_shape=[jax.ShapeDtypeStruct((batch * seq, GLA_VW), BF16),
                   jax.ShapeDtypeStruct((batch * seq, SWA_QW), BF16)],
        scratch_shapes=[
            pltpu.VMEM((GLA_HEADS // 2, GLA_DV, LANES), F32),
            pltpu.VMEM((tl // GLA_CHUNK, GLA_HEADS // 2, GLA_DV, LANES), BF16),
        ],
        compiler_params=pltpu.CompilerParams(
            dimension_semantics=("parallel", "arbitrary"), vmem_limit_bytes=VMEM_LIMIT),
        name="attn",
    )(sinks, qe, ke, kd, dec, gv, sr, gn, sq, sk, sk, sv, sv, bias)


def _swa_bias_tables():
    W = SWA_WINDOW
    i = np.arange(W)[:, None]
    j = np.arange(W)[None, :]
    dist = np.where(j <= i, i - j, W + i - j).astype(np.float32)
    slopes = _alibi_slopes(SWA_HEADS)
    base = (-slopes[:, None, None] * dist[None] * LOG2E).astype(np.float32)
    first = np.where((j <= i)[None], base, -np.inf).astype(np.float32)
    return np.concatenate([base, first], axis=0)


def _load_bf16(w_hbm, w_ref, stage_ref, sem_ref):
    rows = stage_ref.shape[1]
    n_chunks = w_ref.shape[0] // rows

    def chunk_copy(i, slot):
        return pltpu.make_async_copy(
            w_hbm.at[pl.ds(pl.multiple_of(i * rows, rows), rows)], stage_ref.at[slot], sem_ref.at[slot])

    chunk_copy(0, 0).start()

    def body(i, carry):
        slot = i % 2

        @pl.when(i + 1 < n_chunks)
        def _():
            chunk_copy(i + 1, 1 - slot).start()

        chunk_copy(i, slot).wait()
        w_ref[pl.ds(pl.multiple_of(i * rows, rows), rows), :] = stage_ref[slot].astype(BF16)
        return carry

    lax.fori_loop(0, n_chunks, body, 0)


def _ffn_kernel(x_ref, og_ref, os_ref, wo_hbm, g_ref, wu_hbm, cw_ref, wd_hbm, y_ref,
                x1_ref, h_ref, t3_ref, ubuf_ref, gbuf_ref, halo_ref,
                wo_ref, wu_ref, wd_ref, wide_stage_ref, stage_ref, sem_ref, *, layer):
    tm = x_ref.shape[0]
    S = SUBLANES
    nv = tm // S
    nb = nv // S
    ng = D_MODEL // LANES
    first = pl.program_id(1) == 0

    @pl.when((pl.program_id(0) == 0) & first)
    def _():
        _load_bf16(wo_hbm.at[layer], wo_ref, stage_ref, sem_ref)
        _load_bf16(wu_hbm.at[layer], wu_ref, wide_stage_ref, sem_ref)
        _load_bf16(wd_hbm.at[layer], wd_ref, stage_ref, sem_ref)

    x1 = (x_ref[...]
          + jnp.dot(og_ref[...], wo_ref[:GLA_VW, :], preferred_element_type=F32)
          + jnp.dot(os_ref[...], wo_ref[GLA_VW:, :], preferred_element_type=F32))
    x1_ref[...] = x1
    ms = jnp.mean(x1 * x1, axis=-1, keepdims=True)
    hn = x1 * lax.rsqrt(ms + EPS) * g_ref[...]
    for a in range(S):
        for b in range(nb):
            src = slice(S * (nb * a + b), S * (nb * a + b) + S)
            dst = slice(S * (S * b + a), S * (S * b + a) + S)
            for g in range(ng):
                t3_ref[g, dst, :] = hn[src, g * LANES:(g + 1) * LANES]
    for k in range(nv // 2):
        rows = []
        for v in (2 * k, 2 * k + 1):
            b, c = v // S, v % S
            rows.append(jnp.concatenate(
                [t3_ref[g, pl.ds(S * S * b + c, S, stride=S), :] for g in range(ng)], axis=1))
        h_ref[2 * S * k:2 * S * (k + 1), :] = jnp.concatenate(rows, axis=0).astype(BF16)

    sub = lax.broadcasted_iota(jnp.int32, (S, 2 * FFN_FC), 0)

    def up_stage(c):
        slot = c % 2
        h = h_ref[...]
        u = jnp.concatenate([
            jnp.dot(h, wu_ref[:, c * FFN_FC:(c + 1) * FFN_FC], preferred_element_type=F32),
            jnp.dot(h, wu_ref[:, D_FF + c * FFN_FC:D_FF + (c + 1) * FFN_FC],
                    preferred_element_type=F32)], axis=1)
        halo = jnp.where(first, 0.0, halo_ref[c])
        halo_ref[c] = u[tm - 2 * S:, :]
        fix2 = jnp.where(sub == 0, pltpu.roll(halo[:S], 1, axis=0),
                         pltpu.roll(u[tm - 2 * S:tm - S], 1, axis=0))
        fix1 = jnp.where(sub == 0, pltpu.roll(halo[S:], 1, axis=0),
                         pltpu.roll(u[tm - S:], 1, axis=0))
        ubuf_ref[slot, 0:S, :] = fix2
        ubuf_ref[slot, S:2 * S, :] = fix1
        ubuf_ref[slot, 2 * S:2 * S + tm, :] = u

    def gate_stage(c):
        slot = c % 2
        ca = slice(c * FFN_FC, (c + 1) * FFN_FC)
        cb = slice(D_FF + c * FFN_FC, D_FF + (c + 1) * FFN_FC)
        cw = jnp.concatenate([cw_ref[:, ca], cw_ref[:, cb]], axis=1)
        for r in range(0, tm, FFN_RB):
            u0 = ubuf_ref[slot, 2 * S + r:2 * S + r + FFN_RB, :]
            u1 = ubuf_ref[slot, S + r:S + r + FFN_RB, :]
            u2 = ubuf_ref[slot, r:r + FFN_RB, :]
            y = u2 * cw[0:1] + u1 * cw[1:2] + u0 * cw[2:3] + cw[3:4]
            a = y[:, :FFN_FC]
            gate = (a / (1.0 + jnp.exp(-a))) * y[:, FFN_FC:]
            gbuf_ref[r:r + FFN_RB, ca] = gate.astype(BF16)

    up_stage(0)
    for c in range(FFN_NC):
        if c + 1 < FFN_NC:
            up_stage(c + 1)
        gate_stage(c)
    d = jnp.dot(gbuf_ref[...], wd_ref[...], preferred_element_type=F32)
    for g in range(ng):
        t3_ref[g] = d[:, g * LANES:(g + 1) * LANES]

    for a in range(S):
        for b in range(nb):
            rows = slice(S * (nb * a + b), S * (nb * a + b) + S)
            ffn = jnp.concatenate(
                [t3_ref[g, pl.ds(S * S * b + a, S, stride=S), :] for g in range(ng)], axis=1)
            y_ref[rows, :] = x1_ref[rows, :] + ffn


def _out_ffn(x2, o_gla, o_swa, wo, g, wu, cw, wd, layer, batch, seq):
    tm = min(FFN_TM, seq)
    assert tm % (SUBLANES * SUBLANES) == 0
    nt = seq // tm
    row = lambda w: pl.BlockSpec((tm, w), lambda b, t: (b * nt + t, 0))
    lay = lambda a: pl.BlockSpec((None,) + a.shape[1:], lambda b, t: (layer,) + (0,) * (a.ndim - 1),
                                 pipeline_mode=pl.Buffered(1))
    hbm = pl.BlockSpec(memory_space=pl.ANY)
    return pl.pallas_call(
        functools.partial(_ffn_kernel, layer=layer),
        grid=(batch, nt),
        in_specs=[row(D_MODEL), row(GLA_VW), row(SWA_QW), hbm, lay(g), hbm, lay(cw), hbm],
        out_specs=row(D_MODEL),
        out_shape=jax.ShapeDtypeStruct((batch * seq, D_MODEL), F32),
        scratch_shapes=[
            pltpu.VMEM((tm, D_MODEL), F32),
            pltpu.VMEM((tm, D_MODEL), BF16),
            pltpu.VMEM((D_MODEL // LANES, tm, LANES), F32),
            pltpu.VMEM((2, tm + 2 * SUBLANES, 2 * FFN_FC), F32),
            pltpu.VMEM((tm, D_FF), BF16),
            pltpu.VMEM((FFN_NC, 2 * SUBLANES, 2 * FFN_FC), F32),
            pltpu.VMEM((D_MODEL, D_MODEL), BF16),
            pltpu.VMEM((D_MODEL, 2 * D_FF), BF16),
            pltpu.VMEM((D_FF, D_MODEL), BF16),
            pltpu.VMEM((2, W_STAGE_ROWS // 2, 2 * D_FF), F32),
            pltpu.VMEM((2, 2 * W_STAGE_ROWS, D_MODEL), F32),
            pltpu.SemaphoreType.DMA((2,)),
        ],
        compiler_params=pltpu.CompilerParams(
            dimension_semantics=("arbitrary", "arbitrary"), vmem_limit_bytes=VMEM_LIMIT),
        name="out_ffn",
    )(x2, o_gla, o_swa, wo, g, wu, cw, wd)


def kernel(x, mix_norm, w_in, w_alpha2, b_alpha, gla_norm, q_norm, k_norm, sinks, w_out, ffn_norm,
           w_up, conv_w, conv_b, w_down):
    batch, seq, d = x.shape
    depth = w_in.shape[0]
    assert d == D_MODEL and w_in.shape[2] == P_IN
    assert all(seq % min(t, seq) == 0 for t in (ATT_T, FFN_TM))
    assert seq % IN_SUB == 0 and (batch * seq) % min(IN_TM, batch * seq) == 0

    w_in_t = jnp.swapaxes(w_in, 1, 2)
    wa_pad = jnp.concatenate(
        [w_alpha2, jnp.zeros((depth, LANES - GLA_LOWRANK, GLA_QK), w_alpha2.dtype)],
        axis=1).astype(BF16)
    cw = jnp.concatenate([
        conv_w, conv_b[:, None, :],
        jnp.zeros((depth, SUBLANES - CONV_K - 1, 2 * D_FF), conv_w.dtype)], axis=1)
    row3 = lambda a: a.reshape(depth, 1, a.shape[-1])
    qn2 = row3(jnp.tile(q_norm, (1, 2)))
    kn2 = row3(jnp.tile(k_norm, (1, 2)))

    pos = np.arange(GLA_TRI)
    tri = jnp.asarray(
        (pos[:, None] >= pos[None, :]) & (pos[:, None] // GLA_CHUNK == pos[None, :] // GLA_CHUNK),
        dtype=BF16)
    bias = jnp.asarray(_swa_bias_tables())

    x2 = x.reshape(batch * seq, D_MODEL)
    for l in range(depth):
        qe, ke, kd, dec, gv, sr, sq, sk, sv = _in_proj(
            x2, row3(mix_norm), w_in_t, wa_pad, row3(b_alpha), qn2, kn2, tri, l)
        o_gla, o_swa = _attn(sinks, qe, ke, kd, dec, gv, sr, row3(gla_norm), sq, sk, sv, bias,
                             l, batch, seq)
        x2 = _out_ffn(x2, o_gla, o_swa, w_out, row3(ffn_norm), w_up, cw, w_down, l, batch, seq)
    return x2.reshape(batch, seq, D_MODEL)
```

```python
import functools

import numpy as np
import jax
import jax.numpy as jnp
from jax import lax
from jax.experimental import pallas as pl
from jax.experimental.pallas import tpu as pltpu

F32 = jnp.float32
BF16 = jnp.bfloat16

D_MODEL = 1024
GLA_HEADS = 4
GLA_DV = 128
GLA_DK = 64
GLA_LOWRANK = 16
GLA_TAU = 16.0
GLA_CHUNK = 64
SWA_HEADS = 8
SWA_KV_HEADS = 2
SWA_HD = 64
SWA_WINDOW = 128
D_FF = 2816
CONV_K = 3
EPS = 1e-6
LOG2E = 1.4426950408889634

GLA_QK = GLA_HEADS * GLA_DK
GLA_VW = GLA_HEADS * GLA_DV
SWA_QW = SWA_HEADS * SWA_HD
SWA_KW = SWA_KV_HEADS * SWA_HD
SWA_GROUP = SWA_HEADS // SWA_KV_HEADS

LANES = 128
SUBLANES = 8
VMEM_LIMIT = 56 * 1024 * 1024

_R_LR = 2 * GLA_QK + 2 * GLA_VW
_R_SQ = _R_LR + GLA_LOWRANK
P_IN = _R_SQ + SWA_QW + 2 * SWA_KW
_C_GQ = 0
_C_GK = _C_GQ + GLA_QK
_C_GV = _C_GK + GLA_QK
_C_GR = _C_GV + GLA_VW
_C_SQ = _C_GR + GLA_VW
_C_SK = _C_SQ + SWA_QW
_C_SV = _C_SK + SWA_KW
_C_LR = _C_SV + SWA_KW
P_PACK = _C_LR + LANES

IN_TM = 2048
W_STAGE_ROWS = 128
IN_SUB = 512
GLA_TRI = 128
ATT_T = 2048
ATT_GLA_STEPS = 1
ATT_SWA_STEPS = 1
FFN_TM = 512
FFN_FC = 256
FFN_NC = D_FF // FFN_FC
FFN_RB = 64


def _alibi_slopes(n_heads):
    return np.array([2.0 ** (-8.0 * (h + 1) / n_heads) for h in range(n_heads)], dtype=np.float32)


def _lane_lo(shape):
    return lax.broadcasted_iota(jnp.int32, shape, len(shape) - 1) % LANES < (LANES // 2)


def _half_rms_inv(v):
    lo = _lane_lo(v.shape)
    sq = v * v
    ss_lo = jnp.sum(jnp.where(lo, sq, 0.0), axis=-1, keepdims=True)
    ss_hi = jnp.sum(jnp.where(lo, 0.0, sq), axis=-1, keepdims=True)
    inv_lo = lax.rsqrt(ss_lo * (1.0 / SWA_HD) + EPS)
    inv_hi = lax.rsqrt(ss_hi * (1.0 / SWA_HD) + EPS)
    return jnp.where(lo, inv_lo, inv_hi)


def _dup_halves(v):
    lo = _lane_lo(v.shape)
    swapped = pltpu.roll(v, LANES // 2, axis=1)
    return jnp.concatenate([jnp.where(lo, v, swapped), jnp.where(lo, swapped, v)], axis=1)


def _load_packed_w_in(wt_hbm, w_ref, stage_ref, sem_ref):
    n_chunks = P_PACK // LANES
    lr_chunk = n_chunks - 1

    def chunk_copy(j):
        slot = j % 2
        if j == lr_chunk:
            return pltpu.make_async_copy(wt_hbm.at[pl.ds(_R_LR, GLA_LOWRANK)],
                                         stage_ref.at[slot, pl.ds(0, GLA_LOWRANK)], sem_ref.at[slot])
        src = j * LANES if j * LANES < _R_LR else j * LANES + GLA_LOWRANK
        return pltpu.make_async_copy(wt_hbm.at[pl.ds(src, LANES)], stage_ref.at[slot], sem_ref.at[slot])

    row = lax.broadcasted_iota(jnp.int32, (LANES, D_MODEL), 0)
    chunk_copy(0).start()
    for j in range(n_chunks):
        if j + 1 < n_chunks:
            chunk_copy(j + 1).start()
        chunk_copy(j).wait()
        t = stage_ref[j % 2]
        if j == lr_chunk:
            t = jnp.where(row < GLA_LOWRANK, t, 0.0)
        w_ref[:, j * LANES:(j + 1) * LANES] = t.T.astype(BF16)


def _inproj_kernel(x_ref, g_ref, w_hbm, wa_ref, ba_ref, qn_ref, kn_ref, tri_ref,
                   qe_ref, ke_ref, kd_ref, dec_ref, gv_ref, sr_ref, sq_ref, sk_ref, sv_ref,
                   w_ref, stage_ref, sem_ref, *, layer):
    tm = x_ref.shape[0]
    C = GLA_CHUNK
    tri = tri_ref[...]
    tb = tri.shape[0]
    qn = qn_ref[...]

    @pl.when(pl.program_id(0) == 0)
    def _():
        _load_packed_w_in(w_hbm.at[layer], w_ref, stage_ref, sem_ref)

    def norm_stage(r0):
        x = x_ref[r0:r0 + IN_SUB, :]
        ms = jnp.mean(x * x, axis=-1, keepdims=True)
        return (x * lax.rsqrt(ms + EPS) * g_ref[...]).astype(BF16)

    def matmul_stage(r0, h):
        rows = slice(r0, r0 + IN_SUB)
        proj = lambda c0, width: jnp.dot(h, w_ref[:, c0:c0 + width], preferred_element_type=F32)
        glr = proj(_C_LR, LANES).astype(BF16)
        gq = proj(_C_GQ, GLA_QK) * (GLA_DK ** -0.5)
        pre = jnp.dot(glr, wa_ref[...], preferred_element_type=F32) + ba_ref[...]
        la = (jnp.minimum(pre, 0.0) - jnp.log(1.0 + jnp.exp(-jnp.abs(pre)))) * (1.0 / GLA_TAU)
        la_hi = la.astype(BF16)
        la_lo = (la - la_hi.astype(F32)).astype(BF16)
        gk = proj(_C_GK, GLA_QK)
        b_blks = [jnp.dot(tri, la_hi[t0:t0 + tb], preferred_element_type=F32)
                  + jnp.dot(tri, la_lo[t0:t0 + tb], preferred_element_type=F32)
                  for t0 in range(0, IN_SUB, tb)]
        q_all = proj(_C_SQ, SWA_QW)
        kv = proj(_C_SK, 2 * SWA_KW)
        r = proj(_C_GR, GLA_VW)
        gv_ref[rows, :] = proj(_C_GV, GLA_VW).astype(BF16)
        return dict(r=r, b_blks=b_blks, gq=gq, gk=gk, q_all=q_all, kv=kv)

    def tail_stage(r0, v):
        rows = slice(r0, r0 + IN_SUB)
        r = v["r"]
        sr_ref[rows, :] = (r / (1.0 + jnp.exp(-r))).astype(sr_ref.dtype)
        for i, t0 in enumerate(range(0, IN_SUB, tb)):
            for c0 in range(0, tb, C):
                crow = slice(r0 + t0 + c0, r0 + t0 + c0 + C)
                lrow = slice(t0 + c0, t0 + c0 + C)
                b = v["b_blks"][i][c0:c0 + C]
                b_last = b[C - 1:C]
                qe_ref[crow, :] = (v["gq"][lrow] * jnp.exp(b)).astype(BF16)
                ke_ref[crow, :] = (v["gk"][lrow] * jnp.exp(-b)).astype(BF16)
                kd_ref[crow, :] = (v["gk"][lrow] * jnp.exp(b_last - b)).astype(BF16)
                ci = (r0 + t0 + c0) // C
                dec_ref[ci:ci + 1, :] = jnp.exp(b_last)
        for p in range(SWA_QW // LANES):
            q = v["q_all"][:, p * LANES:(p + 1) * LANES]
            sq_ref[rows, p * LANES:(p + 1) * LANES] = (
                q * _half_rms_inv(q) * qn * (SWA_HD ** -0.5 * LOG2E)).astype(BF16)
        k = v["kv"][:, :SWA_KW]
        sk_ref[rows, :] = _dup_halves(k * _half_rms_inv(k) * kn_ref[...]).astype(BF16)
        sv_ref[rows, :] = _dup_halves(v["kv"][:, SWA_KW:]).astype(BF16)

    starts = list(range(0, tm, IN_SUB))
    hs = [norm_stage(r0) for r0 in starts]
    vals = matmul_stage(starts[0], hs[0])
    for i, r0 in enumerate(starts):
        nxt = matmul_stage(starts[i + 1], hs[i + 1]) if i + 1 < len(starts) else None
        tail_stage(r0, vals)
        vals = nxt


def _in_proj(x2, g, w_in, wa_pad, ba, qn2, kn2, tri, layer):
    n = x2.shape[0]
    tm = min(IN_TM, n)
    row = lambda w: pl.BlockSpec((tm, w), lambda i: (i, 0))
    lay = lambda a: pl.BlockSpec((None,) + a.shape[1:], lambda i: (layer,) + (0,) * (a.ndim - 1),
                                 pipeline_mode=pl.Buffered(1))
    outs = [
        (GLA_QK, BF16), (GLA_QK, BF16), (GLA_QK, BF16), None,
        (GLA_VW, BF16), (GLA_VW, BF16), (SWA_QW, BF16), (2 * SWA_KW, BF16), (2 * SWA_KW, BF16),
    ]
    dec_spec = pl.BlockSpec((tm // GLA_CHUNK, GLA_QK), lambda i: (i, 0))
    dec_shape = jax.ShapeDtypeStruct((n // GLA_CHUNK, GLA_QK), F32)
    return pl.pallas_call(
        functools.partial(_inproj_kernel, layer=layer),
        grid=(n // tm,),
        in_specs=[row(D_MODEL), lay(g), pl.BlockSpec(memory_space=pl.ANY), lay(wa_pad), lay(ba),
                  lay(qn2), lay(kn2), pl.BlockSpec(tri.shape, lambda i: (0, 0))],
        out_specs=[dec_spec if o is None else row(o[0]) for o in outs],
        out_shape=[dec_shape if o is None else jax.ShapeDtypeStruct((n, o[0]), o[1]) for o in outs],
        scratch_shapes=[
            pltpu.VMEM((D_MODEL, P_PACK), BF16),
            pltpu.VMEM((2, LANES, D_MODEL), F32),
            pltpu.SemaphoreType.DMA((2,)),
        ],
        compiler_params=pltpu.CompilerParams(
            dimension_semantics=("arbitrary",), vmem_limit_bytes=VMEM_LIMIT),
        name="in_proj",
    )(x2, g, w_in, wa_pad, ba, qn2, kn2, tri)


def _gla_gen(qe_ref, ke_ref, kd_ref, dec_ref, v_ref, sr_ref, gn_ref, o_ref, st_ref, sall_ref):
    tl = v_ref.shape[0]
    C = GLA_CHUNK
    nc = tl // C
    npair = GLA_HEADS // 2
    units = [(c, p) for c in range(nc) for p in range(npair)]
    rows_of = lambda c: slice(c * C, (c + 1) * C)
    lanes_of = lambda p: slice(p * LANES, (p + 1) * LANES)
    vcol_of = lambda hd: slice(hd * GLA_DV, (hd + 1) * GLA_DV)
    nt = (((1,), (1,)), ((), ()))
    tn = (((0,), (0,)), ((), ()))

    ri = lax.broadcasted_iota(jnp.int32, (2 * C, 2 * C), 0)
    ci = lax.broadcasted_iota(jnp.int32, (2 * C, 2 * C), 1)
    blockdiag_causal = (ri // C == ci // C) & (ci % C <= ri % C)
    gn = gn_ref[...]
    lo = _lane_lo((C, LANES))

    def split_pair(ref, c, p):
        x = ref[rows_of(c), lanes_of(p)]
        zero = jnp.zeros_like(x)
        return jnp.concatenate([jnp.where(lo, x, zero), jnp.where(lo, zero, x)], axis=0)

    def v_pair(c, p):
        return jnp.concatenate([v_ref[rows_of(c), vcol_of(2 * p)],
                                v_ref[rows_of(c), vcol_of(2 * p + 1)]], axis=0)

    upd = {}
    for c, p in units:
        upd[c, p] = lax.dot_general(v_pair(c, p), split_pair(kd_ref, c, p), tn,
                                    preferred_element_type=F32)
        if p == npair - 1 and c % 4 == 3:
            yield
    for p in range(npair):
        st = st_ref[p]
        for c in range(nc):
            sall_ref[c, p] = st.astype(BF16)
            st = st * dec_ref[c:c + 1, lanes_of(p)] + upd.pop((c, p))
        st_ref[p] = st
        yield

    sc, out = {}, {}

    def score_stage(c):
        for p in range(npair):
            q2 = split_pair(qe_ref, c, p)
            ke = ke_ref[rows_of(c), lanes_of(p)]
            rhs = jnp.concatenate([ke, ke, sall_ref[c, p]], axis=0)
            sc[c, p] = lax.dot_general(q2, rhs, nt, preferred_element_type=F32)

    def value_stage(c):
        for p in range(npair):
            s2 = sc.pop((c, p))
            am = jnp.where(blockdiag_causal, s2[:, :2 * C], 0.0).astype(BF16)
            out[c, p] = jnp.dot(am, v_pair(c, p), preferred_element_type=F32) + s2[:, 2 * C:]

    def norm_stage(c):
        for p in range(npair):
            o = out.pop((c, p))
            o = o * lax.rsqrt(jnp.mean(o * o, axis=-1, keepdims=True) + EPS) * gn
            for half in range(2):
                vcol = vcol_of(2 * p + half)
                o_ref[rows_of(c), vcol] = (
                    o[half * C:(half + 1) * C] * sr_ref[rows_of(c), vcol]).astype(o_ref.dtype)

    for c in range(nc + 2):
        if c < nc:
            score_stage(c)
        if 0 <= c - 1 < nc:
            value_stage(c - 1)
        if 0 <= c - 2 < nc:
            norm_stage(c - 2)
        yield


def _swa_gen(sink_ref, q_ref, kc_ref, kp_ref, vc_ref, vp_ref, bias_ref, o_ref, first, layer):
    W = SWA_WINDOW
    tq = q_ref.shape[0]
    lo = _lane_lo((W, LANES))
    causal = (lax.broadcasted_iota(jnp.int32, (W, W), 1)
              <= lax.broadcasted_iota(jnp.int32, (W, W), 0))
    nt = (((1,), (1,)), ((), ()))
    pairs_per_kv = SWA_GROUP // 2
    units = [(j, kv) for j in range(tq // W) for kv in range(SWA_KV_HEADS)]
    rows_of = lambda j: slice(j * W, (j + 1) * W)
    lanes_of = lambda p: slice(p * LANES, (p + 1) * LANES)

    def window(cur_ref, prev_ref, j, kv):
        prev = prev_ref[:, lanes_of(kv)] if j == 0 else cur_ref[rows_of(j - 1), lanes_of(kv)]
        return jnp.concatenate([prev, cur_ref[rows_of(j), lanes_of(kv)]], axis=0)

    scores, outs = {}, {}

    def score_stage(j, kv):
        parts = []
        for p in range(kv * pairs_per_kv, (kv + 1) * pairs_per_kv):
            qp = q_ref[rows_of(j), lanes_of(p)]
            parts.append(jnp.where(lo, qp, jnp.zeros_like(qp)))
            parts.append(jnp.where(lo, jnp.zeros_like(qp), qp))
        qs = jnp.concatenate(parts, axis=0)
        scores[j, kv] = lax.dot_general(qs, window(kc_ref, kp_ref, j, kv), nt,
                                        preferred_element_type=F32)

    def softmax_stage(j, kv):
        s_all = scores.pop((j, kv))
        pes, denoms = [], []
        for r in range(SWA_GROUP):
            head = kv * SWA_GROUP + r
            table = jnp.where(first, SWA_HEADS + head, head) if j == 0 else head
            s2 = s_all[r * W:(r + 1) * W]
            s = jnp.where(causal, s2[:, W:], s2[:, :W]) + bias_ref[table]
            sink = sink_ref[layer, head] * LOG2E
            m = jnp.maximum(jnp.max(s, axis=-1, keepdims=True), sink)
            e = jnp.exp2(s - m)
            denoms.append(jnp.sum(e, axis=-1, keepdims=True) + jnp.exp2(sink - m))
            eb = e.astype(BF16)
            zero = jnp.zeros_like(eb)
            pes.append(jnp.concatenate(
                [jnp.where(causal, zero, eb), jnp.where(causal, eb, zero)], axis=1))
        o_all = jnp.dot(jnp.concatenate(pes, axis=0), window(vc_ref, vp_ref, j, kv),
                        preferred_element_type=F32)
        for r in range(SWA_GROUP):
            outs[j, kv * SWA_GROUP + r] = o_all[r * W:(r + 1) * W] / denoms[r]

    def store_stage(j):
        for p in range(SWA_QW // LANES):
            o_ref[rows_of(j), lanes_of(p)] = jnp.where(
                lo, outs.pop((j, 2 * p)), outs.pop((j, 2 * p + 1))).astype(o_ref.dtype)

    score_stage(*units[0])
    for i, (j, kv) in enumerate(units):
        if i + 1 < len(units):
            score_stage(*units[i + 1])
        softmax_stage(j, kv)
        if kv == SWA_KV_HEADS - 1:
            store_stage(j)
        yield


def _attn_kernel(sink_ref, qe_ref, ke_ref, kd_ref, dec_ref, v_ref, sr_ref, gn_ref,
                 q_ref, kc_ref, kp_ref, vc_ref, vp_ref, bias_ref, og_ref, os_ref,
                 st_ref, sall_ref, *, layer):
    first = pl.program_id(1) == 0

    @pl.when(first)
    def _():
        st_ref[...] = jnp.zeros_like(st_ref)

    gens = [(_gla_gen(qe_ref, ke_ref, kd_ref, dec_ref, v_ref, sr_ref, gn_ref, og_ref, st_ref, sall_ref),
             ATT_GLA_STEPS),
            (_swa_gen(sink_ref, q_ref, kc_ref, kp_ref, vc_ref, vp_ref, bias_ref, os_ref, first, layer),
             ATT_SWA_STEPS)]
    while gens:
        for item in list(gens):
            g, n = item
            for _ in range(n):
                if next(g, StopIteration) is StopIteration:
                    gens.remove(item)
                    break


def _attn(sinks, qe, ke, kd, dec, gv, sr, gn, sq, sk, sv, bias, layer, batch, seq):
    W = SWA_WINDOW
    tl = min(ATT_T, seq)
    nt = seq // tl
    bpt = tl // W
    bps = seq // W
    row = lambda w: pl.BlockSpec((tl, w), lambda b, t: (b * nt + t, 0))
    prev = pl.BlockSpec((W, 2 * SWA_KW), lambda b, t: (b * bps + jnp.maximum(t * bpt - 1, 0), 0))
    return pl.pallas_call(
        functools.partial(_attn_kernel, layer=layer),
        grid=(batch, nt),
        in_specs=[pl.BlockSpec(memory_space=pltpu.SMEM),
                  row(GLA_QK), row(GLA_QK), row(GLA_QK),
                  pl.BlockSpec((tl // GLA_CHUNK, GLA_QK), lambda b, t: (b * nt + t, 0)),
                  row(GLA_VW), row(GLA_VW),
                  pl.BlockSpec((None, 1, GLA_DV), lambda b, t: (layer, 0, 0)),
                  row(SWA_QW), row(2 * SWA_KW), prev, row(2 * SWA_KW), prev,
                  pl.BlockSpec(bias.shape, lambda b, t: (0, 0, 0))],
        out_specs=[row(GLA_VW), row(SWA_QW)],
        out_shape=[jax.ShapeDtypeStruct((batch * seq, GLA_VW), BF16),
                   jax.ShapeDtypeStruct((batch * seq, SWA_QW), BF16)],
        scratch_shapes=[
            pltpu.VMEM((GLA_HEADS // 2, GLA_DV, LANES), F32),
            pltpu.VMEM((tl // GLA_CHUNK, GLA_HEADS // 2, GLA_DV, LANES), BF16),
        ],
        compiler_params=pltpu.CompilerParams(
            dimension_semantics=("parallel", "arbitrary"), vmem_limit_bytes=VMEM_LIMIT),
        name="attn",
    )(sinks, qe, ke, kd, dec, gv, sr, gn, sq, sk, sk, sv, sv, bias)


def _swa_bias_tables():
    W = SWA_WINDOW
    i = np.arange(W)[:, None]
    j = np.arange(W)[None, :]
    dist = np.where(j <= i, i - j, W + i - j).astype(np.float32)
    slopes = _alibi_slopes(SWA_HEADS)
    base = (-slopes[:, None, None] * dist[None] * LOG2E).astype(np.float32)
    first = np.where((j <= i)[None], base, -np.inf).astype(np.float32)
    return np.concatenate([base, first], axis=0)


def _load_bf16(w_hbm, w_ref, stage_ref, sem_ref):
    rows = stage_ref.shape[1]
    n_chunks = w_ref.shape[0] // rows

    def chunk_copy(i, slot):
        return pltpu.make_async_copy(
            w_hbm.at[pl.ds(pl.multiple_of(i * rows, rows), rows)], stage_ref.at[slot], sem_ref.at[slot])

    chunk_copy(0, 0).start()

    def body(i, carry):
        slot = i % 2

        @pl.when(i + 1 < n_chunks)
        def _():
            chunk_copy(i + 1, 1 - slot).start()

        chunk_copy(i, slot).wait()
        w_ref[pl.ds(pl.multiple_of(i * rows, rows), rows), :] = stage_ref[slot].astype(BF16)
        return carry

    lax.fori_loop(0, n_chunks, body, 0)


def _ffn_kernel(x_ref, og_ref, os_ref, wo_hbm, g_ref, wu_hbm, cw_ref, wd_hbm, y_ref,
                x1_ref, h_ref, t3_ref, ubuf_ref, gbuf_ref, halo_ref,
                wo_ref, wu_ref, wd_ref, wide_stage_ref, stage_ref, sem_ref, *, layer):
    tm = x_ref.shape[0]
    S = SUBLANES
    nv = tm // S
    nb = nv // S
    ng = D_MODEL // LANES
    first = pl.program_id(1) == 0

    @pl.when((pl.program_id(0) == 0) & first)
    def _():
        _load_bf16(wo_hbm.at[layer], wo_ref, stage_ref, sem_ref)
        _load_bf16(wu_hbm.at[layer], wu_ref, wide_stage_ref, sem_ref)
        _load_bf16(wd_hbm.at[layer], wd_ref, stage_ref, sem_ref)

    x1 = (x_ref[...]
          + jnp.dot(og_ref[...], wo_ref[:GLA_VW, :], preferred_element_type=F32)
          + jnp.dot(os_ref[...], wo_ref[GLA_VW:, :], preferred_element_type=F32))
    x1_ref[...] = x1
    ms = jnp.mean(x1 * x1, axis=-1, keepdims=True)
    hn = x1 * lax.rsqrt(ms + EPS) * g_ref[...]
    for a in range(S):
        for b in range(nb):
            src = slice(S * (nb * a + b), S * (nb * a + b) + S)
            dst = slice(S * (S * b + a), S * (S * b + a) + S)
            for g in range(ng):
                t3_ref[g, dst, :] = hn[src, g * LANES:(g + 1) * LANES]
    for k in range(nv // 2):
        rows = []
        for v in (2 * k, 2 * k + 1):
            b, c = v // S, v % S
            rows.append(jnp.concatenate(
                [t3_ref[g, pl.ds(S * S * b + c, S, stride=S), :] for g in range(ng)], axis=1))
        h_ref[2 * S * k:2 * S * (k + 1), :] = jnp.concatenate(rows, axis=0).astype(BF16)

    sub = lax.broadcasted_iota(jnp.int32, (S, 2 * FFN_FC), 0)

    def up_stage(c):
        slot = c % 2
        h = h_ref[...]
        u = jnp.concatenate([
            jnp.dot(h, wu_ref[:, c * FFN_FC:(c + 1) * FFN_FC], preferred_element_type=F32),
            jnp.dot(h, wu_ref[:, D_FF + c * FFN_FC:D_FF + (c + 1) * FFN_FC],
                    preferred_element_type=F32)], axis=1)
        halo = jnp.where(first, 0.0, halo_ref[c])
        halo_ref[c] = u[tm - 2 * S:, :]
        fix2 = jnp.where(sub == 0, pltpu.roll(halo[:S], 1, axis=0),
                         pltpu.roll(u[tm - 2 * S:tm - S], 1, axis=0))
        fix1 = jnp.where(sub == 0, pltpu.roll(halo[S:], 1, axis=0),
                         pltpu.roll(u[tm - S:], 1, axis=0))
        ubuf_ref[slot, 0:S, :] = fix2
        ubuf_ref[slot, S:2 * S, :] = fix1
        ubuf_ref[slot, 2 * S:2 * S + tm, :] = u

    def gate_stage(c):
        slot = c % 2
        ca = slice(c * FFN_FC, (c + 1) * FFN_FC)
        cb = slice(D_FF + c * FFN_FC, D_FF + (c + 1) * FFN_FC)
        cw = jnp.concatenate([cw_ref[:, ca], cw_ref[:, cb]], axis=1)
        for r in range(0, tm, FFN_RB):
            u0 = ubuf_ref[slot, 2 * S + r:2 * S + r + FFN_RB, :]
            u1 = ubuf_ref[slot, S + r:S + r + FFN_RB, :]
            u2 = ubuf_ref[slot, r:r + FFN_RB, :]
            y = u2 * cw[0:1] + u1 * cw[1:2] + u0 * cw[2:3] + cw[3:4]
            a = y[:, :FFN_FC]
            gate = (a / (1.0 + jnp.exp(-a))) * y[:, FFN_FC:]
            gbuf_ref[r:r + FFN_RB, ca] = gate.astype(BF16)

    up_stage(0)
    for c in range(FFN_NC):
        if c + 1 < FFN_NC:
            up_stage(c + 1)
        gate_stage(c)
    d = jnp.dot(gbuf_ref[...], wd_ref[...], preferred_element_type=F32)
    for g in range(ng):
        t3_ref[g] = d[:, g * LANES:(g + 1) * LANES]

    for a in range(S):
        for b in range(nb):
            rows = slice(S * (nb * a + b), S * (nb * a + b) + S)
            ffn = jnp.concatenate(
                [t3_ref[g, pl.ds(S * S * b + a, S, stride=S), :] for g in range(ng)], axis=1)
            y_ref[rows, :] = x1_ref[rows, :] + ffn


def _out_ffn(x2, o_gla, o_swa, wo, g, wu, cw, wd, layer, batch, seq):
    tm = min(FFN_TM, seq)
    assert tm % (SUBLANES * SUBLANES) == 0
    nt = seq // tm
    row = lambda w: pl.BlockSpec((tm, w), lambda b, t: (b * nt + t, 0))
    lay = lambda a: pl.BlockSpec((None,) + a.shape[1:], lambda b, t: (layer,) + (0,) * (a.ndim - 1),
                                 pipeline_mode=pl.Buffered(1))
    hbm = pl.BlockSpec(memory_space=pl.ANY)
    return pl.pallas_call(
        functools.partial(_ffn_kernel, layer=layer),
        grid=(batch, nt),
        in_specs=[row(D_MODEL), row(GLA_VW), row(SWA_QW), hbm, lay(g), hbm, lay(cw), hbm],
        out_specs=row(D_MODEL),
        out_shape=jax.ShapeDtypeStruct((batch * seq, D_MODEL), F32),
        scratch_shapes=[
            pltpu.VMEM((tm, D_MODEL), F32),
            pltpu.VMEM((tm, D_MODEL), BF16),
            pltpu.VMEM((D_MODEL // LANES, tm, LANES), F32),
            pltpu.VMEM((2, tm + 2 * SUBLANES, 2 * FFN_FC), F32),
            pltpu.VMEM((tm, D_FF), BF16),
            pltpu.VMEM((FFN_NC, 2 * SUBLANES, 2 * FFN_FC), F32),
            pltpu.VMEM((D_MODEL, D_MODEL), BF16),
            pltpu.VMEM((D_MODEL, 2 * D_FF), BF16),
            pltpu.VMEM((D_FF, D_MODEL), BF16),
            pltpu.VMEM((2, W_STAGE_ROWS // 2, 2 * D_FF), F32),
            pltpu.VMEM((2, 2 * W_STAGE_ROWS, D_MODEL), F32),
            pltpu.SemaphoreType.DMA((2,)),
        ],
        compiler_params=pltpu.CompilerParams(
            dimension_semantics=("arbitrary", "arbitrary"), vmem_limit_bytes=VMEM_LIMIT),
        name="out_ffn",
    )(x2, o_gla, o_swa, wo, g, wu, cw, wd)


def kernel(x, mix_norm, w_in, w_alpha2, b_alpha, gla_norm, q_norm, k_norm, sinks, w_out, ffn_norm,
           w_up, conv_w, conv_b, w_down):
    batch, seq, d = x.shape
    depth = w_in.shape[0]
    assert d == D_MODEL and w_in.shape[2] == P_IN
    assert all(seq % min(t, seq) == 0 for t in (ATT_T, FFN_TM))
    assert seq % IN_SUB == 0 and (batch * seq) % min(IN_TM, batch * seq) == 0

    w_in_t = jnp.swapaxes(w_in, 1, 2)
    wa_pad = jnp.concatenate(
        [w_alpha2, jnp.zeros((depth, LANES - GLA_LOWRANK, GLA_QK), w_alpha2.dtype)],
        axis=1).astype(BF16)
    cw = jnp.concatenate([
        conv_w, conv_b[:, None, :],
        jnp.zeros((depth, SUBLANES - CONV_K - 1, 2 * D_FF), conv_w.dtype)], axis=1)
    row3 = lambda a: a.reshape(depth, 1, a.shape[-1])
    qn2 = row3(jnp.tile(q_norm, (1, 2)))
    kn2 = row3(jnp.tile(k_norm, (1, 2)))

    pos = np.arange(GLA_TRI)
    tri = jnp.asarray(
        (pos[:, None] >= pos[None, :]) & (pos[:, None] // GLA_CHUNK == pos[None, :] // GLA_CHUNK),
        dtype=BF16)
    bias = jnp.asarray(_swa_bias_tables())

    x2 = x.reshape(batch * seq, D_MODEL)
    for l in range(depth):
        qe, ke, kd, dec, gv, sr, sq, sk, sv = _in_proj(
            x2, row3(mix_norm), w_in_t, wa_pad, row3(b_alpha), qn2, kn2, tri, l)
        o_gla, o_swa = _attn(sinks, qe, ke, kd, dec, gv, sr, row3(gla_norm), sq, sk, sv, bias,
                             l, batch, seq)
        x2 = _out_ffn(x2, o_gla, o_swa, w_out, row3(ffn_norm), w_up, cw, w_down, l, batch, seq)
    return x2.reshape(batch, seq, D_MODEL)
```
